```python
import math
import jax
import jax.numpy as jnp
from jax import lax
import numpy as np

D_MODEL = 1024
BATCH = 16
SEQ = 256
DEPTH = 4
DEC_BATCH = 8
DEC_SEQ = 2048
PAST_LEN = 256

GRID_W = 64
N_MIXERS = 4
CHUNK = 128
EPS = 1e-6
D_FF = 4 * D_MODEL

RET_HEADS = 4
RET_DK = D_MODEL // RET_HEADS
RET_DV = 2 * RET_DK
RET_VW = RET_HEADS * RET_DV

ATT_HEADS = 8
ATT_KV = 2
ATT_HD = D_MODEL // ATT_HEADS
ROPE_THETA = 10000.0

ML_HEADS = 4
ML_DV = D_MODEL // ML_HEADS
ML_DQK = ML_DV // 2

HY_BANDS = 16
HY_EMB = 1 + 2 * HY_BANDS
HY_FILT_HID = 64
HY_SHORT = 3
HY_FAST_DECAY = 0.3
HY_SLOW_DECAY = 1.5
HY_TARGET = 1e-2

N_RET = (DEPTH + 3) // N_MIXERS
N_ATT = (DEPTH + 2) // N_MIXERS
N_ML = (DEPTH + 1) // N_MIXERS
N_HY = DEPTH // N_MIXERS

kernel_name = 'hybrid_diffusion_interleaved_step'

F32 = jnp.float32


def _rmsnorm(x, g):
    xf = x.astype(F32)
    y = xf * lax.rsqrt(jnp.mean(xf * xf, axis=-1, keepdims=True) + EPS)
    return (y * g.astype(F32)).astype(x.dtype)


def _adaln(cond, w, b):
    mod = (jax.nn.silu(cond) @ w + b)[:, None, :]
    return jnp.split(mod, 6, axis=-1)


def _mod_norm(x, g, shift, scale):
    return _rmsnorm(x, g) * (1 + scale) + shift


def _to_chunks(a):
    B, L = a.shape[:2]
    return jnp.moveaxis(a.reshape(B, L // CHUNK, CHUNK, *a.shape[2:]), 1, 0)


def _from_chunks(a):
    a = jnp.moveaxis(a, 0, 1)
    return a.reshape(a.shape[0], a.shape[1] * a.shape[2], *a.shape[3:])


def _retention_scan(q, k, v, log_gamma, s0):
    idx = jnp.arange(CHUNK, dtype=F32)
    rel = idx[:, None] - idx[None, :]
    lg = log_gamma[:, None, None]
    intra = jnp.where(rel >= 0, jnp.exp(lg * jnp.maximum(rel, 0.0)), 0.0)
    q_dec = jnp.exp(log_gamma[None, :] * (idx[:, None] + 1.0))[None, :, :, None]
    k_dec = jnp.exp(log_gamma[None, :] * (CHUNK - 1.0 - idx[:, None]))[None, :, :, None]
    c_dec = jnp.exp(log_gamma * CHUNK)[None, :, None, None]

    def step(s, inp):
        qc, kc, vc = inp
        sc = jnp.einsum('bqhd,bkhd->bhqk', qc, kc) * intra
        o = jnp.einsum('bhqk,bkhe->bqhe', sc, vc) + jnp.einsum('bqhd,bhde->bqhe', qc, s) * q_dec
        s = s * c_dec + jnp.einsum('bkhd,bkhe->bhde', kc * k_dec, vc)
        return s, o

    s, o = lax.scan(step, s0, (_to_chunks(q), _to_chunks(k), _to_chunks(v)))
    return _from_chunks(o), s


def _retention(h, w_in, dec_f, dec_b, gn, w_out, s0_f, s0_b):
    B, L, _ = h.shape
    qk_w = RET_HEADS * RET_DK
    q, k, v, g = jnp.split(h @ w_in, [qk_w, 2 * qk_w, 2 * qk_w + RET_VW], axis=-1)
    q = q.reshape(B, L, RET_HEADS, RET_DK).astype(F32) * (RET_DK ** -0.5)
    k = k.reshape(B, L, RET_HEADS, RET_DK).astype(F32)
    v = v.reshape(B, L, RET_HEADS, RET_DV).astype(F32)
    lg_f = jax.nn.log_sigmoid(dec_f.astype(F32))
    lg_b = jax.nn.log_sigmoid(dec_b.astype(F32))
    o_f, s_f = _retention_scan(q, k, v, lg_f, s0_f.astype(F32))
    o_b, s_b = _retention_scan(q[:, ::-1], k[:, ::-1], v[:, ::-1], lg_b, s0_b.astype(F32))
    o = o_f + o_b[:, ::-1]
    o = _rmsnorm(o, gn.reshape(RET_HEADS, RET_DV)).reshape(B, L, RET_VW).astype(h.dtype)
    return (jax.nn.silu(g) * o) @ w_out, s_f, s_b


def _axial_rope(L):
    rows = L // GRID_W
    row = jnp.repeat(jnp.arange(rows, dtype=F32), GRID_W)
    col = jnp.tile(jnp.arange(GRID_W, dtype=F32), rows)
    half = ATT_HD // 2
    inv = ROPE_THETA ** (-jnp.arange(0, half, 2, dtype=F32) / half)
    ang = jnp.concatenate([row[:, None] * inv, col[:, None] * inv], axis=-1)
    return jnp.cos(ang), jnp.sin(ang)


def _rope(x, cos, sin):
    xp = x.astype(F32).reshape(*x.shape[:-1], ATT_HD // 2, 2)
    x0, x1 = xp[..., 0], xp[..., 1]
    c = cos[None, :, None, :]
    s = sin[None, :, None, :]
    return jnp.stack([x0 * c - x1 * s, x0 * s + x1 * c], axis=-1).reshape(x.shape).astype(x.dtype)


def _gqa_qkv(h, w_in, qg, kg):
    B, L, _ = h.shape
    qw = ATT_HEADS * ATT_HD
    kw = ATT_KV * ATT_HD
    q, k, v = jnp.split(h @ w_in, [qw, qw + kw], axis=-1)
    q = _rmsnorm(q.reshape(B, L, ATT_HEADS, ATT_HD), qg)
    k = _rmsnorm(k.reshape(B, L, ATT_KV, ATT_HD), kg)
    return q, k, v.reshape(B, L, ATT_KV, ATT_HD)


def _block_attention(q, k, v):
    B, Lq, H, hd = q.shape
    qb = q.astype(F32).reshape(B, Lq // CHUNK, CHUNK, ATT_KV, H // ATT_KV, hd).transpose(1, 0, 2, 3, 4, 5)
    kf = k.astype(F32)
    vf = v.astype(F32)

    def one(qblk):
        s = jnp.einsum('bqkgd,bskd->bkgqs', qblk, kf) * (hd ** -0.5)
        p = jax.nn.softmax(s, axis=-1)
        return jnp.einsum('bkgqs,bskd->bqkgd', p, vf)

    o = lax.map(one, qb)
    return o.transpose(1, 0, 2, 3, 4, 5).reshape(B, Lq, H * hd)


def _mlstm_scan(q, k, v, ig, lf, c0, n0, m0):
    tril = jnp.tril(jnp.ones((CHUNK, CHUNK), dtype=bool))

    def step(carry, inp):
        cmat, nvec, m = carry
        qc, kc, vc, ic, fc = inp
        b = jnp.cumsum(fc, axis=1).transpose(0, 2, 1)
        it = ic.transpose(0, 2, 1)
        dlog = jnp.where(tril, b[..., :, None] - b[..., None, :] + it[..., None, :], -jnp.inf)
        inter = b + m[..., None]
        m_row = jnp.maximum(inter, jnp.max(dlog, axis=-1))
        w_intra = jnp.exp(dlog - m_row[..., None])
        w_x = jnp.exp(inter - m_row)
        s = jnp.einsum('bqhd,bshd->bhqs', qc, kc) * w_intra
        num = (jnp.einsum('bhqs,bshe->bqhe', s, vc)
               + jnp.einsum('bqhd,bhde->bqhe', qc, cmat) * w_x.transpose(0, 2, 1)[..., None])
        den = jnp.sum(s, axis=-1) + jnp.einsum('bqhd,bhd->bhq', qc, nvec) * w_x
        den = jnp.maximum(jnp.abs(den), jnp.exp(-m_row)).transpose(0, 2, 1)
        hc = num / den[..., None]
        b_end = b[..., -1]
        wlog = b_end[..., None] - b + it
        m_new = jnp.maximum(b_end + m, jnp.max(wlog, axis=-1))
        carry_dec = jnp.exp(b_end + m - m_new)
        kw = kc * jnp.exp(wlog - m_new[..., None]).transpose(0, 2, 1)[..., None]
        cmat = cmat * carry_dec[..., None, None] + jnp.einsum('bshd,bshe->bhde', kw, vc)
        nvec = nvec * carry_dec[..., None] + jnp.sum(kw, axis=1)
        return (cmat, nvec, m_new), hc

    xs = (_to_chunks(q), _to_chunks(k), _to_chunks(v), _to_chunks(ig), _to_chunks(lf))
    (c, n, m), h = lax.scan(step, (c0, n0, m0), xs)
    return _from_chunks(h), c, n, m


def _mlstm(h, w_in, gate_b, gn, w_out, cf, nf, mf, cb, nb, mb):
    B, L, _ = h.shape
    qw = ML_HEADS * ML_DQK
    vw = ML_HEADS * ML_DV
    q, k, v, o, gates = jnp.split(h @ w_in, [qw, 2 * qw, 2 * qw + vw, 2 * qw + 2 * vw], axis=-1)
    q = q.reshape(B, L, ML_HEADS, ML_DQK).astype(F32)
    k = k.reshape(B, L, ML_HEADS, ML_DQK).astype(F32) * (ML_DQK ** -0.5)
    v = v.reshape(B, L, ML_HEADS, ML_DV).astype(F32)
    gates = gates.astype(F32) + gate_b.astype(F32)
    i_f, f_f, i_b, f_b = jnp.split(gates, 4, axis=-1)
    h_f, cf2, nf2, mf2 = _mlstm_scan(q, k, v, i_f, jax.nn.log_sigmoid(f_f),
                                     cf.astype(F32), nf.astype(F32), mf.astype(F32))
    h_b, cb2, nb2, mb2 = _mlstm_scan(q[:, ::-1], k[:, ::-1], v[:, ::-1], i_b[:, ::-1],
                                     jax.nn.log_sigmoid(f_b[:, ::-1]),
                                     cb.astype(F32), nb.astype(F32), mb.astype(F32))
    hh = _rmsnorm(h_f + h_b[:, ::-1], gn.reshape(ML_HEADS, ML_DV)).reshape(B, L, vw).astype(h.dtype)
    return (jax.nn.sigmoid(o) * hh) @ w_out, (cf2, nf2, mf2, cb2, nb2, mb2)


def _hyena_filters(L, w1, b1, w2, b2, w3, freq):
    t = jnp.linspace(0.0, 1.0, L, dtype=F32)[:, None]
    pos = jnp.arange(L, dtype=F32)[:, None]
    bands = jnp.linspace(1e-4, HY_BANDS - 1, HY_BANDS, dtype=F32)
    ang = 2.0 * math.pi * pos * bands / L
    feats = jnp.concatenate([t, jnp.cos(ang), -jnp.sin(ang)], axis=-1)
    z = jnp.sin(freq[0] * (feats @ w1 + b1))
    z = jnp.sin(freq[1] * (z @ w2 + b2))
    filt = (z @ w3).astype(F32)
    deltas = jnp.abs(jnp.linspace(math.log(HY_TARGET) / HY_SLOW_DECAY,
                                  math.log(HY_TARGET) / HY_FAST_DECAY, D_MODEL, dtype=F32))
    window = jnp.exp(-t * deltas)
    return filt[:, :D_MODEL] * window, filt[:, D_MODEL:] * window


def _fft_conv(u, filt):
    L = u.shape[1]
    U = jnp.fft.rfft(u, n=2 * L, axis=1)
    Hf = jnp.fft.rfft(filt, n=2 * L, axis=0)
    return jnp.fft.irfft(U * Hf[None], n=2 * L, axis=1)[:, :L]


def _short_conv(u, w, b):
    L = u.shape[1]
    pad = HY_SHORT // 2
    up = jnp.pad(u, ((0, 0), (pad, pad), (0, 0)))
    out = b
    for tap in range(HY_SHORT):
        out = out + up[:, tap:tap + L] * w[tap]
    return out


def _hyena(h, w_in, b_in, conv_w, conv_b, f_w1, f_b1, f_w2, f_b2, f_w3, freq, f_bias, w_out):
    L = h.shape[1]
    u = _short_conv(h @ w_in + b_in, conv_w, conv_b)
    x0, x1, v = jnp.split(u, 3, axis=-1)
    filt_f, filt_b = _hyena_filters(L, f_w1, f_b1, f_w2, f_b2, f_w3, freq)
    z = (v * x1).astype(F32)
    z = _fft_conv(z, filt_f) + _fft_conv(z[:, ::-1], filt_b)[:, ::-1] + z * f_bias.astype(F32)
    return (z.astype(h.dtype) * x0) @ w_out


def _mlp(h, w1, w2):
    return jnp.square(jax.nn.relu(h @ w1)) @ w2


def setup_inputs(seed: int = 0) -> dict:
    key = jax.random.key(seed)
    ks = iter(jax.random.split(key, 96))

    def nrm(shape, scale):
        return scale * jax.random.normal(next(ks), shape, F32)

    def gain(shape):
        return 1.0 + 0.05 * jax.random.normal(next(ks), shape, F32)

    D = D_MODEL
    gamma = 1.0 - 2.0 ** (-5.0 - np.arange(RET_HEADS))
    dec_logit = jnp.asarray(np.log(gamma / (1.0 - gamma)), F32)
    f_bias = jnp.linspace(3.0, 6.0, ML_HEADS, dtype=F32)
    ml_w = 2 * ML_HEADS * ML_DQK + 2 * ML_HEADS * ML_DV + 4 * ML_HEADS
    return {
        'x_prompt': nrm((BATCH, SEQ, D), 1.0),
        'x_sample': nrm((DEC_BATCH, DEC_SEQ, D), 1.0),
        'cache_k': nrm((DEC_BATCH, N_ATT, PAST_LEN, ATT_KV, ATT_HD), 1.0),
        'cache_v': nrm((DEC_BATCH, N_ATT, PAST_LEN, ATT_KV, ATT_HD), 1.0),
        'state_ret_fwd': nrm((DEC_BATCH, N_RET, RET_HEADS, RET_DK, RET_DV), 1.0),
        'state_ret_bwd': nrm((DEC_BATCH, N_RET, RET_HEADS, RET_DK, RET_DV), 1.0),
        'state_ml_C_fwd': nrm((DEC_BATCH, N_ML, ML_HEADS, ML_DQK, ML_DV), 0.3),
        'state_ml_n_fwd': nrm((DEC_BATCH, N_ML, ML_HEADS, ML_DQK), 0.3),
        'state_ml_m_fwd': 1.0 + nrm((DEC_BATCH, N_ML, ML_HEADS), 0.5),
        'state_ml_C_bwd': nrm((DEC_BATCH, N_ML, ML_HEADS, ML_DQK, ML_DV), 0.3),
        'state_ml_n_bwd': nrm((DEC_BATCH, N_ML, ML_HEADS, ML_DQK), 0.3),
        'state_ml_m_bwd': 1.0 + nrm((DEC_BATCH, N_ML, ML_HEADS), 0.5),
        'c': nrm((DEC_BATCH, D), 1.0),
        'c_ctx': nrm((D,), 1.0),
        'mod_w': nrm((DEPTH, D, 6 * D), 0.5 * D ** -0.5),
        'mod_b': nrm((DEPTH, 6 * D), 0.02),
        'norm_mix_pre': gain((DEPTH, D)),
        'norm_mix_post': gain((DEPTH, D)),
        'norm_ffn_pre': gain((DEPTH, D)),
        'norm_ffn_post': gain((DEPTH, D)),
        'mlp_w1': nrm((DEPTH, D, D_FF), D ** -0.5),
        'mlp_w2': nrm((DEPTH, D_FF, D), D_FF ** -0.5),
        'ret_w_in': nrm((N_RET, D, 2 * RET_HEADS * RET_DK + 2 * RET_VW), D ** -0.5),
        'ret_decay_fwd': dec_logit[None] + nrm((N_RET, RET_HEADS), 0.01),
        'ret_decay_bwd': dec_logit[None] + nrm((N_RET, RET_HEADS), 0.01),
        'ret_gn': gain((N_RET, RET_VW)),
        'ret_w_out': nrm((N_RET, RET_VW, D), RET_VW ** -0.5),
        'att_w_in': nrm((N_ATT, D, (ATT_HEADS + 2 * ATT_KV) * ATT_HD), D ** -0.5),
        'att_q_gain': gain((N_ATT, ATT_HD)),
        'att_k_gain': gain((N_ATT, ATT_HD)),
        'att_w_out': nrm((N_ATT, ATT_HEADS * ATT_HD, D), (ATT_HEADS * ATT_HD) ** -0.5),
        'ml_w_in': nrm((N_ML, D, ml_w), D ** -0.5),
        'ml_gate_b': jnp.concatenate([nrm((N_ML, ML_HEADS), 0.1), f_bias[None] + nrm((N_ML, ML_HEADS), 0.1),
                                      nrm((N_ML, ML_HEADS), 0.1), f_bias[None] + nrm((N_ML, ML_HEADS), 0.1)], axis=-1),
        'ml_gn': gain((N_ML, ML_HEADS * ML_DV)),
        'ml_w_out': nrm((N_ML, ML_HEADS * ML_DV, D), (ML_HEADS * ML_DV) ** -0.5),
        'hy_w_in': nrm((N_HY, D, 3 * D), D ** -0.5),
        'hy_b_in': nrm((N_HY, 3 * D), 0.02),
        'hy_conv_w': nrm((N_HY, HY_SHORT, 3 * D), HY_SHORT ** -0.5),
        'hy_conv_b': nrm((N_HY, 3 * D), 0.02),
        'hy_f_w1': nrm((N_HY, HY_EMB, HY_FILT_HID), HY_EMB ** -0.5),
        'hy_f_b1': nrm((N_HY, HY_FILT_HID), 0.1),
        'hy_f_w2': nrm((N_HY, HY_FILT_HID, HY_FILT_HID), HY_FILT_HID ** -0.5),
        'hy_f_b2': nrm((N_HY, HY_FILT_HID), 0.1),
        'hy_f_w3': nrm((N_HY, HY_FILT_HID, 2 * D), HY_FILT_HID ** -0.5),
        'hy_sin_freq': 1.0 + nrm((N_HY, 2, HY_FILT_HID), 0.1),
        'hy_f_bias': nrm((N_HY, D), 0.1),
        'hy_w_out': nrm((N_HY, D, D), D ** -0.5),
    }


def reference(x_prompt, x_sample, cache_k, cache_v, state_ret_fwd, state_ret_bwd,
              state_ml_C_fwd, state_ml_n_fwd, state_ml_m_fwd, state_ml_C_bwd, state_ml_n_bwd, state_ml_m_bwd,
              c, c_ctx, mod_w, mod_b, norm_mix_pre, norm_mix_post, norm_ffn_pre, norm_ffn_post,
              mlp_w1, mlp_w2, ret_w_in, ret_decay_fwd, ret_decay_bwd, ret_gn, ret_w_out,
              att_w_in, att_q_gain, att_k_gain, att_w_out, ml_w_in, ml_gate_b, ml_gn, ml_w_out,
              hy_w_in, hy_b_in, hy_conv_w, hy_conv_b, hy_f_w1, hy_f_b1, hy_f_w2, hy_f_b2, hy_f_w3,
              hy_sin_freq, hy_f_bias, hy_w_out):
    bp = x_prompt.shape[0]
    sdt = x_prompt.dtype
    xp, xs = x_prompt, x_sample
    cond_p = c_ctx[None, :]
    rope_cos, rope_sin = _axial_rope(x_sample.shape[1])
    new_k, new_v, new_rf, new_rb = [], [], [], []
    new_cf, new_nf, new_mf, new_cb, new_nb, new_mb = [], [], [], [], [], []
    for i in range(DEPTH):
        mixer = i % N_MIXERS
        j = i // N_MIXERS
        sh1p, sc1p, g1p, sh2p, sc2p, g2p = _adaln(cond_p, mod_w[i], mod_b[i])
        sh1s, sc1s, g1s, sh2s, sc2s, g2s = _adaln(c, mod_w[i], mod_b[i])
        hp = _mod_norm(xp, norm_mix_pre[i], sh1p, sc1p)
        hs = _mod_norm(xs, norm_mix_pre[i], sh1s, sc1s)
        if mixer == 0:
            z0 = jnp.zeros((bp, RET_HEADS, RET_DK, RET_DV), F32)
            yp, s_f, s_b = _retention(hp, ret_w_in[j], ret_decay_fwd[j], ret_decay_bwd[j], ret_gn[j],
                                      ret_w_out[j], z0, z0)
            ys, _, _ = _retention(hs, ret_w_in[j], ret_decay_fwd[j], ret_decay_bwd[j], ret_gn[j],
                                  ret_w_out[j], state_ret_fwd[:, j], state_ret_bwd[:, j])
            new_rf.append(s_f.astype(sdt))
            new_rb.append(s_b.astype(sdt))
        elif mixer == 1:
            qp, kp, vp = _gqa_qkv(hp, att_w_in[j], att_q_gain[j], att_k_gain[j])
            yp = _block_attention(qp, kp, vp).astype(hp.dtype) @ att_w_out[j]
            qs, ks_, vs = _gqa_qkv(hs, att_w_in[j], att_q_gain[j], att_k_gain[j])
            qs = _rope(qs, rope_cos, rope_sin)
            ks_ = _rope(ks_, rope_cos, rope_sin)
            k_all = jnp.concatenate([ks_, cache_k[:, j].astype(ks_.dtype)], axis=1)
            v_all = jnp.concatenate([vs, cache_v[:, j].astype(vs.dtype)], axis=1)
            ys = _block_attention(qs, k_all, v_all).astype(hs.dtype) @ att_w_out[j]
            new_k.append(kp)
            new_v.append(vp)
        elif mixer == 2:
            zc = jnp.zeros((bp, ML_HEADS, ML_DQK, ML_DV), F32)
            zn = jnp.zeros((bp, ML_HEADS, ML_DQK), F32)
            zm = jnp.zeros((bp, ML_HEADS), F32)
            yp, st = _mlstm(hp, ml_w_in[j], ml_gate_b[j], ml_gn[j], ml_w_out[j], zc, zn, zm, zc, zn, zm)
            ys, _ = _mlstm(hs, ml_w_in[j], ml_gate_b[j], ml_gn[j], ml_w_out[j],
                           state_ml_C_fwd[:, j], state_ml_n_fwd[:, j], state_ml_m_fwd[:, j],
                           state_ml_C_bwd[:, j], state_ml_n_bwd[:, j], state_ml_m_bwd[:, j])
            new_cf.append(st[0].astype(sdt))
            new_nf.append(st[1].astype(sdt))
            new_mf.append(st[2].astype(sdt))
            new_cb.append(st[3].astype(sdt))
            new_nb.append(st[4].astype(sdt))
            new_mb.append(st[5].astype(sdt))
        else:
            yp = _hyena(hp, hy_w_in[j], hy_b_in[j], hy_conv_w[j], hy_conv_b[j], hy_f_w1[j], hy_f_b1[j],
                        hy_f_w2[j], hy_f_b2[j], hy_f_w3[j], hy_sin_freq[j], hy_f_bias[j], hy_w_out[j])
            ys = _hyena(hs, hy_w_in[j], hy_b_in[j], hy_conv_w[j], hy_conv_b[j], hy_f_w1[j], hy_f_b1[j],
                        hy_f_w2[j], hy_f_b2[j], hy_f_w3[j], hy_sin_freq[j], hy_f_bias[j], hy_w_out[j])
        xp = xp + g1p * _rmsnorm(yp, norm_mix_post[i])
        xs = xs + g1s * _rmsnorm(ys, norm_mix_post[i])
        hp = _mod_norm(xp, norm_ffn_pre[i], sh2p, sc2p)
        hs = _mod_norm(xs, norm_ffn_pre[i], sh2s, sc2s)
        xp = xp + g2p * _rmsnorm(_mlp(hp, mlp_w1[i], mlp_w2[i]), norm_ffn_post[i])
        xs = xs + g2s * _rmsnorm(_mlp(hs, mlp_w1[i], mlp_w2[i]), norm_ffn_post[i])
    new_cache_k = jnp.stack(new_k, axis=1)
    new_cache_v = jnp.stack(new_v, axis=1)
    new_state_ret_fwd = jnp.stack(new_rf, axis=1)
    new_state_ret_bwd = jnp.stack(new_rb, axis=1)
    new_state_ml_C_fwd = jnp.stack(new_cf, axis=1)
    new_state_ml_n_fwd = jnp.stack(new_nf, axis=1)
    new_state_ml_m_fwd = jnp.stack(new_mf, axis=1)
    new_state_ml_C_bwd = jnp.stack(new_cb, axis=1)
    new_state_ml_n_bwd = jnp.stack(new_nb, axis=1)
    new_state_ml_m_bwd = jnp.stack(new_mb, axis=1)
    return (xp, xs, new_cache_k, new_cache_v, new_state_ret_fwd, new_state_ret_bwd,
            new_state_ml_C_fwd, new_state_ml_n_fwd, new_state_ml_m_fwd,
            new_state_ml_C_bwd, new_state_ml_n_bwd, new_state_ml_m_bwd)
```

```python
import functools
import math

import jax
import jax.numpy as jnp
import numpy as np
from jax import lax
from jax.experimental import pallas as pl
from jax.experimental.pallas import tpu as pltpu

F32 = jnp.float32
BF16 = jnp.bfloat16

EPS = 1e-6
CHUNK = 128
GRID_W = 64
ROPE_THETA = 10000.0

RET_HEADS = 4
ATT_HEADS = 8
ATT_KV = 2
ATT_HD = 128
ML_HEADS = 4

HY_BANDS = 16
HY_SHORT = 3
HY_FAST_DECAY = 0.3
HY_SLOW_DECAY = 1.5
HY_TARGET = 1e-2

VMEM_LIMIT_BYTES = 56 * 1024 * 1024
LANES = 128


def _params(*sem):
    return pltpu.CompilerParams(dimension_semantics=sem, vmem_limit_bytes=VMEM_LIMIT_BYTES)


def _dot(a, b):
    return jnp.dot(a, b, preferred_element_type=F32)


def _dot_nt(a, b):
    return lax.dot_general(a, b, (((1,), (1,)), ((), ())), preferred_element_type=F32)


def _dot_tn(a, b):
    return lax.dot_general(a, b, (((0,), (0,)), ((), ())), preferred_element_type=F32)


def _rms(x, g):
    return x * lax.rsqrt(jnp.mean(x * x, axis=-1, keepdims=True) + EPS) * g


def _log_sigmoid(x):
    return jnp.minimum(x, 0.0) - jnp.log1p(jnp.exp(-jnp.abs(x)))


class Group:
    def __init__(self, b, l, row0, per_batch):
        self.b, self.l, self.row0, self.per_batch = b, l, row0, per_batch
        self.t = b * l

    def mod_spec(self, chunk, tm, d):
        row0, per_batch, l = self.row0, self.per_batch, self.l
        if per_batch:
            return pl.BlockSpec((1, 1, d), lambda i, *_: (row0 + (i * tm) // l, 0, chunk))
        return pl.BlockSpec((1, 1, d), lambda i, *_: (row0, 0, chunk))


def _adaln_kernel(c_ref, w_ref, b_ref, o_ref):
    s = jax.nn.silu(c_ref[...])
    o_ref[0] = _dot(s.astype(BF16), w_ref[0].astype(BF16)) + b_ref[0]


def adaln_all(cond, mod_w, mod_b):
    depth, d, n = mod_w.shape
    rows = cond.shape[0]
    tn = 768
    return pl.pallas_call(
        _adaln_kernel,
        grid=(depth, n // tn),
        in_specs=[
            pl.BlockSpec((rows, d), lambda l, j: (0, 0)),
            pl.BlockSpec((1, d, tn), lambda l, j: (l, 0, j)),
            pl.BlockSpec((1, 1, tn), lambda l, j: (l, 0, j)),
        ],
        out_specs=pl.BlockSpec((1, rows, tn), lambda l, j: (l, 0, j)),
        out_shape=jax.ShapeDtypeStruct((depth, rows, n), F32),
        compiler_params=_params("parallel", "parallel"),
        name="adaln",
    )(cond, mod_w, mod_b.reshape(depth, 1, n))


def _norm_mm_kernel(x_ref, g_ref, sh_ref, sc_ref, w_ref, *rest, has_bias):
    if has_bias:
        b_ref, o_ref, h_scr = rest
    else:
        o_ref, h_scr = rest

    @pl.when(pl.program_id(1) == 0)
    def _():
        y = _rms(x_ref[...], g_ref[...])
        h_scr[...] = (y * (1.0 + sc_ref[0]) + sh_ref[0]).astype(BF16)

    acc = _dot(h_scr[...], w_ref[...])
    if has_bias:
        acc = acc + b_ref[...]
    o_ref[...] = acc.astype(o_ref.dtype)


def norm_matmul(x, gain, mod3, sh_idx, sc_idx, w, bias, out_dtype, grp, tm, tn):
    t, d = x.shape
    n = w.shape[1]
    in_specs = [
        pl.BlockSpec((tm, d), lambda i, j: (i, 0)),
        pl.BlockSpec((1, d), lambda i, j: (0, 0)),
        grp.mod_spec(sh_idx, tm, d),
        grp.mod_spec(sc_idx, tm, d),
        pl.BlockSpec((d, tn), lambda i, j: (0, j)),
    ]
    args = [x, gain.reshape(1, d), mod3, mod3, w]
    if bias is not None:
        in_specs.append(pl.BlockSpec((1, tn), lambda i, j: (0, j)))
        args.append(bias.reshape(1, n))
    return pl.pallas_call(
        functools.partial(_norm_mm_kernel, has_bias=bias is not None),
        grid=(t // tm, n // tn),
        in_specs=in_specs,
        out_specs=pl.BlockSpec((tm, tn), lambda i, j: (i, j)),
        out_shape=jax.ShapeDtypeStruct((t, n), out_dtype),
        scratch_shapes=[pltpu.VMEM((tm, d), BF16)],
        compiler_params=_params("parallel", "arbitrary"),
        name="norm_matmul",
    )(*args)


def _mm_res_kernel(a_ref, w_ref, x_ref, gate_ref, pg_ref, o_ref):
    y = _dot(a_ref[...], w_ref[...])
    o_ref[...] = x_ref[...] + gate_ref[0] * _rms(y, pg_ref[...])


def matmul_resnorm(a, w, x, mod3, gate_idx, post_gain, grp, tm):
    t, k = a.shape
    d = w.shape[1]
    return pl.pallas_call(
        _mm_res_kernel,
        grid=(t // tm,),
        in_specs=[
            pl.BlockSpec((tm, k), lambda i: (i, 0)),
            pl.BlockSpec((k, d), lambda i: (0, 0)),
            pl.BlockSpec((tm, d), lambda i: (i, 0)),
            grp.mod_spec(gate_idx, tm, d),
            pl.BlockSpec((1, d), lambda i: (0, 0)),
        ],
        out_specs=pl.BlockSpec((tm, d), lambda i: (i, 0)),
        out_shape=jax.ShapeDtypeStruct((t, d), F32),
        compiler_params=_params("parallel"),
        name="out_proj",
    )(a, w, x, mod3, post_gain.reshape(1, d))


def _mlp_kernel(x_ref, g_ref, sh_ref, sc_ref, w1_ref, w2_ref, gate_ref, pg_ref, o_ref, h_scr, acc_scr):
    j = pl.program_id(1)

    @pl.when(j == 0)
    def _():
        y = _rms(x_ref[...], g_ref[...])
        h_scr[...] = (y * (1.0 + sc_ref[0]) + sh_ref[0]).astype(BF16)
        acc_scr[...] = jnp.zeros_like(acc_scr)

    u = _dot(h_scr[...], w1_ref[...])
    u = jnp.square(jnp.maximum(u, 0.0)).astype(BF16)
    acc_scr[...] += _dot(u, w2_ref[...])

    @pl.when(j == pl.num_programs(1) - 1)
    def _():
        o_ref[...] = x_ref[...] + gate_ref[0] * _rms(acc_scr[...], pg_ref[...])


def mlp_block(x, pre_gain, post_gain, mod3, w1, w2, grp, tm, tf):
    t, d = x.shape
    f = w1.shape[1]
    return pl.pallas_call(
        _mlp_kernel,
        grid=(t // tm, f // tf),
        in_specs=[
            pl.BlockSpec((tm, d), lambda i, j: (i, 0)),
            pl.BlockSpec((1, d), lambda i, j: (0, 0)),
            grp.mod_spec(3, tm, d),
            grp.mod_spec(4, tm, d),
            pl.BlockSpec((d, tf), lambda i, j: (0, j)),
            pl.BlockSpec((tf, d), lambda i, j: (j, 0)),
            grp.mod_spec(5, tm, d),
            pl.BlockSpec((1, d), lambda i, j: (0, 0)),
        ],
        out_specs=pl.BlockSpec((tm, d), lambda i, j: (i, 0)),
        out_shape=jax.ShapeDtypeStruct((t, d), F32),
        scratch_shapes=[pltpu.VMEM((tm, d), BF16), pltpu.VMEM((tm, d), F32)],
        compiler_params=_params("parallel", "arbitrary"),
        name="mlp",
    )(x, pre_gain.reshape(1, d), mod3, mod3, w1, w2, mod3, post_gain.reshape(1, d))


def _ret_kernel(dec_ref, q_ref, k_ref, v_ref, g_ref, gn_ref, *rest, n_chunks, has_state, emit_state):
    rest = list(rest)
    s0f_ref = s0b_ref = sf_ref = sb_ref = None
    if has_state:
        s0f_ref, s0b_ref = rest[:2]
        rest = rest[2:]
    o_ref = rest.pop(0)
    if emit_state:
        sf_ref, sb_ref = rest[:2]
        rest = rest[2:]
    st_f, st_b, of_scr, ob_scr = rest

    c = CHUNK
    dk = q_ref.shape[-1]
    h = pl.program_id(1)
    lg_f = _log_sigmoid(jnp.full((1, 1), dec_ref[0, h], F32))
    lg_b = _log_sigmoid(jnp.full((1, 1), dec_ref[1, h], F32))
    ri = lax.broadcasted_iota(jnp.int32, (c, c), 0)
    ci = lax.broadcasted_iota(jnp.int32, (c, c), 1)
    rel = (ri - ci).astype(F32)
    intra_f = jnp.where(rel >= 0, jnp.exp(lg_f * jnp.maximum(rel, 0.0)), 0.0)
    intra_b = jnp.where(rel <= 0, jnp.exp(lg_b * jnp.maximum(-rel, 0.0)), 0.0)
    idx = lax.broadcasted_iota(jnp.int32, (c, 1), 0).astype(F32)
    qdec_f = jnp.exp(lg_f * (idx + 1.0))
    kdec_f = jnp.exp(lg_f * (c - 1.0 - idx))
    qdec_b = jnp.exp(lg_b * (c - idx))
    kdec_b = jnp.exp(lg_b * idx)
    cdec_f = jnp.exp(lg_f * c)
    cdec_b = jnp.exp(lg_b * c)
    q_scale = dk ** -0.5

    if has_state:
        st_f[...] = s0f_ref[0, 0, 0]
        st_b[...] = s0b_ref[0, 0, 0]
    else:
        st_f[...] = jnp.zeros_like(st_f)
        st_b[...] = jnp.zeros_like(st_b)

    def step(ch, st, out_scr, intra, qdec, kdec, cdec):
        rows = pl.ds(pl.multiple_of(ch * c, c), c)
        qc = q_ref[0, rows, :]
        kc = k_ref[0, rows, :]
        vc = v_ref[0, rows, :]
        s_old = st[...]
        sc = _dot_nt(qc, kc) * (intra * q_scale)
        o = _dot(sc.astype(BF16), vc) + _dot(qc, s_old.astype(BF16)) * (qdec * q_scale)
        kw = (kc.astype(F32) * kdec).astype(BF16)
        st[...] = s_old * cdec + _dot_tn(kw, vc)
        out_scr[rows, :] = o

    def body(i, carry):
        step(i, st_f, of_scr, intra_f, qdec_f, kdec_f, cdec_f)
        step(n_chunks - 1 - i, st_b, ob_scr, intra_b, qdec_b, kdec_b, cdec_b)
        return carry

    lax.fori_loop(0, n_chunks, body, 0)

    if emit_state:
        sf_ref[0, 0] = st_f[...]
        sb_ref[0, 0] = st_b[...]

    def fin(i, carry):
        rows = pl.ds(pl.multiple_of(i * c, c), c)
        o = _rms(of_scr[rows, :] + ob_scr[rows, :], gn_ref[...])
        o_ref[0, rows, :] = (jax.nn.silu(g_ref[0, rows, :].astype(F32)) * o).astype(o_ref.dtype)
        return carry

    lax.fori_loop(0, n_chunks, fin, 0)


def retention_core(proj, dec, gn, s0f, s0b, b, l, emit_state):
    hh = RET_HEADS
    width = proj.shape[-1]
    dk = width // (6 * hh)
    dv = 2 * dk
    has_state = s0f is not None
    n_chunks = l // CHUNK
    in_specs = [
        pl.BlockSpec(memory_space=pltpu.SMEM),
        pl.BlockSpec((1, l, dk), lambda bi, h: (bi, 0, h)),
        pl.BlockSpec((1, l, dk), lambda bi, h: (bi, 0, hh + h)),
        pl.BlockSpec((1, l, dv), lambda bi, h: (bi, 0, hh + h)),
        pl.BlockSpec((1, l, dv), lambda bi, h: (bi, 0, 2 * hh + h)),
        pl.BlockSpec((1, dv), lambda bi, h: (0, h)),
    ]
    args = [dec, proj, proj, proj, proj, gn.reshape(1, hh * dv)]
    if has_state:
        st_spec = pl.BlockSpec((1, 1, 1, dk, dv), lambda bi, h: (bi, 0, h, 0, 0))
        in_specs += [st_spec, st_spec]
        args += [s0f, s0b]
    out_specs = [pl.BlockSpec((1, l, dv), lambda bi, h: (bi, 0, h))]
    out_shape = [jax.ShapeDtypeStruct((b, l, hh * dv), BF16)]
    if emit_state:
        so_spec = pl.BlockSpec((1, 1, dk, dv), lambda bi, h: (bi, h, 0, 0))
        out_specs += [so_spec, so_spec]
        out_shape += [jax.ShapeDtypeStruct((b, hh, dk, dv), F32)] * 2
    return pl.pallas_call(
        functools.partial(_ret_kernel, n_chunks=n_chunks, has_state=has_state, emit_state=emit_state),
        grid=(b, hh),
        in_specs=in_specs,
        out_specs=out_specs,
        out_shape=out_shape,
        scratch_shapes=[pltpu.VMEM((dk, dv), F32), pltpu.VMEM((dk, dv), F32),
                        pltpu.VMEM((l, dv), F32), pltpu.VMEM((l, dv), F32)],
        compiler_params=_params("parallel", "parallel"),
        name="retention",
    )(*args)


def _rope_rot(x, cos_t, sin_t):
    lane = lax.broadcasted_iota(jnp.int32, x.shape, x.ndim - 1)
    nxt = pltpu.roll(x, LANES - 1, axis=x.ndim - 1)
    prv = pltpu.roll(x, 1, axis=x.ndim - 1)
    swapped = jnp.where(jnp.bitwise_and(lane, 1) == 0, nxt, prv)
    return x * cos_t + swapped * sin_t


def _att_kernel(q_ref, k_ref, v_ref, qg_ref, kg_ref, *rest, rope, has_cache, emit_kv, groups):
    rest = list(rest)
    cosq_ref = sinq_ref = cosk_ref = sink_ref = ck_ref = cv_ref = kn_ref = vo_ref = None
    if rope:
        cosq_ref, sinq_ref, cosk_ref, sink_ref = rest[:4]
        rest = rest[4:]
    if has_cache:
        ck_ref, cv_ref = rest[:2]
        rest = rest[2:]
    o_ref = rest.pop(0)
    if emit_kv:
        kn_ref, vo_ref = rest[:2]
        rest = rest[2:]
    k_scr, v_scr = rest
    hd = ATT_HD
    scale = hd ** -0.5

    @pl.when(pl.program_id(2) == 0)
    def _():
        kn = _rms(k_ref[0], kg_ref[...])
        if emit_kv:
            kn_ref[0] = kn
            vo_ref[0] = v_ref[0]
        if rope:
            kn = _rope_rot(kn, cosk_ref[...], sink_ref[...])
        k_scr[...] = kn.astype(BF16)
        v_scr[...] = v_ref[0].astype(BF16)

    for g in range(groups):
        cols = slice(g * hd, (g + 1) * hd)
        qn = _rms(q_ref[0, :, cols], qg_ref[...])
        if rope:
            qn = _rope_rot(qn, cosq_ref[...], sinq_ref[...])
        qb = qn.astype(BF16)
        s1 = _dot_nt(qb, k_scr[...]) * scale
        m = jnp.max(s1, axis=-1, keepdims=True)
        if has_cache:
            s2 = _dot_nt(qb, ck_ref[0].astype(BF16)) * scale
            m = jnp.maximum(m, jnp.max(s2, axis=-1, keepdims=True))
        p1 = jnp.exp(s1 - m)
        den = jnp.sum(p1, axis=-1, keepdims=True)
        num = _dot(p1.astype(BF16), v_scr[...])
        if has_cache:
            p2 = jnp.exp(s2 - m)
            den = den + jnp.sum(p2, axis=-1, keepdims=True)
            num = num + _dot(p2.astype(BF16), cv_ref[0].astype(BF16))
        o_ref[0, :, cols] = (num / den).astype(o_ref.dtype)


def attention_core(proj, q_gain, k_gain, rope_tabs, cache_k, cache_v, b, l, tq, emit_kv):
    hd, kv, heads = ATT_HD, ATT_KV, ATT_HEADS
    groups = heads // kv
    rope = rope_tabs is not None
    has_cache = cache_k is not None
    in_specs = [
        pl.BlockSpec((1, tq, groups * hd), lambda bi, kh, qi: (bi, qi, kh)),
        pl.BlockSpec((1, l, hd), lambda bi, kh, qi: (bi, 0, heads + kh)),
        pl.BlockSpec((1, l, hd), lambda bi, kh, qi: (bi, 0, heads + kv + kh)),
        pl.BlockSpec((1, hd), lambda bi, kh, qi: (0, 0)),
        pl.BlockSpec((1, hd), lambda bi, kh, qi: (0, 0)),
    ]
    args = [proj, proj, proj, q_gain.reshape(1, hd), k_gain.reshape(1, hd)]
    if rope:
        cos_t, sin_t = rope_tabs
        in_specs += [
            pl.BlockSpec((tq, hd), lambda bi, kh, qi: (qi, 0)),
            pl.BlockSpec((tq, hd), lambda bi, kh, qi: (qi, 0)),
            pl.BlockSpec((l, hd), lambda bi, kh, qi: (0, 0)),
            pl.BlockSpec((l, hd), lambda bi, kh, qi: (0, 0)),
        ]
        args += [cos_t, sin_t, cos_t, sin_t]
    if has_cache:
        past = cache_k.shape[1]
        c_spec = pl.BlockSpec((1, past, hd), lambda bi, kh, qi: (bi, 0, kh))
        in_specs += [c_spec, c_spec]
        args += [cache_k, cache_v]
    out_specs = [pl.BlockSpec((1, tq, groups * hd), lambda bi, kh, qi: (bi, qi, kh))]
    out_shape = [jax.ShapeDtypeStruct((b, l, heads * hd), BF16)]
    if emit_kv:
        kv_spec = pl.BlockSpec((1, l, hd), lambda bi, kh, qi: (bi, 0, kh))
        out_specs += [kv_spec, kv_spec]
        out_shape += [jax.ShapeDtypeStruct((b, l, kv * hd), F32)] * 2
    return pl.pallas_call(
        functools.partial(_att_kernel, rope=rope, has_cache=has_cache, emit_kv=emit_kv, groups=groups),
        grid=(b, kv, l // tq),
        in_specs=in_specs,
        out_specs=out_specs,
        out_shape=out_shape,
        scratch_shapes=[pltpu.VMEM((l, hd), BF16), pltpu.VMEM((l, hd), BF16)],
        compiler_params=_params("parallel", "parallel", "arbitrary"),
        name="attention",
    )(*args)


def _rope_tables(l):
    rows = l // GRID_W
    row = jnp.repeat(jnp.arange(rows, dtype=F32), GRID_W)
    col = jnp.tile(jnp.arange(GRID_W, dtype=F32), rows)
    half = ATT_HD // 2
    inv = ROPE_THETA ** (-jnp.arange(0, half, 2, dtype=F32) / half)
    ang = jnp.concatenate([row[:, None] * inv, col[:, None] * inv], axis=-1)
    cos_t = jnp.repeat(jnp.cos(ang), 2, axis=-1)
    sin_h = jnp.sin(ang)
    sin_t = jnp.stack([-sin_h, sin_h], axis=-1).reshape(l, ATT_HD)
    return cos_t, sin_t


def _ml_kernel(q_ref, k_ref, v_ref, og_ref, gt_ref, gc_ref, gn_ref, *rest, n_chunks, has_state, emit_state):
    rest = list(rest)
    c0f_ref = n0f_ref = m0f_ref = c0b_ref = n0b_ref = m0b_ref = None
    if has_state:
        c0f_ref, n0f_ref, m0f_ref, c0b_ref, n0b_ref, m0b_ref = rest[:6]
        rest = rest[6:]
    o_ref = rest.pop(0)
    outs = None
    if emit_state:
        outs = rest[:6]
        rest = rest[6:]
    cm_f, cm_b, nv_f, nv_b, mm_f, mm_b, hf_scr, hb_scr = rest

    c = CHUNK
    dqk = q_ref.shape[-1]
    k_scale = dqk ** -0.5
    ri = lax.broadcasted_iota(jnp.int32, (c, c), 0)
    ci = lax.broadcasted_iota(jnp.int32, (c, c), 1)
    mask_f = ci <= ri
    mask_b = ci >= ri
    head = pl.program_id(1)
    n_heads = pl.num_programs(1)
    lane = lax.broadcasted_iota(jnp.int32, (c, LANES), 1)

    if has_state:
        cm_f[...] = c0f_ref[0, 0, 0]
        cm_b[...] = c0b_ref[0, 0, 0]
        nv_f[...] = n0f_ref[0, 0]
        nv_b[...] = n0b_ref[0, 0]
        mm_f[...] = m0f_ref[0, 0]
        mm_b[...] = m0b_ref[0, 0]
    else:
        for r in (cm_f, cm_b, nv_f, nv_b, mm_f, mm_b):
            r[...] = jnp.zeros_like(r)

    def step(ch, cm, nv, mm, out_scr, mask, mask_t, row_i, row_f):
        off = pl.multiple_of(ch * c, c)
        rows = pl.ds(off, c)
        qc = q_ref[0, rows, :]
        kc = k_ref[0, rows, :]
        vc = v_ref[0, rows, :]
        g_rows = gt_ref[0, 0, :, pl.ds(off, c)]
        g_cols = gc_ref[0, rows, :]
        i_row = g_rows[row_i:row_i + 1, :]
        f_row = _log_sigmoid(g_rows[row_f:row_f + 1, :])
        i_col = jnp.sum(jnp.where(lane == row_i * n_heads + head, g_cols, 0.0), axis=-1, keepdims=True)
        f_col = _log_sigmoid(
            jnp.sum(jnp.where(lane == row_f * n_heads + head, g_cols, 0.0), axis=-1, keepdims=True))
        b_col = jnp.sum(jnp.where(mask, f_row, 0.0), axis=-1, keepdims=True)
        b_row = jnp.sum(jnp.where(mask_t, f_col, 0.0), axis=0, keepdims=True)
        m_old = mm[...]
        dlog = jnp.where(mask, b_col - b_row + i_row, -jnp.inf)
        inter = b_col + m_old
        m_row = jnp.maximum(inter, jnp.max(dlog, axis=-1, keepdims=True))
        w_intra = jnp.exp(dlog - m_row)
        w_x = jnp.exp(inter - m_row)
        s = _dot_nt(qc, kc) * (w_intra * k_scale)
        c_old = cm[...]
        n_old = nv[...]
        num = _dot(s.astype(BF16), vc) + _dot(qc, c_old.astype(BF16)) * w_x
        qn = jnp.sum(qc.astype(F32) * n_old, axis=-1, keepdims=True)
        den = jnp.sum(s, axis=-1, keepdims=True) + qn * w_x
        den = jnp.maximum(jnp.abs(den), jnp.exp(-m_row))
        out_scr[rows, :] = num / den
        b_end = jnp.sum(f_row, axis=-1, keepdims=True)
        wlog = b_end - b_col + i_col
        m_new = jnp.maximum(b_end + m_old, jnp.max(wlog, axis=0, keepdims=True))
        carry_dec = jnp.exp(b_end + m_old - m_new)
        kw = kc.astype(F32) * (jnp.exp(wlog - m_new) * k_scale)
        cm[...] = c_old * carry_dec + _dot_tn(kw.astype(BF16), vc)
        nv[...] = n_old * carry_dec + jnp.sum(kw, axis=0, keepdims=True)
        mm[...] = m_new

    def body(i, carry):
        step(i, cm_f, nv_f, mm_f, hf_scr, mask_f, mask_b, 0, 1)
        step(n_chunks - 1 - i, cm_b, nv_b, mm_b, hb_scr, mask_b, mask_f, 2, 3)
        return carry

    lax.fori_loop(0, n_chunks, body, 0)

    if emit_state:
        for dst, src in zip(outs, (cm_f, nv_f, mm_f, cm_b, nv_b, mm_b)):
            dst[0, 0] = src[...]

    def fin(i, carry):
        rows = pl.ds(pl.multiple_of(i * c, c), c)
        hn = _rms(hf_scr[rows, :] + hb_scr[rows, :], gn_ref[...])
        o_ref[0, rows, :] = (jax.nn.sigmoid(og_ref[0, rows, :].astype(F32)) * hn).astype(o_ref.dtype)
        return carry

    lax.fori_loop(0, n_chunks, fin, 0)


def mlstm_core(proj, gates_t, gates_c, gn, state, b, l, emit_state):
    hh = ML_HEADS
    width = proj.shape[-1]
    dqk = width // (6 * hh)
    dv = 2 * dqk
    has_state = state is not None
    n_chunks = l // CHUNK
    in_specs = [
        pl.BlockSpec((1, l, dqk), lambda bi, h: (bi, 0, h)),
        pl.BlockSpec((1, l, dqk), lambda bi, h: (bi, 0, hh + h)),
        pl.BlockSpec((1, l, dv), lambda bi, h: (bi, 0, hh + h)),
        pl.BlockSpec((1, l, dv), lambda bi, h: (bi, 0, 2 * hh + h)),
        pl.BlockSpec((1, 1, 8, l), lambda bi, h: (bi, h, 0, 0)),
        pl.BlockSpec((1, l, LANES), lambda bi, h: (bi, 0, 0)),
        pl.BlockSpec((1, dv), lambda bi, h: (0, h)),
    ]
    args = [proj, proj, proj, proj, gates_t, gates_c, gn.reshape(1, hh * dv)]
    c_spec_in = pl.BlockSpec((1, 1, 1, dqk, dv), lambda bi, h: (bi, 0, h, 0, 0))
    n_spec = pl.BlockSpec((1, 1, 1, dqk), lambda bi, h: (bi, h, 0, 0))
    m_spec = pl.BlockSpec((1, 1, 1, 1), lambda bi, h: (bi, h, 0, 0))
    if has_state:
        cf, nf, mf, cb, nb, mb = state
        in_specs += [c_spec_in, n_spec, m_spec, c_spec_in, n_spec, m_spec]
        args += [cf, nf.reshape(b, hh, 1, dqk), mf.reshape(b, hh, 1, 1),
                 cb, nb.reshape(b, hh, 1, dqk), mb.reshape(b, hh, 1, 1)]
    out_specs = [pl.BlockSpec((1, l, dv), lambda bi, h: (bi, 0, h))]
    out_shape = [jax.ShapeDtypeStruct((b, l, hh * dv), BF16)]
    if emit_state:
        c_spec_out = pl.BlockSpec((1, 1, dqk, dv), lambda bi, h: (bi, h, 0, 0))
        out_specs += [c_spec_out, n_spec, m_spec] * 2
        out_shape += [jax.ShapeDtypeStruct((b, hh, dqk, dv), F32),
                      jax.ShapeDtypeStruct((b, hh, 1, dqk), F32),
                      jax.ShapeDtypeStruct((b, hh, 1, 1), F32)] * 2
    return pl.pallas_call(
        functools.partial(_ml_kernel, n_chunks=n_chunks, has_state=has_state, emit_state=emit_state),
        grid=(b, hh),
        in_specs=in_specs,
        out_specs=out_specs,
        out_shape=out_shape,
        scratch_shapes=[pltpu.VMEM((dqk, dv), F32), pltpu.VMEM((dqk, dv), F32),
                        pltpu.VMEM((1, dqk), F32), pltpu.VMEM((1, dqk), F32),
                        pltpu.VMEM((1, 1), F32), pltpu.VMEM((1, 1), F32),
                        pltpu.VMEM((l, dv), F32), pltpu.VMEM((l, dv), F32)],
        compiler_params=_params("parallel", "parallel"),
        name="mlstm",
    )(*args)


def _hy_filter_kernel(feat_ref, w1_ref, b1_ref, w2_ref, b2_ref, w3_ref, fr_ref, win_ref, sum_ref, dif_ref):
    d = win_ref.shape[-1]
    z = jnp.sin(fr_ref[0:1, :] * (_dot(feat_ref[...].astype(BF16), w1_ref[...].astype(BF16)) + b1_ref[...]))
    z = jnp.sin(fr_ref[1:2, :] * (_dot(z.astype(BF16), w2_ref[...].astype(BF16)) + b2_ref[...]))
    filt = _dot(z.astype(BF16), w3_ref[...].astype(BF16))
    win = win_ref[...]
    ff = filt[:, :d] * win
    fb = filt[:, d:] * win
    sum_ref[...] = (ff + fb).astype(sum_ref.dtype)
    dif_ref[...] = (ff - fb).astype(dif_ref.dtype)


def hyena_filters(l, d, w1, b1, w2, b2, w3, freq):
    t = jnp.linspace(0.0, 1.0, l, dtype=F32)[:, None]
    pos = jnp.arange(l, dtype=F32)[:, None]
    bands = jnp.linspace(1e-4, HY_BANDS - 1, HY_BANDS, dtype=F32)
    ang = 2.0 * math.pi * pos * bands / l
    feats = jnp.concatenate([t, jnp.cos(ang), -jnp.sin(ang)], axis=-1)
    emb = feats.shape[1]
    feats = jnp.pad(feats, ((0, 0), (0, LANES - emb)))
    w1p = jnp.pad(w1, ((0, LANES - emb), (0, 0)))
    deltas = jnp.abs(jnp.linspace(math.log(HY_TARGET) / HY_SLOW_DECAY,
                                  math.log(HY_TARGET) / HY_FAST_DECAY, d, dtype=F32))
    window = jnp.exp(-t * deltas)
    hid = w2.shape[0]
    tl = min(l, 512)
    full = lambda shp: pl.BlockSpec(shp, lambda i: (0,) * len(shp))
    return pl.pallas_call(
        _hy_filter_kernel,
        grid=(l // tl,),
        in_specs=[
            pl.BlockSpec((tl, LANES), lambda i: (i, 0)),
            full((LANES, hid)), full((1, hid)), full((hid, hid)), full((1, hid)), full((hid, 2 * d)),
            full((2, hid)),
            pl.BlockSpec((tl, d), lambda i: (i, 0)),
        ],
        out_specs=[pl.BlockSpec((tl, d), lambda i: (i, 0))] * 2,
        out_shape=[jax.ShapeDtypeStruct((l, d), BF16)] * 2,
        compiler_params=_params("parallel"),
        name="hyena_filter",
    )(feats, w1p, b1.reshape(1, hid), w2, b2.reshape(1, hid), w3, freq, window)


def _dft_tables(l):
    n = 2 * l
    k = lax.broadcasted_iota(jnp.int32, (l, l), 0)
    t = lax.broadcasted_iota(jnp.int32, (l, l), 1)
    phase = ((2 * k + 1) * t) % (2 * n)
    ang = phase.astype(F32) * (math.pi / n)
    cos_kt = jnp.cos(ang).astype(BF16)
    sin_kt = jnp.sin(ang).astype(BF16)
    return cos_kt, sin_kt, cos_kt.T, sin_kt.T


def _hy_spec_kernel(c_ref, s_ref, fs_ref, fd_ref, bias_ref, gr_ref, gs_ref):
    gr_ref[...] = _dot(c_ref[...], fs_ref[...]) + bias_ref[...]
    gs_ref[...] = _dot(s_ref[...], fd_ref[...])


def hyena_filter_spectrum(cos_kt, sin_kt, f_sum, f_dif, f_bias):
    l, d = f_sum.shape
    tm = min(l, 512)
    tn = min(d, 512)
    return pl.pallas_call(
        _hy_spec_kernel,
        grid=(l // tm, d // tn),
        in_specs=[
            pl.BlockSpec((tm, l), lambda i, j: (i, 0)),
            pl.BlockSpec((tm, l), lambda i, j: (i, 0)),
            pl.BlockSpec((l, tn), lambda i, j: (0, j)),
            pl.BlockSpec((l, tn), lambda i, j: (0, j)),
            pl.BlockSpec((1, tn), lambda i, j: (0, j)),
        ],
        out_specs=[pl.BlockSpec((tm, tn), lambda i, j: (i, j))] * 2,
        out_shape=[jax.ShapeDtypeStruct((l, d), F32)] * 2,
        compiler_params=_params("parallel", "parallel"),
        name="hyena_filter_spectrum",
    )(cos_kt, sin_kt, f_sum, f_dif, f_bias.reshape(1, d))


def _hy_conv_kernel(p0_ref, p1_ref, pv_ref, w0_ref, w1_ref, wv_ref, b0_ref, b1_ref, bv_ref, z_ref, x0_ref):
    l = p0_ref.shape[1]
    t = lax.broadcasted_iota(jnp.int32, (l, 1), 0)

    def conv(p_ref, w_ref, b_ref):
        p = p_ref[0].astype(F32)
        prev = jnp.where(t == 0, 0.0, pltpu.roll(p, 1, axis=0))
        nxt = jnp.where(t == l - 1, 0.0, pltpu.roll(p, l - 1, axis=0))
        return b_ref[...] + prev * w_ref[0:1, :] + p * w_ref[1:2, :] + nxt * w_ref[2:3, :]

    x0_ref[0] = conv(p0_ref, w0_ref, b0_ref)
    z_ref[0] = (conv(pv_ref, wv_ref, bv_ref) * conv(p1_ref, w1_ref, b1_ref)).astype(z_ref.dtype)


def hyena_short_conv(proj, conv_w, conv_b, b, l, d):
    tc = min(d, 512)
    nc = d // tc
    p_spec = lambda off: pl.BlockSpec((1, l, tc), lambda bi, j: (bi, 0, off * nc + j))
    w_spec = lambda off: pl.BlockSpec((HY_SHORT, tc), lambda bi, j: (0, off * nc + j))
    b_spec = lambda off: pl.BlockSpec((1, tc), lambda bi, j: (0, off * nc + j))
    cb = conv_b.reshape(1, 3 * d)
    return pl.pallas_call(
        _hy_conv_kernel,
        grid=(b, nc),
        in_specs=[p_spec(0), p_spec(1), p_spec(2), w_spec(0), w_spec(1), w_spec(2),
                  b_spec(0), b_spec(1), b_spec(2)],
        out_specs=[pl.BlockSpec((1, l, tc), lambda bi, j: (bi, 0, j))] * 2,
        out_shape=[jax.ShapeDtypeStruct((b, l, d), BF16), jax.ShapeDtypeStruct((b, l, d), F32)],
        compiler_params=_params("parallel", "parallel"),
        name="hyena_short_conv",
    )(proj, proj, proj, conv_w, conv_w, conv_w, cb, cb, cb)


def _hy_fwd_kernel(c_ref, s_ref, z_ref, gr_ref, gs_ref, yr_ref, ys_ref):
    z = z_ref[0]
    zr = _dot(c_ref[...], z)
    zs = _dot(s_ref[...], z)
    gr = gr_ref[...]
    gs = gs_ref[...]
    yr_ref[0] = (zr * gr - zs * gs).astype(yr_ref.dtype)
    ys_ref[0] = (zr * gs + zs * gr).astype(ys_ref.dtype)


def hyena_forward_dft(cos_kt, sin_kt, z, g_r, g_s):
    b, l, d = z.shape
    tm = min(l, 512)
    tn = min(d, 512)
    return pl.pallas_call(
        _hy_fwd_kernel,
        grid=(l // tm, b, d // tn),
        in_specs=[
            pl.BlockSpec((tm, l), lambda i, bi, j: (i, 0)),
            pl.BlockSpec((tm, l), lambda i, bi, j: (i, 0)),
            pl.BlockSpec((1, l, tn), lambda i, bi, j: (bi, 0, j)),
            pl.BlockSpec((tm, tn), lambda i, bi, j: (i, j)),
            pl.BlockSpec((tm, tn), lambda i, bi, j: (i, j)),
        ],
        out_specs=[pl.BlockSpec((1, tm, tn), lambda i, bi, j: (bi, i, j))] * 2,
        out_shape=[jax.ShapeDtypeStruct((b, l, d), BF16)] * 2,
        compiler_params=_params("parallel", "parallel", "parallel"),
        name="hyena_forward_dft",
    )(cos_kt, sin_kt, z, g_r, g_s)


def _hy_inv_kernel(ct_ref, st_ref, yr_ref, ys_ref, x0_ref, o_ref, *, inv_scale):
    y = _dot(ct_ref[...], yr_ref[0]) + _dot(st_ref[...], ys_ref[0])
    o_ref[0] = (y * inv_scale * x0_ref[0]).astype(o_ref.dtype)


def hyena_inverse_dft(cos_tk, sin_tk, y_r, y_s, x0):
    b, l, d = y_r.shape
    tm = min(l, 512)
    tn = min(d, 512)
    return pl.pallas_call(
        functools.partial(_hy_inv_kernel, inv_scale=1.0 / l),
        grid=(l // tm, b, d // tn),
        in_specs=[
            pl.BlockSpec((tm, l), lambda i, bi, j: (i, 0)),
            pl.BlockSpec((tm, l), lambda i, bi, j: (i, 0)),
            pl.BlockSpec((1, l, tn), lambda i, bi, j: (bi, 0, j)),
            pl.BlockSpec((1, l, tn), lambda i, bi, j: (bi, 0, j)),
            pl.BlockSpec((1, tm, tn), lambda i, bi, j: (bi, i, j)),
        ],
        out_specs=pl.BlockSpec((1, tm, tn), lambda i, bi, j: (bi, i, j)),
        out_shape=jax.ShapeDtypeStruct((b, l, d), BF16),
        compiler_params=_params("parallel", "parallel", "parallel"),
        name="hyena_inverse_dft",
    )(cos_tk, sin_tk, y_r, y_s, x0)


def _tile_rows(grp):
    span = grp.l if grp.per_batch else grp.t
    return next(tm for tm in (1024, 512, 256, 128) if span % tm == 0)


def kernel(x_prompt, x_sample, cache_k, cache_v, state_ret_fwd, state_ret_bwd, state_ml_C_fwd, state_ml_n_fwd, state_ml_m_fwd, state_ml_C_bwd, state_ml_n_bwd, state_ml_m_bwd, c, c_ctx, mod_w, mod_b, norm_mix_pre, norm_mix_post, norm_ffn_pre, norm_ffn_post, mlp_w1, mlp_w2, ret_w_in, ret_decay_fwd, ret_decay_bwd, ret_gn, ret_w_out, att_w_in, att_q_gain, att_k_gain, att_w_out, ml_w_in, ml_gate_b, ml_gn, ml_w_out, hy_w_in, hy_b_in, hy_conv_w, hy_conv_b, hy_f_w1, hy_f_b1, hy_f_w2, hy_f_b2, hy_f_w3, hy_sin_freq, hy_f_bias, hy_w_out):
    bp, lp, d = x_prompt.shape
    bs, ls, _ = x_sample.shape
    depth = mod_w.shape[0]
    n_mixers = 4
    mod_rows = 16
    assert 1 + bs <= mod_rows

    grp_p = Group(bp, lp, 0, False)
    grp_s = Group(bs, ls, 1, True)
    groups = (grp_p, grp_s)

    cond = jnp.concatenate([c_ctx[None, :], c, jnp.zeros((mod_rows - 1 - bs, d), F32)], axis=0)
    mod_all = adaln_all(cond, mod_w, mod_b)

    xs = [x_prompt.reshape(grp_p.t, d), x_sample.reshape(grp_s.t, d)]
    new_k = new_v = new_rf = new_rb = None
    new_ml = None

    for i in range(depth):
        mixer = i % n_mixers
        j = i // n_mixers
        mod3 = mod_all[i].reshape(mod_rows, 1, 6 * d)
        ys = []
        for gi, grp in enumerate(groups):
            x = xs[gi]
            tm = _tile_rows(grp)
            is_prompt = gi == 0
            if mixer == 0:
                w_in = ret_w_in[j].astype(BF16)
                proj = norm_matmul(x, norm_mix_pre[i], mod3, 0, 1, w_in, None, BF16, grp, tm, 512)
                dec = jnp.stack([ret_decay_fwd[j], ret_decay_bwd[j]]).astype(F32)
                s0f = None if is_prompt else state_ret_fwd
                s0b = None if is_prompt else state_ret_bwd
                assert is_prompt or state_ret_fwd.shape[1] == 1
                res = retention_core(proj.reshape(grp.b, grp.l, -1), dec, ret_gn[j], s0f, s0b,
                                     grp.b, grp.l, emit_state=is_prompt)
                if is_prompt:
                    new_rf, new_rb = res[1][:, None], res[2][:, None]
                a = res[0].reshape(grp.t, -1)
                w_out = ret_w_out[j].astype(BF16)
            elif mixer == 1:
                w_in = att_w_in[j].astype(BF16)
                proj = norm_matmul(x, norm_mix_pre[i], mod3, 0, 1, w_in, None, F32, grp, tm, 512)
                proj = proj.reshape(grp.b, grp.l, -1)
                if is_prompt:
                    res = attention_core(proj, att_q_gain[j], att_k_gain[j], None, None, None,
                                         grp.b, grp.l, min(grp.l, 256), emit_kv=True)
                    new_k = res[1].reshape(grp.b, 1, grp.l, ATT_KV, ATT_HD)
                    new_v = res[2].reshape(grp.b, 1, grp.l, ATT_KV, ATT_HD)
                else:
                    assert cache_k.shape[1] == 1
                    ck = cache_k.reshape(grp.b, cache_k.shape[2], ATT_KV * ATT_HD)
                    cv = cache_v.reshape(grp.b, cache_v.shape[2], ATT_KV * ATT_HD)
                    res = attention_core(proj, att_q_gain[j], att_k_gain[j], _rope_tables(grp.l), ck, cv,
                                         grp.b, grp.l, min(grp.l, 256), emit_kv=False)
                a = res[0].reshape(grp.t, -1)
                w_out = att_w_out[j].astype(BF16)
            elif mixer == 2:
                hh = ML_HEADS
                n_main = ml_w_in.shape[2] - 4 * hh
                w_main = ml_w_in[j][:, :n_main].astype(BF16)
                w_gate = jnp.pad(ml_w_in[j][:, n_main:], ((0, 0), (0, LANES - 4 * hh))).astype(BF16)
                b_gate = jnp.pad(ml_gate_b[j], (0, LANES - 4 * hh))
                proj = norm_matmul(x, norm_mix_pre[i], mod3, 0, 1, w_main, None, BF16, grp, tm, 512)
                gates = norm_matmul(x, norm_mix_pre[i], mod3, 0, 1, w_gate, b_gate, F32, grp, tm, LANES)
                gates_c = gates.reshape(grp.b, grp.l, LANES)
                gates = gates[:, :4 * hh].reshape(grp.b, grp.l, 4, hh)
                gates_t = jnp.pad(gates.transpose(0, 3, 2, 1), ((0, 0), (0, 0), (0, 4), (0, 0)))
                state = None
                if not is_prompt:
                    assert state_ml_C_fwd.shape[1] == 1
                    state = (state_ml_C_fwd, state_ml_n_fwd, state_ml_m_fwd,
                             state_ml_C_bwd, state_ml_n_bwd, state_ml_m_bwd)
                res = mlstm_core(proj.reshape(grp.b, grp.l, -1), gates_t, gates_c, ml_gn[j], state,
                                 grp.b, grp.l, emit_state=is_prompt)
                if is_prompt:
                    dqk = res[2].shape[-1]
                    new_ml = (res[1][:, None], res[2].reshape(grp.b, 1, hh, dqk), res[3].reshape(grp.b, 1, hh),
                              res[4][:, None], res[5].reshape(grp.b, 1, hh, dqk), res[6].reshape(grp.b, 1, hh))
                a = res[0].reshape(grp.t, -1)
                w_out = ml_w_out[j].astype(BF16)
            else:
                w_in = hy_w_in[j].astype(BF16)
                proj = norm_matmul(x, norm_mix_pre[i], mod3, 0, 1, w_in, hy_b_in[j], F32, grp, tm, 512)
                z, x0 = hyena_short_conv(proj.reshape(grp.b, grp.l, 3 * d), hy_conv_w[j], hy_conv_b[j],
                                         grp.b, grp.l, d)
                f_sum, f_dif = hyena_filters(grp.l, d, hy_f_w1[j], hy_f_b1[j], hy_f_w2[j], hy_f_b2[j],
                                             hy_f_w3[j], hy_sin_freq[j])
                cos_kt, sin_kt, cos_tk, sin_tk = _dft_tables(grp.l)
                g_r, g_s = hyena_filter_spectrum(cos_kt, sin_kt, f_sum, f_dif, hy_f_bias[j])
                y_r, y_s = hyena_forward_dft(cos_kt, sin_kt, z, g_r, g_s)
                a = hyena_inverse_dft(cos_tk, sin_tk, y_r, y_s, x0).reshape(grp.t, d)
                w_out = hy_w_out[j].astype(BF16)
            x = matmul_resnorm(a, w_out, x, mod3, 2, norm_mix_post[i], grp, tm)
            x = mlp_block(x, norm_ffn_pre[i], norm_ffn_post[i], mod3,
                          mlp_w1[i].astype(BF16), mlp_w2[i].astype(BF16), grp, tm, 512)
            ys.append(x)
        xs = ys

    y_prompt = xs[0].reshape(bp, lp, d)
    y_sample = xs[1].reshape(bs, ls, d)
    return (y_prompt, y_sample, new_k, new_v, new_rf, new_rb) + tuple(new_ml)
```

```python
import functools
import math

import jax
import jax.numpy as jnp
import numpy as np
from jax import lax
from jax.experimental import pallas as pl
from jax.experimental.pallas import tpu as pltpu

F32 = jnp.float32
BF16 = jnp.bfloat16

EPS = 1e-6
CHUNK = 128
GRID_W = 64
ROPE_THETA = 10000.0

RET_HEADS = 4
ATT_HEADS = 8
ATT_KV = 2
ATT_HD = 128
ML_HEADS = 4

HY_BANDS = 16
HY_SHORT = 3
HY_FAST_DECAY = 0.3
HY_SLOW_DECAY = 1.5
HY_TARGET = 1e-2

VMEM_LIMIT_BYTES = 56 * 1024 * 1024
LANES = 128
PROJ_TILE_N = 1536
MLP_TILE_F = 1024


def _params(*sem):
    return pltpu.CompilerParams(dimension_semantics=sem, vmem_limit_bytes=VMEM_LIMIT_BYTES)


def _dot(a, b):
    return jnp.dot(a, b, preferred_element_type=F32)


def _dot_nt(a, b):
    return lax.dot_general(a, b, (((1,), (1,)), ((), ())), preferred_element_type=F32)


def _dot_tn(a, b):
    return lax.dot_general(a, b, (((0,), (0,)), ((), ())), preferred_element_type=F32)


def _rms(x, g):
    return x * lax.rsqrt(jnp.mean(x * x, axis=-1, keepdims=True) + EPS) * g


def _log_sigmoid(x):
    return jnp.minimum(x, 0.0) - jnp.log1p(jnp.exp(-jnp.abs(x)))


def _two_phase_loop(n_chunks, body, body_fin):
    assert n_chunks % 2 == 0
    half = n_chunks // 2
    unroll = 2 if half % 2 == 0 else 1
    lax.fori_loop(0, half, body, 0, unroll=unroll)
    lax.fori_loop(half, n_chunks, body_fin, 0, unroll=unroll)


class Group:
    def __init__(self, b, l, row0, per_batch):
        self.b, self.l, self.row0, self.per_batch = b, l, row0, per_batch
        self.t = b * l

    def mod_spec(self, chunk, tm, d):
        row0, per_batch, l = self.row0, self.per_batch, self.l
        if per_batch:
            return pl.BlockSpec((1, 1, d), lambda i, *_: (row0 + (i * tm) // l, 0, chunk))
        return pl.BlockSpec((1, 1, d), lambda i, *_: (row0, 0, chunk))


def _adaln_kernel(c_ref, w_ref, b_ref, o_ref):
    s = jax.nn.silu(c_ref[...])
    o_ref[0] = _dot(s.astype(BF16), w_ref[0].astype(BF16)) + b_ref[0]


def adaln_all(cond, mod_w, mod_b):
    depth, d, n = mod_w.shape
    rows = cond.shape[0]
    tn = 768
    return pl.pallas_call(
        _adaln_kernel,
        grid=(depth, n // tn),
        in_specs=[
            pl.BlockSpec((rows, d), lambda l, j: (0, 0)),
            pl.BlockSpec((1, d, tn), lambda l, j: (l, 0, j)),
            pl.BlockSpec((1, 1, tn), lambda l, j: (l, 0, j)),
        ],
        out_specs=pl.BlockSpec((1, rows, tn), lambda l, j: (l, 0, j)),
        out_shape=jax.ShapeDtypeStruct((depth, rows, n), F32),
        compiler_params=_params("parallel", "parallel"),
        name="adaln",
    )(cond, mod_w, mod_b.reshape(depth, 1, n))


def _norm_mm_kernel(x_ref, g_ref, sh_ref, sc_ref, w_ref, *rest, has_bias):
    if has_bias:
        b_ref, o_ref, h_scr = rest
    else:
        o_ref, h_scr = rest

    @pl.when(pl.program_id(1) == 0)
    def _():
        y = _rms(x_ref[...], g_ref[...])
        h_scr[...] = (y * (1.0 + sc_ref[0]) + sh_ref[0]).astype(BF16)

    acc = _dot(h_scr[...], w_ref[...])
    if has_bias:
        acc = acc + b_ref[...]
    o_ref[...] = acc.astype(o_ref.dtype)


def norm_matmul(x, gain, mod3, sh_idx, sc_idx, w, bias, out_dtype, grp, tm, tn):
    t, d = x.shape
    n = w.shape[1]
    in_specs = [
        pl.BlockSpec((tm, d), lambda i, j: (i, 0)),
        pl.BlockSpec((1, d), lambda i, j: (0, 0)),
        grp.mod_spec(sh_idx, tm, d),
        grp.mod_spec(sc_idx, tm, d),
        pl.BlockSpec((d, tn), lambda i, j: (0, j)),
    ]
    args = [x, gain.reshape(1, d), mod3, mod3, w]
    if bias is not None:
        in_specs.append(pl.BlockSpec((1, tn), lambda i, j: (0, j)))
        args.append(bias.reshape(1, n))
    return pl.pallas_call(
        functools.partial(_norm_mm_kernel, has_bias=bias is not None),
        grid=(t // tm, n // tn),
        in_specs=in_specs,
        out_specs=pl.BlockSpec((tm, tn), lambda i, j: (i, j)),
        out_shape=jax.ShapeDtypeStruct((t, n), out_dtype),
        scratch_shapes=[pltpu.VMEM((tm, d), BF16)],
        compiler_params=_params("parallel", "arbitrary"),
        name="norm_matmul",
    )(*args)


def _mm_res_kernel(a_ref, w_ref, x_ref, gate_ref, pg_ref, o_ref):
    y = _dot(a_ref[...], w_ref[...])
    o_ref[...] = x_ref[...] + gate_ref[0] * _rms(y, pg_ref[...])


def matmul_resnorm(a, w, x, mod3, gate_idx, post_gain, grp, tm):
    t, k = a.shape
    d = w.shape[1]
    return pl.pallas_call(
        _mm_res_kernel,
        grid=(t // tm,),
        in_specs=[
            pl.BlockSpec((tm, k), lambda i: (i, 0)),
            pl.BlockSpec((k, d), lambda i: (0, 0)),
            pl.BlockSpec((tm, d), lambda i: (i, 0)),
            grp.mod_spec(gate_idx, tm, d),
            pl.BlockSpec((1, d), lambda i: (0, 0)),
        ],
        out_specs=pl.BlockSpec((tm, d), lambda i: (i, 0)),
        out_shape=jax.ShapeDtypeStruct((t, d), F32),
        compiler_params=_params("parallel"),
        name="out_proj",
    )(a, w, x, mod3, post_gain.reshape(1, d))


def _mlp_kernel(x_ref, g_ref, sh_ref, sc_ref, w1_ref, w2_ref, gate_ref, pg_ref, o_ref, h_scr, acc_scr):
    j = pl.program_id(1)

    @pl.when(j == 0)
    def _():
        y = _rms(x_ref[...], g_ref[...])
        h_scr[...] = (y * (1.0 + sc_ref[0]) + sh_ref[0]).astype(BF16)
        acc_scr[...] = jnp.zeros_like(acc_scr)

    u = _dot(h_scr[...], w1_ref[...])
    u = jnp.square(jnp.maximum(u, 0.0)).astype(BF16)
    acc_scr[...] += _dot(u, w2_ref[...])

    @pl.when(j == pl.num_programs(1) - 1)
    def _():
        o_ref[...] = x_ref[...] + gate_ref[0] * _rms(acc_scr[...], pg_ref[...])


def mlp_block(x, pre_gain, post_gain, mod3, w1, w2, grp, tm, tf):
    t, d = x.shape
    f = w1.shape[1]
    return pl.pallas_call(
        _mlp_kernel,
        grid=(t // tm, f // tf),
        in_specs=[
            pl.BlockSpec((tm, d), lambda i, j: (i, 0)),
            pl.BlockSpec((1, d), lambda i, j: (0, 0)),
            grp.mod_spec(3, tm, d),
            grp.mod_spec(4, tm, d),
            pl.BlockSpec((d, tf), lambda i, j: (0, j)),
            pl.BlockSpec((tf, d), lambda i, j: (j, 0)),
            grp.mod_spec(5, tm, d),
            pl.BlockSpec((1, d), lambda i, j: (0, 0)),
        ],
        out_specs=pl.BlockSpec((tm, d), lambda i, j: (i, 0)),
        out_shape=jax.ShapeDtypeStruct((t, d), F32),
        scratch_shapes=[pltpu.VMEM((tm, d), BF16), pltpu.VMEM((tm, d), F32)],
        compiler_params=_params("parallel", "arbitrary"),
        name="mlp",
    )(x, pre_gain.reshape(1, d), mod3, mod3, w1, w2, mod3, post_gain.reshape(1, d))


def _ret_kernel(dec_ref, q_ref, k_ref, v_ref, g_ref, gn_ref, *rest, n_chunks, has_state, emit_state):
    rest = list(rest)
    s0f_ref = s0b_ref = sf_ref = sb_ref = None
    if has_state:
        s0f_ref, s0b_ref = rest[:2]
        rest = rest[2:]
    o_ref = rest.pop(0)
    if emit_state:
        sf_ref, sb_ref = rest[:2]
        rest = rest[2:]
    st_f, st_b, of_scr, ob_scr = rest

    c = CHUNK
    dk = q_ref.shape[-1]
    h = pl.program_id(1)
    lg_f = _log_sigmoid(jnp.full((1, 1), dec_ref[0, h], F32))
    lg_b = _log_sigmoid(jnp.full((1, 1), dec_ref[1, h], F32))
    ri = lax.broadcasted_iota(jnp.int32, (c, c), 0)
    ci = lax.broadcasted_iota(jnp.int32, (c, c), 1)
    rel = (ri - ci).astype(F32)
    intra_f = jnp.where(rel >= 0, jnp.exp(lg_f * jnp.maximum(rel, 0.0)), 0.0)
    intra_b = jnp.where(rel <= 0, jnp.exp(lg_b * jnp.maximum(-rel, 0.0)), 0.0)
    idx = lax.broadcasted_iota(jnp.int32, (c, 1), 0).astype(F32)
    qdec_f = jnp.exp(lg_f * (idx + 1.0))
    kdec_f = jnp.exp(lg_f * (c - 1.0 - idx))
    qdec_b = jnp.exp(lg_b * (c - idx))
    kdec_b = jnp.exp(lg_b * idx)
    cdec_f = jnp.exp(lg_f * c)
    cdec_b = jnp.exp(lg_b * c)
    q_scale = dk ** -0.5

    if has_state:
        st_f[...] = s0f_ref[0, 0, 0]
        st_b[...] = s0b_ref[0, 0, 0]
    else:
        st_f[...] = jnp.zeros_like(st_f)
        st_b[...] = jnp.zeros_like(st_b)

    def step(ch, st, out_scr, intra, qdec, kdec, cdec):
        rows = pl.ds(pl.multiple_of(ch * c, c), c)
        qc = q_ref[0, rows, :]
        kc = k_ref[0, rows, :]
        vc = v_ref[0, rows, :]
        s_old = st[...]
        sc = _dot_nt(qc, kc) * (intra * q_scale)
        o = _dot(sc.astype(BF16), vc) + _dot(qc, s_old.astype(BF16)) * (qdec * q_scale)
        kw = (kc.astype(F32) * kdec).astype(BF16)
        st[...] = s_old * cdec + _dot_tn(kw, vc)
        out_scr[rows, :] = o

    def fin(ch):
        rows = pl.ds(pl.multiple_of(ch * c, c), c)
        o = _rms(of_scr[rows, :] + ob_scr[rows, :], gn_ref[...])
        o_ref[0, rows, :] = (jax.nn.silu(g_ref[0, rows, :].astype(F32)) * o).astype(o_ref.dtype)

    def body(i, carry):
        step(i, st_f, of_scr, intra_f, qdec_f, kdec_f, cdec_f)
        step(n_chunks - 1 - i, st_b, ob_scr, intra_b, qdec_b, kdec_b, cdec_b)
        return carry

    def body_fin(i, carry):
        body(i, carry)
        fin(i)
        fin(n_chunks - 1 - i)
        return carry

    _two_phase_loop(n_chunks, body, body_fin)

    if emit_state:
        sf_ref[0, 0] = st_f[...]
        sb_ref[0, 0] = st_b[...]


def retention_core(proj, dec, gn, s0f, s0b, b, l, emit_state):
    hh = RET_HEADS
    width = proj.shape[-1]
    dk = width // (6 * hh)
    dv = 2 * dk
    has_state = s0f is not None
    n_chunks = l // CHUNK
    in_specs = [
        pl.BlockSpec(memory_space=pltpu.SMEM),
        pl.BlockSpec((1, l, dk), lambda bi, h: (bi, 0, h)),
        pl.BlockSpec((1, l, dk), lambda bi, h: (bi, 0, hh + h)),
        pl.BlockSpec((1, l, dv), lambda bi, h: (bi, 0, hh + h)),
        pl.BlockSpec((1, l, dv), lambda bi, h: (bi, 0, 2 * hh + h)),
        pl.BlockSpec((1, dv), lambda bi, h: (0, h)),
    ]
    args = [dec, proj, proj, proj, proj, gn.reshape(1, hh * dv)]
    if has_state:
        st_spec = pl.BlockSpec((1, 1, 1, dk, dv), lambda bi, h: (bi, 0, h, 0, 0))
        in_specs += [st_spec, st_spec]
        args += [s0f, s0b]
    out_specs = [pl.BlockSpec((1, l, dv), lambda bi, h: (bi, 0, h))]
    out_shape = [jax.ShapeDtypeStruct((b, l, hh * dv), BF16)]
    if emit_state:
        so_spec = pl.BlockSpec((1, 1, dk, dv), lambda bi, h: (bi, h, 0, 0))
        out_specs += [so_spec, so_spec]
        out_shape += [jax.ShapeDtypeStruct((b, hh, dk, dv), F32)] * 2
    return pl.pallas_call(
        functools.partial(_ret_kernel, n_chunks=n_chunks, has_state=has_state, emit_state=emit_state),
        grid=(b, hh),
        in_specs=in_specs,
        out_specs=out_specs,
        out_shape=out_shape,
        scratch_shapes=[pltpu.VMEM((dk, dv), F32), pltpu.VMEM((dk, dv), F32),
                        pltpu.VMEM((l, dv), F32), pltpu.VMEM((l, dv), F32)],
        compiler_params=_params("parallel", "parallel"),
        name="retention",
    )(*args)


def _rope_rot(x, cos_t, sin_t):
    lane = lax.broadcasted_iota(jnp.int32, x.shape, x.ndim - 1)
    nxt = pltpu.roll(x, LANES - 1, axis=x.ndim - 1)
    prv = pltpu.roll(x, 1, axis=x.ndim - 1)
    swapped = jnp.where(jnp.bitwise_and(lane, 1) == 0, nxt, prv)
    return x * cos_t + swapped * sin_t


def _att_kernel(q_ref, k_ref, v_ref, qg_ref, kg_ref, *rest, rope, has_cache, emit_kv, groups):
    rest = list(rest)
    cosq_ref = sinq_ref = cosk_ref = sink_ref = ck_ref = cv_ref = kn_ref = vo_ref = None
    if rope:
        cosq_ref, sinq_ref, cosk_ref, sink_ref = rest[:4]
        rest = rest[4:]
    if has_cache:
        ck_ref, cv_ref = rest[:2]
        rest = rest[2:]
    o_ref = rest.pop(0)
    if emit_kv:
        kn_ref, vo_ref = rest[:2]
        rest = rest[2:]
    k_scr, v_scr = rest
    hd = ATT_HD
    scale = hd ** -0.5

    @pl.when(pl.program_id(2) == 0)
    def _():
        kn = _rms(k_ref[0], kg_ref[...])
        if emit_kv:
            kn_ref[0] = kn
            vo_ref[0] = v_ref[0]
        if rope:
            kn = _rope_rot(kn, cosk_ref[...], sink_ref[...])
        k_scr[...] = kn.astype(BF16)
        v_scr[...] = v_ref[0].astype(BF16)

    for g in range(groups):
        cols = slice(g * hd, (g + 1) * hd)
        qn = _rms(q_ref[0, :, cols], qg_ref[...])
        if rope:
            qn = _rope_rot(qn, cosq_ref[...], sinq_ref[...])
        qb = qn.astype(BF16)
        s1 = _dot_nt(qb, k_scr[...]) * scale
        m = jnp.max(s1, axis=-1, keepdims=True)
        if has_cache:
            s2 = _dot_nt(qb, ck_ref[0].astype(BF16)) * scale
            m = jnp.maximum(m, jnp.max(s2, axis=-1, keepdims=True))
        p1 = jnp.exp(s1 - m)
        den = jnp.sum(p1, axis=-1, keepdims=True)
        num = _dot(p1.astype(BF16), v_scr[...])
        if has_cache:
            p2 = jnp.exp(s2 - m)
            den = den + jnp.sum(p2, axis=-1, keepdims=True)
            num = num + _dot(p2.astype(BF16), cv_ref[0].astype(BF16))
        o_ref[0, :, cols] = (num / den).astype(o_ref.dtype)


def attention_core(proj, q_gain, k_gain, rope_tabs, cache_k, cache_v, b, l, tq, emit_kv):
    hd, kv, heads = ATT_HD, ATT_KV, ATT_HEADS
    groups = heads // kv
    rope = rope_tabs is not None
    has_cache = cache_k is not None
    in_specs = [
        pl.BlockSpec((1, tq, groups * hd), lambda bi, kh, qi: (bi, qi, kh)),
        pl.BlockSpec((1, l, hd), lambda bi, kh, qi: (bi, 0, heads + kh)),
        pl.BlockSpec((1, l, hd), lambda bi, kh, qi: (bi, 0, heads + kv + kh)),
        pl.BlockSpec((1, hd), lambda bi, kh, qi: (0, 0)),
        pl.BlockSpec((1, hd), lambda bi, kh, qi: (0, 0)),
    ]
    args = [proj, proj, proj, q_gain.reshape(1, hd), k_gain.reshape(1, hd)]
    if rope:
        cos_t, sin_t = rope_tabs
        in_specs += [
            pl.BlockSpec((tq, hd), lambda bi, kh, qi: (qi, 0)),
            pl.BlockSpec((tq, hd), lambda bi, kh, qi: (qi, 0)),
            pl.BlockSpec((l, hd), lambda bi, kh, qi: (0, 0)),
            pl.BlockSpec((l, hd), lambda bi, kh, qi: (0, 0)),
        ]
        args += [cos_t, sin_t, cos_t, sin_t]
    if has_cache:
        past = cache_k.shape[1]
        c_spec = pl.BlockSpec((1, past, hd), lambda bi, kh, qi: (bi, 0, kh))
        in_specs += [c_spec, c_spec]
        args += [cache_k, cache_v]
    out_specs = [pl.BlockSpec((1, tq, groups * hd), lambda bi, kh, qi: (bi, qi, kh))]
    out_shape = [jax.ShapeDtypeStruct((b, l, heads * hd), BF16)]
    if emit_kv:
        kv_spec = pl.BlockSpec((1, l, hd), lambda bi, kh, qi: (bi, 0, kh))
        out_specs += [kv_spec, kv_spec]
        out_shape += [jax.ShapeDtypeStruct((b, l, kv * hd), F32)] * 2
    return pl.pallas_call(
        functools.partial(_att_kernel, rope=rope, has_cache=has_cache, emit_kv=emit_kv, groups=groups),
        grid=(b, kv, l // tq),
        in_specs=in_specs,
        out_specs=out_specs,
        out_shape=out_shape,
        scratch_shapes=[pltpu.VMEM((l, hd), BF16), pltpu.VMEM((l, hd), BF16)],
        compiler_params=_params("parallel", "parallel", "arbitrary"),
        name="attention",
    )(*args)


def _rope_tables(l):
    rows = l // GRID_W
    row = jnp.repeat(jnp.arange(rows, dtype=F32), GRID_W)
    col = jnp.tile(jnp.arange(GRID_W, dtype=F32), rows)
    half = ATT_HD // 2
    inv = ROPE_THETA ** (-jnp.arange(0, half, 2, dtype=F32) / half)
    ang = jnp.concatenate([row[:, None] * inv, col[:, None] * inv], axis=-1)
    cos_t = jnp.repeat(jnp.cos(ang), 2, axis=-1)
    sin_h = jnp.sin(ang)
    sin_t = jnp.stack([-sin_h, sin_h], axis=-1).reshape(l, ATT_HD)
    return cos_t, sin_t


def _ml_kernel(q_ref, k_ref, v_ref, og_ref, gt_ref, gn_ref, *rest, n_chunks, has_state, emit_state):
    rest = list(rest)
    c0f_ref = n0f_ref = m0f_ref = c0b_ref = n0b_ref = m0b_ref = None
    if has_state:
        c0f_ref, n0f_ref, m0f_ref, c0b_ref, n0b_ref, m0b_ref = rest[:6]
        rest = rest[6:]
    o_ref = rest.pop(0)
    outs = None
    if emit_state:
        outs = rest[:6]
        rest = rest[6:]
    cm_f, cm_b, nv_f, nv_b, mm_f, mm_b, hf_scr, hb_scr = rest

    c = CHUNK
    dqk = q_ref.shape[-1]
    k_scale = dqk ** -0.5
    ri = lax.broadcasted_iota(jnp.int32, (c, c), 0)
    ci = lax.broadcasted_iota(jnp.int32, (c, c), 1)
    mask_f = ci <= ri
    mask_b = ci >= ri
    assert c == LANES and dqk == LANES
    dv_tiles = v_ref.shape[-1] // LANES

    def bf3(m):
        m = jnp.where(m, 1.0, 0.0).astype(BF16)
        return jnp.concatenate([m, m, m], axis=1), jnp.concatenate([m, m, m], axis=0)

    mf3, mf3_t = bf3(mask_f)
    mb3, mb3_t = bf3(mask_b)
    eye3, _ = bf3(ci == ri)

    def split3(x):
        hi = x.astype(BF16)
        r1 = x - hi.astype(F32)
        mid = r1.astype(BF16)
        lo = (r1 - mid.astype(F32)).astype(BF16)
        return jnp.concatenate([hi, mid, lo], axis=1)

    def wide(x):
        return jnp.concatenate([x] * dv_tiles, axis=1)

    if has_state:
        cm_f[...] = c0f_ref[0, 0, 0]
        cm_b[...] = c0b_ref[0, 0, 0]
        nv_f[...] = n0f_ref[0, 0]
        nv_b[...] = n0b_ref[0, 0]
        mm_f[...] = jnp.broadcast_to(m0f_ref[0, 0], mm_f.shape)
        mm_b[...] = jnp.broadcast_to(m0b_ref[0, 0], mm_b.shape)
    else:
        for r in (cm_f, cm_b, nv_f, nv_b, mm_f, mm_b):
            r[...] = jnp.zeros_like(r)

    def step(ch, cm, nv, mm, out_scr, mask, m3, m3_t, row_i, row_f):
        off = pl.multiple_of(ch * c, c)
        rows = pl.ds(off, c)
        qc = q_ref[0, rows, :]
        kc = k_ref[0, rows, :]
        vc = v_ref[0, rows, :]
        g_rows = gt_ref[0, 0, :, pl.ds(off, c)]
        i_row = g_rows[row_i:row_i + 1, :]
        f_row = _log_sigmoid(g_rows[row_f:row_f + 1, :])
        f3 = split3(f_row)
        i3 = split3(i_row)
        b_row = _dot(jnp.broadcast_to(f3, (16, 3 * c)), m3_t)[0:1, :]
        b_q = _dot_nt(m3, jnp.broadcast_to(f3, (LANES, 3 * c)))
        i_q = _dot_nt(eye3, jnp.broadcast_to(i3, (LANES, 3 * c)))
        m_old = mm[...]
        dlog = jnp.where(mask, b_q - b_row + i_row, -jnp.inf)
        inter = b_q + m_old
        m_q = jnp.maximum(inter, jnp.max(dlog, axis=-1, keepdims=True))
        w_intra = jnp.exp(dlog - m_q)
        w_x = jnp.exp(inter - m_q)
        s = _dot_nt(qc, kc) * (w_intra * k_scale)
        c_old = cm[...]
        n_old = nv[...]
        num = _dot(s.astype(BF16), vc) + _dot(qc, c_old.astype(BF16)) * wide(w_x)
        qn = _dot_nt(qc, jnp.broadcast_to(n_old, (LANES, dqk)).astype(BF16))
        den = jnp.sum(s, axis=-1, keepdims=True) + qn * w_x
        den = jnp.maximum(jnp.abs(den), jnp.exp(-m_q))
        out_scr[rows, :] = num * wide(1.0 / den)
        b_end = jnp.sum(f_row, axis=-1, keepdims=True)
        wlog = b_end - b_q + i_q
        m_new = jnp.maximum(b_end + m_old, jnp.max(wlog, axis=0, keepdims=True))
        carry_dec = jnp.exp(b_end + m_old - m_new)
        kw = kc.astype(F32) * (jnp.exp(wlog - m_new) * k_scale)
        cm[...] = c_old * wide(carry_dec) + _dot_tn(kw.astype(BF16), vc)
        nv[...] = n_old * carry_dec + jnp.sum(kw, axis=0, keepdims=True)
        mm[...] = m_new

    def fin(ch):
        rows = pl.ds(pl.multiple_of(ch * c, c), c)
        hn = _rms(hf_scr[rows, :] + hb_scr[rows, :], gn_ref[...])
        o_ref[0, rows, :] = (jax.nn.sigmoid(og_ref[0, rows, :].astype(F32)) * hn).astype(o_ref.dtype)

    def body(i, carry):
        step(i, cm_f, nv_f, mm_f, hf_scr, mask_f, mf3, mb3_t, 0, 1)
        step(n_chunks - 1 - i, cm_b, nv_b, mm_b, hb_scr, mask_b, mb3, mf3_t, 2, 3)
        return carry

    def body_fin(i, carry):
        body(i, carry)
        fin(i)
        fin(n_chunks - 1 - i)
        return carry

    _two_phase_loop(n_chunks, body, body_fin)

    if emit_state:
        for dst, src in zip(outs[0::3] + outs[1::3], (cm_f, cm_b, nv_f, nv_b)):
            dst[0, 0] = src[...]
        for dst, src in zip(outs[2::3], (mm_f, mm_b)):
            dst[0, 0] = src[:, 0:1]


def mlstm_core(proj, gates_t, gn, state, b, l, emit_state):
    hh = ML_HEADS
    width = proj.shape[-1]
    dqk = width // (6 * hh)
    dv = 2 * dqk
    has_state = state is not None
    n_chunks = l // CHUNK
    in_specs = [
        pl.BlockSpec((1, l, dqk), lambda bi, h: (bi, 0, h)),
        pl.BlockSpec((1, l, dqk), lambda bi, h: (bi, 0, hh + h)),
        pl.BlockSpec((1, l, dv), lambda bi, h: (bi, 0, hh + h)),
        pl.BlockSpec((1, l, dv), lambda bi, h: (bi, 0, 2 * hh + h)),
        pl.BlockSpec((1, 1, 8, l), lambda bi, h: (bi, h, 0, 0)),
        pl.BlockSpec((1, dv), lambda bi, h: (0, h)),
    ]
    args = [proj, proj, proj, proj, gates_t, gn.reshape(1, hh * dv)]
    c_spec_in = pl.BlockSpec((1, 1, 1, dqk, dv), lambda bi, h: (bi, 0, h, 0, 0))
    n_spec = pl.BlockSpec((1, 1, 1, dqk), lambda bi, h: (bi, h, 0, 0))
    m_spec = pl.BlockSpec((1, 1, 1, 1), lambda bi, h: (bi, h, 0, 0))
    if has_state:
        cf, nf, mf, cb, nb, mb = state
        in_specs += [c_spec_in, n_spec, m_spec, c_spec_in, n_spec, m_spec]
        args += [cf, nf.reshape(b, hh, 1, dqk), mf.reshape(b, hh, 1, 1),
                 cb, nb.reshape(b, hh, 1, dqk), mb.reshape(b, hh, 1, 1)]
    out_specs = [pl.BlockSpec((1, l, dv), lambda bi, h: (bi, 0, h))]
    out_shape = [jax.ShapeDtypeStruct((b, l, hh * dv), BF16)]
    if emit_state:
        c_spec_out = pl.BlockSpec((1, 1, dqk, dv), lambda bi, h: (bi, h, 0, 0))
        out_specs += [c_spec_out, n_spec, m_spec] * 2
        out_shape += [jax.ShapeDtypeStruct((b, hh, dqk, dv), F32),
                      jax.ShapeDtypeStruct((b, hh, 1, dqk), F32),
                      jax.ShapeDtypeStruct((b, hh, 1, 1), F32)] * 2
    return pl.pallas_call(
        functools.partial(_ml_kernel, n_chunks=n_chunks, has_state=has_state, emit_state=emit_state),
        grid=(b, hh),
        in_specs=in_specs,
        out_specs=out_specs,
        out_shape=out_shape,
        scratch_shapes=[pltpu.VMEM((dqk, dv), F32), pltpu.VMEM((dqk, dv), F32),
                        pltpu.VMEM((1, dqk), F32), pltpu.VMEM((1, dqk), F32),
                        pltpu.VMEM((1, LANES), F32), pltpu.VMEM((1, LANES), F32),
                        pltpu.VMEM((l, dv), F32), pltpu.VMEM((l, dv), F32)],
        compiler_params=_params("parallel", "parallel"),
        name="mlstm",
    )(*args)


def _hy_filter_kernel(feat_ref, w1_ref, b1_ref, w2_ref, b2_ref, w3_ref, fr_ref, win_ref, sum_ref, dif_ref):
    d = win_ref.shape[-1]
    z = jnp.sin(fr_ref[0:1, :] * (_dot(feat_ref[...].astype(BF16), w1_ref[...].astype(BF16)) + b1_ref[...]))
    z = jnp.sin(fr_ref[1:2, :] * (_dot(z.astype(BF16), w2_ref[...].astype(BF16)) + b2_ref[...]))
    filt = _dot(z.astype(BF16), w3_ref[...].astype(BF16))
    win = win_ref[...]
    ff = filt[:, :d] * win
    fb = filt[:, d:] * win
    sum_ref[...] = (ff + fb).astype(sum_ref.dtype)
    dif_ref[...] = (ff - fb).astype(dif_ref.dtype)


def hyena_filters(l, d, w1, b1, w2, b2, w3, freq):
    t = jnp.linspace(0.0, 1.0, l, dtype=F32)[:, None]
    pos = jnp.arange(l, dtype=F32)[:, None]
    bands = jnp.linspace(1e-4, HY_BANDS - 1, HY_BANDS, dtype=F32)
    ang = 2.0 * math.pi * pos * bands / l
    feats = jnp.concatenate([t, jnp.cos(ang), -jnp.sin(ang)], axis=-1)
    emb = feats.shape[1]
    feats = jnp.pad(feats, ((0, 0), (0, LANES - emb)))
    w1p = jnp.pad(w1, ((0, LANES - emb), (0, 0)))
    deltas = jnp.abs(jnp.linspace(math.log(HY_TARGET) / HY_SLOW_DECAY,
                                  math.log(HY_TARGET) / HY_FAST_DECAY, d, dtype=F32))
    window = jnp.exp(-t * deltas)
    hid = w2.shape[0]
    tl = min(l, 512)
    full = lambda shp: pl.BlockSpec(shp, lambda i: (0,) * len(shp))
    return pl.pallas_call(
        _hy_filter_kernel,
        grid=(l // tl,),
        in_specs=[
            pl.BlockSpec((tl, LANES), lambda i: (i, 0)),
            full((LANES, hid)), full((1, hid)), full((hid, hid)), full((1, hid)), full((hid, 2 * d)),
            full((2, hid)),
            pl.BlockSpec((tl, d), lambda i: (i, 0)),
        ],
        out_specs=[pl.BlockSpec((tl, d), lambda i: (i, 0))] * 2,
        out_shape=[jax.ShapeDtypeStruct((l, d), BF16)] * 2,
        compiler_params=_params("parallel"),
        name="hyena_filter",
    )(feats, w1p, b1.reshape(1, hid), w2, b2.reshape(1, hid), w3, freq, window)


def _dft_tables(l):
    n = 2 * l
    blk = 64
    assert l % blk == 0

    def trig(step, count):
        k = lax.broadcasted_iota(jnp.int32, (l, count), 0)
        t = lax.broadcasted_iota(jnp.int32, (l, count), 1)
        ang = (((2 * k + 1) * step * t) % (2 * n)).astype(F32) * (math.pi / n)
        return jnp.cos(ang), jnp.sin(ang)

    c1, s1 = trig(blk, l // blk)
    c0, s0 = trig(1, blk)
    cos_kt = (c1[:, :, None] * c0[:, None, :] - s1[:, :, None] * s0[:, None, :]).reshape(l, l).astype(BF16)
    sin_kt = (s1[:, :, None] * c0[:, None, :] + c1[:, :, None] * s0[:, None, :]).reshape(l, l).astype(BF16)
    c1t, s1t, c0t, s0t = c1.T, s1.T, c0.T, s0.T
    cos_tk = (c1t[:, None, :] * c0t[None, :, :] - s1t[:, None, :] * s0t[None, :, :]).reshape(l, l).astype(BF16)
    sin_tk = (s1t[:, None, :] * c0t[None, :, :] + c1t[:, None, :] * s0t[None, :, :]).reshape(l, l).astype(BF16)
    return cos_kt, sin_kt, cos_tk, sin_tk


def _hy_spec_kernel(c_ref, s_ref, fs_ref, fd_ref, bias_ref, gr_ref, gs_ref):
    gr_ref[...] = _dot(c_ref[...], fs_ref[...]) + bias_ref[...]
    gs_ref[...] = _dot(s_ref[...], fd_ref[...])


def hyena_filter_spectrum(cos_kt, sin_kt, f_sum, f_dif, f_bias):
    l, d = f_sum.shape
    tm = min(l, 512)
    tn = min(d, 512)
    return pl.pallas_call(
        _hy_spec_kernel,
        grid=(l // tm, d // tn),
        in_specs=[
            pl.BlockSpec((tm, l), lambda i, j: (i, 0)),
            pl.BlockSpec((tm, l), lambda i, j: (i, 0)),
            pl.BlockSpec((l, tn), lambda i, j: (0, j)),
            pl.BlockSpec((l, tn), lambda i, j: (0, j)),
            pl.BlockSpec((1, tn), lambda i, j: (0, j)),
        ],
        out_specs=[pl.BlockSpec((tm, tn), lambda i, j: (i, j))] * 2,
        out_shape=[jax.ShapeDtypeStruct((l, d), F32)] * 2,
        compiler_params=_params("parallel", "parallel"),
        name="hyena_filter_spectrum",
    )(cos_kt, sin_kt, f_sum, f_dif, f_bias.reshape(1, d))


def _hy_conv_kernel(p0_ref, p1_ref, pv_ref, w0_ref, w1_ref, wv_ref, b0_ref, b1_ref, bv_ref, z_ref, x0_ref):
    l = p0_ref.shape[1]
    t = lax.broadcasted_iota(jnp.int32, (l, 1), 0)

    def conv(p_ref, w_ref, b_ref):
        p = p_ref[0].astype(F32)
        prev = jnp.where(t == 0, 0.0, pltpu.roll(p, 1, axis=0))
        nxt = jnp.where(t == l - 1, 0.0, pltpu.roll(p, l - 1, axis=0))
        return b_ref[...] + prev * w_ref[0:1, :] + p * w_ref[1:2, :] + nxt * w_ref[2:3, :]

    x0_ref[0] = conv(p0_ref, w0_ref, b0_ref)
    z_ref[0] = (conv(pv_ref, wv_ref, bv_ref) * conv(p1_ref, w1_ref, b1_ref)).astype(z_ref.dtype)


def hyena_short_conv(proj, conv_w, conv_b, b, l, d):
    tc = min(d, 512)
    nc = d // tc
    p_spec = lambda off: pl.BlockSpec((1, l, tc), lambda bi, j: (bi, 0, off * nc + j))
    w_spec = lambda off: pl.BlockSpec((HY_SHORT, tc), lambda bi, j: (0, off * nc + j))
    b_spec = lambda off: pl.BlockSpec((1, tc), lambda bi, j: (0, off * nc + j))
    cb = conv_b.reshape(1, 3 * d)
    return pl.pallas_call(
        _hy_conv_kernel,
        grid=(b, nc),
        in_specs=[p_spec(0), p_spec(1), p_spec(2), w_spec(0), w_spec(1), w_spec(2),
                  b_spec(0), b_spec(1), b_spec(2)],
        out_specs=[pl.BlockSpec((1, l, tc), lambda bi, j: (bi, 0, j))] * 2,
        out_shape=[jax.ShapeDtypeStruct((b, l, d), BF16), jax.ShapeDtypeStruct((b, l, d), F32)],
        compiler_params=_params("parallel", "parallel"),
        name="hyena_short_conv",
    )(proj, proj, proj, conv_w, conv_w, conv_w, cb, cb, cb)


def _hy_fwd_kernel(c_ref, s_ref, z_ref, gr_ref, gs_ref, yr_ref, ys_ref):
    z = z_ref[0]
    zr = _dot(c_ref[...], z)
    zs = _dot(s_ref[...], z)
    gr = gr_ref[...]
    gs = gs_ref[...]
    yr_ref[0] = (zr * gr - zs * gs).astype(yr_ref.dtype)
    ys_ref[0] = (zr * gs + zs * gr).astype(ys_ref.dtype)


def hyena_forward_dft(cos_kt, sin_kt, z, g_r, g_s):
    b, l, d = z.shape
    tm = min(l, 512)
    tn = min(d, 512)
    return pl.pallas_call(
        _hy_fwd_kernel,
        grid=(l // tm, b, d // tn),
        in_specs=[
            pl.BlockSpec((tm, l), lambda i, bi, j: (i, 0)),
            pl.BlockSpec((tm, l), lambda i, bi, j: (i, 0)),
            pl.BlockSpec((1, l, tn), lambda i, bi, j: (bi, 0, j)),
            pl.BlockSpec((tm, tn), lambda i, bi, j: (i, j)),
            pl.BlockSpec((tm, tn), lambda i, bi, j: (i, j)),
        ],
        out_specs=[pl.BlockSpec((1, tm, tn), lambda i, bi, j: (bi, i, j))] * 2,
        out_shape=[jax.ShapeDtypeStruct((b, l, d), BF16)] * 2,
        compiler_params=_params("parallel", "parallel", "parallel"),
        name="hyena_forward_dft",
    )(cos_kt, sin_kt, z, g_r, g_s)


def _hy_inv_kernel(ct_ref, st_ref, yr_ref, ys_ref, x0_ref, o_ref, *, inv_scale):
    y = _dot(ct_ref[...], yr_ref[0]) + _dot(st_ref[...], ys_ref[0])
    o_ref[0] = (y * inv_scale * x0_ref[0]).astype(o_ref.dtype)


def hyena_inverse_dft(cos_tk, sin_tk, y_r, y_s, x0):
    b, l, d = y_r.shape
    tm = min(l, 512)
    tn = min(d, 512)
    return pl.pallas_call(
        functools.partial(_hy_inv_kernel, inv_scale=1.0 / l),
        grid=(l // tm, b, d // tn),
        in_specs=[
            pl.BlockSpec((tm, l), lambda i, bi, j: (i, 0)),
            pl.BlockSpec((tm, l), lambda i, bi, j: (i, 0)),
            pl.BlockSpec((1, l, tn), lambda i, bi, j: (bi, 0, j)),
            pl.BlockSpec((1, l, tn), lambda i, bi, j: (bi, 0, j)),
            pl.BlockSpec((1, tm, tn), lambda i, bi, j: (bi, i, j)),
        ],
        out_specs=pl.BlockSpec((1, tm, tn), lambda i, bi, j: (bi, i, j)),
        out_shape=jax.ShapeDtypeStruct((b, l, d), BF16),
        compiler_params=_params("parallel", "parallel", "parallel"),
        name="hyena_inverse_dft",
    )(cos_tk, sin_tk, y_r, y_s, x0)


def _tile_rows(grp):
    span = grp.l if grp.per_batch else grp.t
    return next(tm for tm in (1024, 512, 256, 128) if span % tm == 0)


def kernel(x_prompt, x_sample, cache_k, cache_v, state_ret_fwd, state_ret_bwd, state_ml_C_fwd, state_ml_n_fwd, state_ml_m_fwd, state_ml_C_bwd, state_ml_n_bwd, state_ml_m_bwd, c, c_ctx, mod_w, mod_b, norm_mix_pre, norm_mix_post, norm_ffn_pre, norm_ffn_post, mlp_w1, mlp_w2, ret_w_in, ret_decay_fwd, ret_decay_bwd, ret_gn, ret_w_out, att_w_in, att_q_gain, att_k_gain, att_w_out, ml_w_in, ml_gate_b, ml_gn, ml_w_out, hy_w_in, hy_b_in, hy_conv_w, hy_conv_b, hy_f_w1, hy_f_b1, hy_f_w2, hy_f_b2, hy_f_w3, hy_sin_freq, hy_f_bias, hy_w_out):
    bp, lp, d = x_prompt.shape
    bs, ls, _ = x_sample.shape
    depth = mod_w.shape[0]
    n_mixers = 4
    mod_rows = 16
    assert 1 + bs <= mod_rows

    grp_p = Group(bp, lp, 0, False)
    grp_s = Group(bs, ls, 1, True)
    groups = (grp_p, grp_s)

    cond = jnp.concatenate([c_ctx[None, :], c, jnp.zeros((mod_rows - 1 - bs, d), F32)], axis=0)
    mod_all = adaln_all(cond, mod_w, mod_b)

    xs = [x_prompt.reshape(grp_p.t, d), x_sample.reshape(grp_s.t, d)]
    new_k = new_v = new_rf = new_rb = None
    new_ml = None

    for i in range(depth):
        mixer = i % n_mixers
        j = i // n_mixers
        mod3 = mod_all[i].reshape(mod_rows, 1, 6 * d)
        ys = []
        for gi, grp in enumerate(groups):
            x = xs[gi]
            tm = _tile_rows(grp)
            is_prompt = gi == 0
            if mixer == 0:
                w_in = ret_w_in[j].astype(BF16)
                proj = norm_matmul(x, norm_mix_pre[i], mod3, 0, 1, w_in, None, BF16, grp, tm, PROJ_TILE_N)
                dec = jnp.stack([ret_decay_fwd[j], ret_decay_bwd[j]]).astype(F32)
                s0f = None if is_prompt else state_ret_fwd
                s0b = None if is_prompt else state_ret_bwd
                assert is_prompt or state_ret_fwd.shape[1] == 1
                res = retention_core(proj.reshape(grp.b, grp.l, -1), dec, ret_gn[j], s0f, s0b,
                                     grp.b, grp.l, emit_state=is_prompt)
                if is_prompt:
                    new_rf, new_rb = res[1][:, None], res[2][:, None]
                a = res[0].reshape(grp.t, -1)
                w_out = ret_w_out[j].astype(BF16)
            elif mixer == 1:
                w_in = att_w_in[j].astype(BF16)
                proj = norm_matmul(x, norm_mix_pre[i], mod3, 0, 1, w_in, None, F32, grp, tm, PROJ_TILE_N)
                proj = proj.reshape(grp.b, grp.l, -1)
                if is_prompt:
                    res = attention_core(proj, att_q_gain[j], att_k_gain[j], None, None, None,
                                         grp.b, grp.l, min(grp.l, 256), emit_kv=True)
                    new_k = res[1].reshape(grp.b, 1, grp.l, ATT_KV, ATT_HD)
                    new_v = res[2].reshape(grp.b, 1, grp.l, ATT_KV, ATT_HD)
                else:
                    assert cache_k.shape[1] == 1
                    ck = cache_k.reshape(grp.b, cache_k.shape[2], ATT_KV * ATT_HD)
                    cv = cache_v.reshape(grp.b, cache_v.shape[2], ATT_KV * ATT_HD)
                    res = attention_core(proj, att_q_gain[j], att_k_gain[j], _rope_tables(grp.l), ck, cv,
                                         grp.b, grp.l, min(grp.l, 256), emit_kv=False)
                a = res[0].reshape(grp.t, -1)
                w_out = att_w_out[j].astype(BF16)
            elif mixer == 2:
                hh = ML_HEADS
                n_main = ml_w_in.shape[2] - 4 * hh
                w_main = ml_w_in[j][:, :n_main].astype(BF16)
                w_gate = jnp.pad(ml_w_in[j][:, n_main:], ((0, 0), (0, LANES - 4 * hh))).astype(BF16)
                b_gate = jnp.pad(ml_gate_b[j], (0, LANES - 4 * hh))
                proj = norm_matmul(x, norm_mix_pre[i], mod3, 0, 1, w_main, None, BF16, grp, tm, PROJ_TILE_N)
                gates = norm_matmul(x, norm_mix_pre[i], mod3, 0, 1, w_gate, b_gate, F32, grp, tm, LANES)
                gates = gates[:, :4 * hh].reshape(grp.b, grp.l, 4, hh)
                gates_t = jnp.pad(gates.transpose(0, 3, 2, 1), ((0, 0), (0, 0), (0, 4), (0, 0)))
                state = None
                if not is_prompt:
                    assert state_ml_C_fwd.shape[1] == 1
                    state = (state_ml_C_fwd, state_ml_n_fwd, state_ml_m_fwd,
                             state_ml_C_bwd, state_ml_n_bwd, state_ml_m_bwd)
                res = mlstm_core(proj.reshape(grp.b, grp.l, -1), gates_t, ml_gn[j], state,
                                 grp.b, grp.l, emit_state=is_prompt)
                if is_prompt:
                    dqk = res[2].shape[-1]
                    new_ml = (res[1][:, None], res[2].reshape(grp.b, 1, hh, dqk), res[3].reshape(grp.b, 1, hh),
                              res[4][:, None], res[5].reshape(grp.b, 1, hh, dqk), res[6].reshape(grp.b, 1, hh))
                a = res[0].reshape(grp.t, -1)
                w_out = ml_w_out[j].astype(BF16)
            else:
                w_in = hy_w_in[j].astype(BF16)
                proj = norm_matmul(x, norm_mix_pre[i], mod3, 0, 1, w_in, hy_b_in[j], F32, grp, tm, PROJ_TILE_N)
                z, x0 = hyena_short_conv(proj.reshape(grp.b, grp.l, 3 * d), hy_conv_w[j], hy_conv_b[j],
                                         grp.b, grp.l, d)
                f_sum, f_dif = hyena_filters(grp.l, d, hy_f_w1[j], hy_f_b1[j], hy_f_w2[j], hy_f_b2[j],
                                             hy_f_w3[j], hy_sin_freq[j])
                cos_kt, sin_kt, cos_tk, sin_tk = _dft_tables(grp.l)
                g_r, g_s = hyena_filter_spectrum(cos_kt, sin_kt, f_sum, f_dif, hy_f_bias[j])
                y_r, y_s = hyena_forward_dft(cos_kt, sin_kt, z, g_r, g_s)
                a = hyena_inverse_dft(cos_tk, sin_tk, y_r, y_s, x0).reshape(grp.t, d)
                w_out = hy_w_out[j].astype(BF16)
            x = matmul_resnorm(a, w_out, x, mod3, 2, norm_mix_post[i], grp, tm)
            x = mlp_block(x, norm_ffn_pre[i], norm_ffn_post[i], mod3,
                          mlp_w1[i].astype(BF16), mlp_w2[i].astype(BF16), grp, tm, MLP_TILE_F)
            ys.append(x)
        xs = ys

    y_prompt = xs[0].reshape(bp, lp, d)
    y_sample = xs[1].reshape(bs, ls, d)
    return (y_prompt, y_sample, new_k, new_v, new_rf, new_rb) + tuple(new_ml)
```

```python
import functools
import math

import jax
import jax.numpy as jnp
import numpy as np
from jax import lax
from jax.experimental import pallas as pl
from jax.experimental.pallas import tpu as pltpu

F32 = jnp.float32
BF16 = jnp.bfloat16

EPS = 1e-6
CHUNK = 128
RET_CHUNK = 256
GRID_W = 64
ROPE_THETA = 10000.0

RET_HEADS = 4
ATT_HEADS = 8
ATT_KV = 2
ATT_HD = 128
ML_HEADS = 4

HY_BANDS = 16
HY_SHORT = 3
HY_FAST_DECAY = 0.3
HY_SLOW_DECAY = 1.5
HY_TARGET = 1e-2

VMEM_LIMIT_BYTES = 56 * 1024 * 1024
LANES = 128
PROJ_TILE_N = 1536
MLP_TILE_F = 1024


def _params(*sem):
    return pltpu.CompilerParams(dimension_semantics=sem, vmem_limit_bytes=VMEM_LIMIT_BYTES)


def _dot(a, b):
    return jnp.dot(a, b, preferred_element_type=F32)


def _dot_nt(a, b):
    return lax.dot_general(a, b, (((1,), (1,)), ((), ())), preferred_element_type=F32)


def _dot_tn(a, b):
    return lax.dot_general(a, b, (((0,), (0,)), ((), ())), preferred_element_type=F32)


def _rms(x, g):
    return x * lax.rsqrt(jnp.mean(x * x, axis=-1, keepdims=True) + EPS) * g


def _log_sigmoid(x):
    return jnp.minimum(x, 0.0) - jnp.log1p(jnp.exp(-jnp.abs(x)))


def _chunk_offset(ch, c):
    return ch * c if isinstance(ch, int) else pl.multiple_of(ch * c, c)


def _two_phase_loop(n_chunks, body, fin):
    if n_chunks == 1:
        body(0, 0)
        fin(0)
        return
    assert n_chunks % 2 == 0
    half = n_chunks // 2
    unroll = 2 if half % 2 == 0 else 1

    def body_fin(i, carry):
        body(i, carry)
        fin(i)
        fin(n_chunks - 1 - i)
        return carry

    lax.fori_loop(0, half, body, 0, unroll=unroll)
    lax.fori_loop(half, n_chunks, body_fin, 0, unroll=unroll)


class Group:
    def __init__(self, b, l, row0, per_batch):
        self.b, self.l, self.row0, self.per_batch = b, l, row0, per_batch
        self.t = b * l

    def mod_spec(self, chunk, tm, d):
        row0, per_batch, l = self.row0, self.per_batch, self.l
        if per_batch:
            return pl.BlockSpec((1, 1, d), lambda i, *_: (row0 + (i * tm) // l, 0, chunk))
        return pl.BlockSpec((1, 1, d), lambda i, *_: (row0, 0, chunk))


def _adaln_kernel(c_ref, w_ref, b_ref, o_ref):
    s = jax.nn.silu(c_ref[...])
    o_ref[0] = _dot(s.astype(BF16), w_ref[0].astype(BF16)) + b_ref[0]


def adaln_all(cond, mod_w, mod_b):
    depth, d, n = mod_w.shape
    rows = cond.shape[0]
    tn = 768
    return pl.pallas_call(
        _adaln_kernel,
        grid=(depth, n // tn),
        in_specs=[
            pl.BlockSpec((rows, d), lambda l, j: (0, 0)),
            pl.BlockSpec((1, d, tn), lambda l, j: (l, 0, j)),
            pl.BlockSpec((1, 1, tn), lambda l, j: (l, 0, j)),
        ],
        out_specs=pl.BlockSpec((1, rows, tn), lambda l, j: (l, 0, j)),
        out_shape=jax.ShapeDtypeStruct((depth, rows, n), F32),
        compiler_params=_params("parallel", "parallel"),
        name="adaln",
    )(cond, mod_w, mod_b.reshape(depth, 1, n))


def _norm_mm_kernel(x_ref, g_ref, sh_ref, sc_ref, w_ref, *rest, has_bias):
    if has_bias:
        b_ref, o_ref, h_scr = rest
    else:
        o_ref, h_scr = rest

    @pl.when(pl.program_id(1) == 0)
    def _():
        y = _rms(x_ref[...], g_ref[...])
        h_scr[...] = (y * (1.0 + sc_ref[0]) + sh_ref[0]).astype(BF16)

    acc = _dot(h_scr[...], w_ref[...])
    if has_bias:
        acc = acc + b_ref[...]
    o_ref[...] = acc.astype(o_ref.dtype)


def norm_matmul(x, gain, mod3, sh_idx, sc_idx, w, bias, out_dtype, grp, tm, tn):
    t, d = x.shape
    n = w.shape[1]
    in_specs = [
        pl.BlockSpec((tm, d), lambda i, j: (i, 0)),
        pl.BlockSpec((1, d), lambda i, j: (0, 0)),
        grp.mod_spec(sh_idx, tm, d),
        grp.mod_spec(sc_idx, tm, d),
        pl.BlockSpec((d, tn), lambda i, j: (0, j)),
    ]
    args = [x, gain.reshape(1, d), mod3, mod3, w]
    if bias is not None:
        in_specs.append(pl.BlockSpec((1, tn), lambda i, j: (0, j)))
        args.append(bias.reshape(1, n))
    return pl.pallas_call(
        functools.partial(_norm_mm_kernel, has_bias=bias is not None),
        grid=(t // tm, n // tn),
        in_specs=in_specs,
        out_specs=pl.BlockSpec((tm, tn), lambda i, j: (i, j)),
        out_shape=jax.ShapeDtypeStruct((t, n), out_dtype),
        scratch_shapes=[pltpu.VMEM((tm, d), BF16)],
        compiler_params=_params("parallel", "arbitrary"),
        name="norm_matmul",
    )(*args)


def _mm_res_kernel(a_ref, w_ref, x_ref, gate_ref, pg_ref, o_ref):
    y = _dot(a_ref[...], w_ref[...])
    o_ref[...] = x_ref[...] + gate_ref[0] * _rms(y, pg_ref[...])


def matmul_resnorm(a, w, x, mod3, gate_idx, post_gain, grp, tm):
    t, k = a.shape
    d = w.shape[1]
    return pl.pallas_call(
        _mm_res_kernel,
        grid=(t // tm,),
        in_specs=[
            pl.BlockSpec((tm, k), lambda i: (i, 0)),
            pl.BlockSpec((k, d), lambda i: (0, 0)),
            pl.BlockSpec((tm, d), lambda i: (i, 0)),
            grp.mod_spec(gate_idx, tm, d),
            pl.BlockSpec((1, d), lambda i: (0, 0)),
        ],
        out_specs=pl.BlockSpec((tm, d), lambda i: (i, 0)),
        out_shape=jax.ShapeDtypeStruct((t, d), F32),
        compiler_params=_params("parallel"),
        name="out_proj",
    )(a, w, x, mod3, post_gain.reshape(1, d))


def _mlp_kernel(x_ref, g_ref, sh_ref, sc_ref, w1_ref, w2_ref, gate_ref, pg_ref, o_ref, h_scr, acc_scr):
    j = pl.program_id(1)

    @pl.when(j == 0)
    def _():
        y = _rms(x_ref[...], g_ref[...])
        h_scr[...] = (y * (1.0 + sc_ref[0]) + sh_ref[0]).astype(BF16)
        acc_scr[...] = jnp.zeros_like(acc_scr)

    u = _dot(h_scr[...], w1_ref[...])
    u = jnp.square(jnp.maximum(u, 0.0)).astype(BF16)
    acc_scr[...] += _dot(u, w2_ref[...])

    @pl.when(j == pl.num_programs(1) - 1)
    def _():
        o_ref[...] = x_ref[...] + gate_ref[0] * _rms(acc_scr[...], pg_ref[...])


def mlp_block(x, pre_gain, post_gain, mod3, w1, w2, grp, tm, tf):
    t, d = x.shape
    f = w1.shape[1]
    return pl.pallas_call(
        _mlp_kernel,
        grid=(t // tm, f // tf),
        in_specs=[
            pl.BlockSpec((tm, d), lambda i, j: (i, 0)),
            pl.BlockSpec((1, d), lambda i, j: (0, 0)),
            grp.mod_spec(3, tm, d),
            grp.mod_spec(4, tm, d),
            pl.BlockSpec((d, tf), lambda i, j: (0, j)),
            pl.BlockSpec((tf, d), lambda i, j: (j, 0)),
            grp.mod_spec(5, tm, d),
            pl.BlockSpec((1, d), lambda i, j: (0, 0)),
        ],
        out_specs=pl.BlockSpec((tm, d), lambda i, j: (i, 0)),
        out_shape=jax.ShapeDtypeStruct((t, d), F32),
        scratch_shapes=[pltpu.VMEM((tm, d), BF16), pltpu.VMEM((tm, d), F32)],
        compiler_params=_params("parallel", "arbitrary"),
        name="mlp",
    )(x, pre_gain.reshape(1, d), mod3, mod3, w1, w2, mod3, post_gain.reshape(1, d))


def _ret_kernel(dec_ref, q_ref, k_ref, v_ref, g_ref, gn_ref, *rest, n_chunks, has_state, emit_state):
    rest = list(rest)
    s0f_ref = s0b_ref = sf_ref = sb_ref = None
    if has_state:
        s0f_ref, s0b_ref = rest[:2]
        rest = rest[2:]
    o_ref = rest.pop(0)
    if emit_state:
        sf_ref, sb_ref = rest[:2]
        rest = rest[2:]
    st_f, st_b, of_scr, ob_scr = rest

    c = RET_CHUNK
    dk = q_ref.shape[-1]
    h = pl.program_id(1)
    lg_f = _log_sigmoid(jnp.full((1, 1), dec_ref[0, h], F32))
    lg_b = _log_sigmoid(jnp.full((1, 1), dec_ref[1, h], F32))
    ri = lax.broadcasted_iota(jnp.int32, (c, c), 0)
    ci = lax.broadcasted_iota(jnp.int32, (c, c), 1)
    rel = (ri - ci).astype(F32)
    intra_f = jnp.where(rel >= 0, jnp.exp(lg_f * jnp.maximum(rel, 0.0)), 0.0)
    intra_b = jnp.where(rel <= 0, jnp.exp(lg_b * jnp.maximum(-rel, 0.0)), 0.0)
    idx = lax.broadcasted_iota(jnp.int32, (c, 1), 0).astype(F32)
    qdec_f = jnp.exp(lg_f * (idx + 1.0))
    kdec_f = jnp.exp(lg_f * (c - 1.0 - idx))
    qdec_b = jnp.exp(lg_b * (c - idx))
    kdec_b = jnp.exp(lg_b * idx)
    cdec_f = jnp.exp(lg_f * c)
    cdec_b = jnp.exp(lg_b * c)
    q_scale = dk ** -0.5

    if has_state:
        st_f[...] = s0f_ref[0, 0, 0]
        st_b[...] = s0b_ref[0, 0, 0]
    else:
        st_f[...] = jnp.zeros_like(st_f)
        st_b[...] = jnp.zeros_like(st_b)

    def step(ch, st, out_scr, intra, qdec, kdec, cdec):
        rows = pl.ds(_chunk_offset(ch, c), c)
        qc = q_ref[0, rows, :]
        kc = k_ref[0, rows, :]
        vc = v_ref[0, rows, :]
        s_old = st[...]
        sc = _dot_nt(qc, kc) * (intra * q_scale)
        o = _dot(sc.astype(BF16), vc) + _dot(qc, s_old.astype(BF16)) * (qdec * q_scale)
        kw = (kc.astype(F32) * kdec).astype(BF16)
        st[...] = s_old * cdec + _dot_tn(kw, vc)
        out_scr[rows, :] = o

    def fin(ch):
        rows = pl.ds(_chunk_offset(ch, c), c)
        o = _rms(of_scr[rows, :] + ob_scr[rows, :], gn_ref[...])
        o_ref[0, rows, :] = (jax.nn.silu(g_ref[0, rows, :].astype(F32)) * o).astype(o_ref.dtype)

    def body(i, carry):
        step(i, st_f, of_scr, intra_f, qdec_f, kdec_f, cdec_f)
        step(n_chunks - 1 - i, st_b, ob_scr, intra_b, qdec_b, kdec_b, cdec_b)
        return carry

    _two_phase_loop(n_chunks, body, fin)

    if emit_state:
        sf_ref[0, 0] = st_f[...]
        sb_ref[0, 0] = st_b[...]


def retention_core(proj, dec, gn, s0f, s0b, b, l, emit_state):
    hh = RET_HEADS
    width = proj.shape[-1]
    dk = width // (6 * hh)
    dv = 2 * dk
    has_state = s0f is not None
    n_chunks = l // RET_CHUNK
    in_specs = [
        pl.BlockSpec(memory_space=pltpu.SMEM),
        pl.BlockSpec((1, l, dk), lambda bi, h: (bi, 0, h)),
        pl.BlockSpec((1, l, dk), lambda bi, h: (bi, 0, hh + h)),
        pl.BlockSpec((1, l, dv), lambda bi, h: (bi, 0, hh + h)),
        pl.BlockSpec((1, l, dv), lambda bi, h: (bi, 0, 2 * hh + h)),
        pl.BlockSpec((1, dv), lambda bi, h: (0, h)),
    ]
    args = [dec, proj, proj, proj, proj, gn.reshape(1, hh * dv)]
    if has_state:
        st_spec = pl.BlockSpec((1, 1, 1, dk, dv), lambda bi, h: (bi, 0, h, 0, 0))
        in_specs += [st_spec, st_spec]
        args += [s0f, s0b]
    out_specs = [pl.BlockSpec((1, l, dv), lambda bi, h: (bi, 0, h))]
    out_shape = [jax.ShapeDtypeStruct((b, l, hh * dv), BF16)]
    if emit_state:
        so_spec = pl.BlockSpec((1, 1, dk, dv), lambda bi, h: (bi, h, 0, 0))
        out_specs += [so_spec, so_spec]
        out_shape += [jax.ShapeDtypeStruct((b, hh, dk, dv), F32)] * 2
    return pl.pallas_call(
        functools.partial(_ret_kernel, n_chunks=n_chunks, has_state=has_state, emit_state=emit_state),
        grid=(b, hh),
        in_specs=in_specs,
        out_specs=out_specs,
        out_shape=out_shape,
        scratch_shapes=[pltpu.VMEM((dk, dv), F32), pltpu.VMEM((dk, dv), F32),
                        pltpu.VMEM((l, dv), F32), pltpu.VMEM((l, dv), F32)],
        compiler_params=_params("parallel", "parallel"),
        name="retention",
    )(*args)


def _rope_rot(x, cos_t, sin_t):
    lane = lax.broadcasted_iota(jnp.int32, x.shape, x.ndim - 1)
    nxt = pltpu.roll(x, LANES - 1, axis=x.ndim - 1)
    prv = pltpu.roll(x, 1, axis=x.ndim - 1)
    swapped = jnp.where(jnp.bitwise_and(lane, 1) == 0, nxt, prv)
    return x * cos_t + swapped * sin_t


def _att_kernel(q_ref, k_ref, v_ref, qg_ref, kg_ref, *rest, rope, has_cache, emit_kv, groups):
    rest = list(rest)
    cosq_ref = sinq_ref = cosk_ref = sink_ref = ck_ref = cv_ref = kn_ref = vo_ref = None
    if rope:
        cosq_ref, sinq_ref, cosk_ref, sink_ref = rest[:4]
        rest = rest[4:]
    if has_cache:
        ck_ref, cv_ref = rest[:2]
        rest = rest[2:]
    o_ref = rest.pop(0)
    if emit_kv:
        kn_ref, vo_ref = rest[:2]
        rest = rest[2:]
    k_scr, v_scr = rest
    hd = ATT_HD
    exp2_scale = hd ** -0.5 * math.log2(math.e)

    @pl.when(pl.program_id(2) == 0)
    def _():
        kn = _rms(k_ref[0], kg_ref[...])
        if emit_kv:
            kn_ref[0] = kn
            vo_ref[0] = v_ref[0]
        if rope:
            kn = _rope_rot(kn, cosk_ref[...], sink_ref[...])
        k_scr[...] = kn.astype(BF16)
        v_scr[:, :hd] = v_ref[0].astype(BF16)
        v_scr[:, hd:] = jnp.ones((v_scr.shape[0], hd), BF16)

    if has_cache:
        ck = ck_ref[0].astype(BF16)
        cv = jnp.concatenate([cv_ref[0].astype(BF16), jnp.ones((cv_ref.shape[1], hd), BF16)], axis=1)

    for g in range(groups):
        cols = slice(g * hd, (g + 1) * hd)
        qn = _rms(q_ref[0, :, cols], qg_ref[...])
        if rope:
            qn = _rope_rot(qn, cosq_ref[...], sinq_ref[...])
        qb = qn.astype(BF16)
        s1 = _dot_nt(qb, k_scr[...])
        m = jnp.max(s1, axis=-1, keepdims=True)
        if has_cache:
            s2 = _dot_nt(qb, ck)
            m = jnp.maximum(m, jnp.max(s2, axis=-1, keepdims=True))
        p1 = jnp.exp2((s1 - m) * exp2_scale)
        nd = _dot(p1.astype(BF16), v_scr[...])
        if has_cache:
            p2 = jnp.exp2((s2 - m) * exp2_scale)
            nd = nd + _dot(p2.astype(BF16), cv)
        o_ref[0, :, cols] = (nd[:, :hd] * (1.0 / nd[:, hd:])).astype(o_ref.dtype)


def attention_core(proj, q_gain, k_gain, rope_tabs, cache_k, cache_v, b, l, tq, emit_kv):
    hd, kv, heads = ATT_HD, ATT_KV, ATT_HEADS
    groups = heads // kv
    rope = rope_tabs is not None
    has_cache = cache_k is not None
    in_specs = [
        pl.BlockSpec((1, tq, groups * hd), lambda bi, kh, qi: (bi, qi, kh)),
        pl.BlockSpec((1, l, hd), lambda bi, kh, qi: (bi, 0, heads + kh)),
        pl.BlockSpec((1, l, hd), lambda bi, kh, qi: (bi, 0, heads + kv + kh)),
        pl.BlockSpec((1, hd), lambda bi, kh, qi: (0, 0)),
        pl.BlockSpec((1, hd), lambda bi, kh, qi: (0, 0)),
    ]
    args = [proj, proj, proj, q_gain.reshape(1, hd), k_gain.reshape(1, hd)]
    if rope:
        cos_t, sin_t = rope_tabs
        in_specs += [
            pl.BlockSpec((tq, hd), lambda bi, kh, qi: (qi, 0)),
            pl.BlockSpec((tq, hd), lambda bi, kh, qi: (qi, 0)),
            pl.BlockSpec((l, hd), lambda bi, kh, qi: (0, 0)),
            pl.BlockSpec((l, hd), lambda bi, kh, qi: (0, 0)),
        ]
        args += [cos_t, sin_t, cos_t, sin_t]
    if has_cache:
        past = cache_k.shape[1]
        c_spec = pl.BlockSpec((1, past, hd), lambda bi, kh, qi: (bi, 0, kh))
        in_specs += [c_spec, c_spec]
        args += [cache_k, cache_v]
    out_specs = [pl.BlockSpec((1, tq, groups * hd), lambda bi, kh, qi: (bi, qi, kh))]
    out_shape = [jax.ShapeDtypeStruct((b, l, heads * hd), BF16)]
    if emit_kv:
        kv_spec = pl.BlockSpec((1, l, hd), lambda bi, kh, qi: (bi, 0, kh))
        out_specs += [kv_spec, kv_spec]
        out_shape += [jax.ShapeDtypeStruct((b, l, kv * hd), F32)] * 2
    return pl.pallas_call(
        functools.partial(_att_kernel, rope=rope, has_cache=has_cache, emit_kv=emit_kv, groups=groups),
        grid=(b, kv, l // tq),
        in_specs=in_specs,
        out_specs=out_specs,
        out_shape=out_shape,
        scratch_shapes=[pltpu.VMEM((l, hd), BF16), pltpu.VMEM((l, 2 * hd), BF16)],
        compiler_params=_params("parallel", "parallel", "arbitrary"),
        name="attention",
    )(*args)


def _rope_tables(l):
    rows = l // GRID_W
    row = jnp.repeat(jnp.arange(rows, dtype=F32), GRID_W)
    col = jnp.tile(jnp.arange(GRID_W, dtype=F32), rows)
    half = ATT_HD // 2
    inv = ROPE_THETA ** (-jnp.arange(0, half, 2, dtype=F32) / half)
    ang = jnp.concatenate([row[:, None] * inv, col[:, None] * inv], axis=-1)
    cos_t = jnp.repeat(jnp.cos(ang), 2, axis=-1)
    sin_h = jnp.sin(ang)
    sin_t = jnp.stack([-sin_h, sin_h], axis=-1).reshape(l, ATT_HD)
    return cos_t, sin_t


def _ml_kernel(q_ref, k_ref, v_ref, og_ref, gt_ref, gn_ref, *rest, n_chunks, has_state, emit_state):
    rest = list(rest)
    c0f_ref = n0f_ref = m0f_ref = c0b_ref = n0b_ref = m0b_ref = None
    if has_state:
        c0f_ref, n0f_ref, m0f_ref, c0b_ref, n0b_ref, m0b_ref = rest[:6]
        rest = rest[6:]
    o_ref = rest.pop(0)
    outs = None
    if emit_state:
        outs = rest[:6]
        rest = rest[6:]
    cm_f, cm_b, nv_f, nv_b, mm_f, mm_b, hf_scr, hb_scr = rest

    c = CHUNK
    dqk = q_ref.shape[-1]
    k_scale = dqk ** -0.5
    ri = lax.broadcasted_iota(jnp.int32, (c, c), 0)
    ci = lax.broadcasted_iota(jnp.int32, (c, c), 1)
    mask_f = ci <= ri
    mask_b = ci >= ri
    assert c == LANES and dqk == LANES
    dv_tiles = v_ref.shape[-1] // LANES

    def bf3(m):
        m = jnp.where(m, 1.0, 0.0).astype(BF16)
        return jnp.concatenate([m, m, m], axis=1), jnp.concatenate([m, m, m], axis=0)

    mf3, mf3_t = bf3(mask_f)
    mb3, mb3_t = bf3(mask_b)
    eye3, _ = bf3(ci == ri)

    def split3(x):
        hi = x.astype(BF16)
        r1 = x - hi.astype(F32)
        mid = r1.astype(BF16)
        lo = (r1 - mid.astype(F32)).astype(BF16)
        return jnp.concatenate([hi, mid, lo], axis=1)

    def wide(x):
        return jnp.concatenate([x] * dv_tiles, axis=1)

    if has_state:
        cm_f[...] = c0f_ref[0, 0, 0]
        cm_b[...] = c0b_ref[0, 0, 0]
        nv_f[...] = n0f_ref[0, 0]
        nv_b[...] = n0b_ref[0, 0]
        mm_f[...] = jnp.broadcast_to(m0f_ref[0, 0], mm_f.shape)
        mm_b[...] = jnp.broadcast_to(m0b_ref[0, 0], mm_b.shape)
    else:
        for r in (cm_f, cm_b, nv_f, nv_b, mm_f, mm_b):
            r[...] = jnp.zeros_like(r)

    def step(ch, cm, nv, mm, out_scr, mask, m3, m3_t, row_i, row_f):
        off = _chunk_offset(ch, c)
        rows = pl.ds(off, c)
        qc = q_ref[0, rows, :]
        kc = k_ref[0, rows, :]
        vc = v_ref[0, rows, :]
        g_rows = gt_ref[0, 0, :, pl.ds(off, c)]
        i_row = g_rows[row_i:row_i + 1, :]
        f_row = _log_sigmoid(g_rows[row_f:row_f + 1, :])
        f3 = split3(f_row)
        i3 = split3(i_row)
        b_row = _dot(jnp.broadcast_to(f3, (16, 3 * c)), m3_t)[0:1, :]
        b_q = _dot_nt(m3, jnp.broadcast_to(f3, (LANES, 3 * c)))
        i_q = _dot_nt(eye3, jnp.broadcast_to(i3, (LANES, 3 * c)))
        m_old = mm[...]
        dlog = jnp.where(mask, b_q - b_row + i_row, -jnp.inf)
        inter = b_q + m_old
        m_q = jnp.maximum(inter, jnp.max(dlog, axis=-1, keepdims=True))
        w_intra = jnp.exp(dlog - m_q)
        w_x = jnp.exp(inter - m_q)
        s = _dot_nt(qc, kc) * (w_intra * k_scale)
        c_old = cm[...]
        n_old = nv[...]
        num = _dot(s.astype(BF16), vc) + _dot(qc, c_old.astype(BF16)) * wide(w_x)
        qn = _dot_nt(qc, jnp.broadcast_to(n_old, (LANES, dqk)).astype(BF16))
        den = jnp.sum(s, axis=-1, keepdims=True) + qn * w_x
        den = jnp.maximum(jnp.abs(den), jnp.exp(-m_q))
        out_scr[rows, :] = num * wide(1.0 / den)
        b_end = jnp.sum(f_row, axis=-1, keepdims=True)
        wlog = b_end - b_q + i_q
        m_new = jnp.maximum(b_end + m_old, jnp.max(wlog, axis=0, keepdims=True))
        carry_dec = jnp.exp(b_end + m_old - m_new)
        kw = kc.astype(F32) * (jnp.exp(wlog - m_new) * k_scale)
        cm[...] = c_old * wide(carry_dec) + _dot_tn(kw.astype(BF16), vc)
        nv[...] = n_old * carry_dec + jnp.sum(kw, axis=0, keepdims=True)
        mm[...] = m_new

    def fin(ch):
        rows = pl.ds(_chunk_offset(ch, c), c)
        hn = _rms(hf_scr[rows, :] + hb_scr[rows, :], gn_ref[...])
        o_ref[0, rows, :] = (jax.nn.sigmoid(og_ref[0, rows, :].astype(F32)) * hn).astype(o_ref.dtype)

    def body(i, carry):
        step(i, cm_f, nv_f, mm_f, hf_scr, mask_f, mf3, mb3_t, 0, 1)
        step(n_chunks - 1 - i, cm_b, nv_b, mm_b, hb_scr, mask_b, mb3, mf3_t, 2, 3)
        return carry

    _two_phase_loop(n_chunks, body, fin)

    if emit_state:
        for dst, src in zip(outs[0::3] + outs[1::3], (cm_f, cm_b, nv_f, nv_b)):
            dst[0, 0] = src[...]
        for dst, src in zip(outs[2::3], (mm_f, mm_b)):
            dst[0, 0] = src[:, 0:1]


def mlstm_core(proj, gates_t, gn, state, b, l, emit_state):
    hh = ML_HEADS
    width = proj.shape[-1]
    dqk = width // (6 * hh)
    dv = 2 * dqk
    has_state = state is not None
    n_chunks = l // CHUNK
    in_specs = [
        pl.BlockSpec((1, l, dqk), lambda bi, h: (bi, 0, h)),
        pl.BlockSpec((1, l, dqk), lambda bi, h: (bi, 0, hh + h)),
        pl.BlockSpec((1, l, dv), lambda bi, h: (bi, 0, hh + h)),
        pl.BlockSpec((1, l, dv), lambda bi, h: (bi, 0, 2 * hh + h)),
        pl.BlockSpec((1, 1, 8, l), lambda bi, h: (bi, h, 0, 0)),
        pl.BlockSpec((1, dv), lambda bi, h: (0, h)),
    ]
    args = [proj, proj, proj, proj, gates_t, gn.reshape(1, hh * dv)]
    c_spec_in = pl.BlockSpec((1, 1, 1, dqk, dv), lambda bi, h: (bi, 0, h, 0, 0))
    n_spec = pl.BlockSpec((1, 1, 1, dqk), lambda bi, h: (bi, h, 0, 0))
    m_spec = pl.BlockSpec((1, 1, 1, 1), lambda bi, h: (bi, h, 0, 0))
    if has_state:
        cf, nf, mf, cb, nb, mb = state
        in_specs += [c_spec_in, n_spec, m_spec, c_spec_in, n_spec, m_spec]
        args += [cf, nf.reshape(b, hh, 1, dqk), mf.reshape(b, hh, 1, 1),
                 cb, nb.reshape(b, hh, 1, dqk), mb.reshape(b, hh, 1, 1)]
    out_specs = [pl.BlockSpec((1, l, dv), lambda bi, h: (bi, 0, h))]
    out_shape = [jax.ShapeDtypeStruct((b, l, hh * dv), BF16)]
    if emit_state:
        c_spec_out = pl.BlockSpec((1, 1, dqk, dv), lambda bi, h: (bi, h, 0, 0))
        out_specs += [c_spec_out, n_spec, m_spec] * 2
        out_shape += [jax.ShapeDtypeStruct((b, hh, dqk, dv), F32),
                      jax.ShapeDtypeStruct((b, hh, 1, dqk), F32),
                      jax.ShapeDtypeStruct((b, hh, 1, 1), F32)] * 2
    return pl.pallas_call(
        functools.partial(_ml_kernel, n_chunks=n_chunks, has_state=has_state, emit_state=emit_state),
        grid=(b, hh),
        in_specs=in_specs,
        out_specs=out_specs,
        out_shape=out_shape,
        scratch_shapes=[pltpu.VMEM((dqk, dv), F32), pltpu.VMEM((dqk, dv), F32),
                        pltpu.VMEM((1, dqk), F32), pltpu.VMEM((1, dqk), F32),
                        pltpu.VMEM((1, LANES), F32), pltpu.VMEM((1, LANES), F32),
                        pltpu.VMEM((l, dv), F32), pltpu.VMEM((l, dv), F32)],
        compiler_params=_params("parallel", "parallel"),
        name="mlstm",
    )(*args)


def _hy_filter_kernel(feat_ref, w1_ref, b1_ref, w2_ref, b2_ref, w3_ref, fr_ref, win_ref, sum_ref, dif_ref):
    d = win_ref.shape[-1]
    z = jnp.sin(fr_ref[0:1, :] * (_dot(feat_ref[...].astype(BF16), w1_ref[...].astype(BF16)) + b1_ref[...]))
    z = jnp.sin(fr_ref[1:2, :] * (_dot(z.astype(BF16), w2_ref[...].astype(BF16)) + b2_ref[...]))
    filt = _dot(z.astype(BF16), w3_ref[...].astype(BF16))
    win = win_ref[...]
    ff = filt[:, :d] * win
    fb = filt[:, d:] * win
    sum_ref[...] = (ff + fb).astype(sum_ref.dtype)
    dif_ref[...] = (ff - fb).astype(dif_ref.dtype)


def hyena_filters(l, d, w1, b1, w2, b2, w3, freq):
    t = jnp.linspace(0.0, 1.0, l, dtype=F32)[:, None]
    pos = jnp.arange(l, dtype=F32)[:, None]
    bands = jnp.linspace(1e-4, HY_BANDS - 1, HY_BANDS, dtype=F32)
    ang = 2.0 * math.pi * pos * bands / l
    feats = jnp.concatenate([t, jnp.cos(ang), -jnp.sin(ang)], axis=-1)
    emb = feats.shape[1]
    feats = jnp.pad(feats, ((0, 0), (0, LANES - emb)))
    w1p = jnp.pad(w1, ((0, LANES - emb), (0, 0)))
    deltas = jnp.abs(jnp.linspace(math.log(HY_TARGET) / HY_SLOW_DECAY,
                                  math.log(HY_TARGET) / HY_FAST_DECAY, d, dtype=F32))
    window = jnp.exp(-t * deltas)
    hid = w2.shape[0]
    tl = min(l, 512)
    full = lambda shp: pl.BlockSpec(shp, lambda i: (0,) * len(shp))
    return pl.pallas_call(
        _hy_filter_kernel,
        grid=(l // tl,),
        in_specs=[
            pl.BlockSpec((tl, LANES), lambda i: (i, 0)),
            full((LANES, hid)), full((1, hid)), full((hid, hid)), full((1, hid)), full((hid, 2 * d)),
            full((2, hid)),
            pl.BlockSpec((tl, d), lambda i: (i, 0)),
        ],
        out_specs=[pl.BlockSpec((tl, d), lambda i: (i, 0))] * 2,
        out_shape=[jax.ShapeDtypeStruct((l, d), BF16)] * 2,
        compiler_params=_params("parallel"),
        name="hyena_filter",
    )(feats, w1p, b1.reshape(1, hid), w2, b2.reshape(1, hid), w3, freq, window)


def _dft_tables(l):
    n = 2 * l
    blk = 64
    assert l % blk == 0

    def trig(step, count):
        k = lax.broadcasted_iota(jnp.int32, (l, count), 0)
        t = lax.broadcasted_iota(jnp.int32, (l, count), 1)
        ang = (((2 * k + 1) * step * t) % (2 * n)).astype(F32) * (math.pi / n)
        return jnp.cos(ang), jnp.sin(ang)

    c1, s1 = trig(blk, l // blk)
    c0, s0 = trig(1, blk)
    cos_kt = (c1[:, :, None] * c0[:, None, :] - s1[:, :, None] * s0[:, None, :]).reshape(l, l).astype(BF16)
    sin_kt = (s1[:, :, None] * c0[:, None, :] + c1[:, :, None] * s0[:, None, :]).reshape(l, l).astype(BF16)
    c1t, s1t, c0t, s0t = c1.T, s1.T, c0.T, s0.T
    cos_tk = (c1t[:, None, :] * c0t[None, :, :] - s1t[:, None, :] * s0t[None, :, :]).reshape(l, l).astype(BF16)
    sin_tk = (s1t[:, None, :] * c0t[None, :, :] + c1t[:, None, :] * s0t[None, :, :]).reshape(l, l).astype(BF16)
    return cos_kt, sin_kt, cos_tk, sin_tk


def _hy_spec_kernel(c_ref, s_ref, fs_ref, fd_ref, bias_ref, gr_ref, gs_ref):
    gr_ref[...] = _dot(c_ref[...], fs_ref[...]) + bias_ref[...]
    gs_ref[...] = _dot(s_ref[...], fd_ref[...])


def hyena_filter_spectrum(cos_kt, sin_kt, f_sum, f_dif, f_bias):
    l, d = f_sum.shape
    tm = min(l, 512)
    tn = min(d, 512)
    return pl.pallas_call(
        _hy_spec_kernel,
        grid=(l // tm, d // tn),
        in_specs=[
            pl.BlockSpec((tm, l), lambda i, j: (i, 0)),
            pl.BlockSpec((tm, l), lambda i, j: (i, 0)),
            pl.BlockSpec((l, tn), lambda i, j: (0, j)),
            pl.BlockSpec((l, tn), lambda i, j: (0, j)),
            pl.BlockSpec((1, tn), lambda i, j: (0, j)),
        ],
        out_specs=[pl.BlockSpec((tm, tn), lambda i, j: (i, j))] * 2,
        out_shape=[jax.ShapeDtypeStruct((l, d), F32)] * 2,
        compiler_params=_params("parallel", "parallel"),
        name="hyena_filter_spectrum",
    )(cos_kt, sin_kt, f_sum, f_dif, f_bias.reshape(1, d))


def _hy_conv_kernel(p0_ref, p1_ref, pv_ref, w0_ref, w1_ref, wv_ref, b0_ref, b1_ref, bv_ref, z_ref, x0_ref):
    l = p0_ref.shape[1]
    t = lax.broadcasted_iota(jnp.int32, (l, 1), 0)

    def conv(p_ref, w_ref, b_ref):
        p = p_ref[0].astype(F32)
        prev = jnp.where(t == 0, 0.0, pltpu.roll(p, 1, axis=0))
        nxt = jnp.where(t == l - 1, 0.0, pltpu.roll(p, l - 1, axis=0))
        return b_ref[...] + prev * w_ref[0:1, :] + p * w_ref[1:2, :] + nxt * w_ref[2:3, :]

    x0_ref[0] = conv(p0_ref, w0_ref, b0_ref)
    z_ref[0] = (conv(pv_ref, wv_ref, bv_ref) * conv(p1_ref, w1_ref, b1_ref)).astype(z_ref.dtype)


def hyena_short_conv(proj, conv_w, conv_b, b, l, d):
    tc = min(d, 512)
    nc = d // tc
    p_spec = lambda off: pl.BlockSpec((1, l, tc), lambda bi, j: (bi, 0, off * nc + j))
    w_spec = lambda off: pl.BlockSpec((HY_SHORT, tc), lambda bi, j: (0, off * nc + j))
    b_spec = lambda off: pl.BlockSpec((1, tc), lambda bi, j: (0, off * nc + j))
    cb = conv_b.reshape(1, 3 * d)
    return pl.pallas_call(
        _hy_conv_kernel,
        grid=(b, nc),
        in_specs=[p_spec(0), p_spec(1), p_spec(2), w_spec(0), w_spec(1), w_spec(2),
                  b_spec(0), b_spec(1), b_spec(2)],
        out_specs=[pl.BlockSpec((1, l, tc), lambda bi, j: (bi, 0, j))] * 2,
        out_shape=[jax.ShapeDtypeStruct((b, l, d), BF16), jax.ShapeDtypeStruct((b, l, d), F32)],
        compiler_params=_params("parallel", "parallel"),
        name="hyena_short_conv",
    )(proj, proj, proj, conv_w, conv_w, conv_w, cb, cb, cb)


def _hy_fwd_kernel(c_ref, s_ref, z_ref, gr_ref, gs_ref, yr_ref, ys_ref):
    z = z_ref[0]
    zr = _dot(c_ref[...], z)
    zs = _dot(s_ref[...], z)
    gr = gr_ref[...]
    gs = gs_ref[...]
    yr_ref[0] = (zr * gr - zs * gs).astype(yr_ref.dtype)
    ys_ref[0] = (zr * gs + zs * gr).astype(ys_ref.dtype)


def hyena_forward_dft(cos_kt, sin_kt, z, g_r, g_s):
    b, l, d = z.shape
    tm = min(l, 512)
    tn = min(d, 512)
    return pl.pallas_call(
        _hy_fwd_kernel,
        grid=(l // tm, b, d // tn),
        in_specs=[
            pl.BlockSpec((tm, l), lambda i, bi, j: (i, 0)),
            pl.BlockSpec((tm, l), lambda i, bi, j: (i, 0)),
            pl.BlockSpec((1, l, tn), lambda i, bi, j: (bi, 0, j)),
            pl.BlockSpec((tm, tn), lambda i, bi, j: (i, j)),
            pl.BlockSpec((tm, tn), lambda i, bi, j: (i, j)),
        ],
        out_specs=[pl.BlockSpec((1, tm, tn), lambda i, bi, j: (bi, i, j))] * 2,
        out_shape=[jax.ShapeDtypeStruct((b, l, d), BF16)] * 2,
        compiler_params=_params("parallel", "parallel", "parallel"),
        name="hyena_forward_dft",
    )(cos_kt, sin_kt, z, g_r, g_s)


def _hy_inv_kernel(ct_ref, st_ref, yr_ref, ys_ref, x0_ref, o_ref, *, inv_scale):
    y = _dot(ct_ref[...], yr_ref[0]) + _dot(st_ref[...], ys_ref[0])
    o_ref[0] = (y * inv_scale * x0_ref[0]).astype(o_ref.dtype)


def hyena_inverse_dft(cos_tk, sin_tk, y_r, y_s, x0):
    b, l, d = y_r.shape
    tm = min(l, 512)
    tn = min(d, 512)
    return pl.pallas_call(
        functools.partial(_hy_inv_kernel, inv_scale=1.0 / l),
        grid=(l // tm, b, d // tn),
        in_specs=[
            pl.BlockSpec((tm, l), lambda i, bi, j: (i, 0)),
            pl.BlockSpec((tm, l), lambda i, bi, j: (i, 0)),
            pl.BlockSpec((1, l, tn), lambda i, bi, j: (bi, 0, j)),
            pl.BlockSpec((1, l, tn), lambda i, bi, j: (bi, 0, j)),
            pl.BlockSpec((1, tm, tn), lambda i, bi, j: (bi, i, j)),
        ],
        out_specs=pl.BlockSpec((1, tm, tn), lambda i, bi, j: (bi, i, j)),
        out_shape=jax.ShapeDtypeStruct((b, l, d), BF16),
        compiler_params=_params("parallel", "parallel", "parallel"),
        name="hyena_inverse_dft",
    )(cos_tk, sin_tk, y_r, y_s, x0)


def _tile_rows(grp):
    span = grp.l if grp.per_batch else grp.t
    return next(tm for tm in (1024, 512, 256, 128) if span % tm == 0)


def kernel(x_prompt, x_sample, cache_k, cache_v, state_ret_fwd, state_ret_bwd, state_ml_C_fwd, state_ml_n_fwd, state_ml_m_fwd, state_ml_C_bwd, state_ml_n_bwd, state_ml_m_bwd, c, c_ctx, mod_w, mod_b, norm_mix_pre, norm_mix_post, norm_ffn_pre, norm_ffn_post, mlp_w1, mlp_w2, ret_w_in, ret_decay_fwd, ret_decay_bwd, ret_gn, ret_w_out, att_w_in, att_q_gain, att_k_gain, att_w_out, ml_w_in, ml_gate_b, ml_gn, ml_w_out, hy_w_in, hy_b_in, hy_conv_w, hy_conv_b, hy_f_w1, hy_f_b1, hy_f_w2, hy_f_b2, hy_f_w3, hy_sin_freq, hy_f_bias, hy_w_out):
    bp, lp, d = x_prompt.shape
    bs, ls, _ = x_sample.shape
    depth = mod_w.shape[0]
    n_mixers = 4
    mod_rows = 16
    assert 1 + bs <= mod_rows

    grp_p = Group(bp, lp, 0, False)
    grp_s = Group(bs, ls, 1, True)
    groups = (grp_p, grp_s)

    cond = jnp.concatenate([c_ctx[None, :], c, jnp.zeros((mod_rows - 1 - bs, d), F32)], axis=0)
    mod_all = adaln_all(cond, mod_w, mod_b)

    xs = [x_prompt.reshape(grp_p.t, d), x_sample.reshape(grp_s.t, d)]
    new_k = new_v = new_rf = new_rb = None
    new_ml = None

    for i in range(depth):
        mixer = i % n_mixers
        j = i // n_mixers
        mod3 = mod_all[i].reshape(mod_rows, 1, 6 * d)
        ys = []
        for gi, grp in enumerate(groups):
            x = xs[gi]
            tm = _tile_rows(grp)
            is_prompt = gi == 0
            if mixer == 0:
                w_in = ret_w_in[j].astype(BF16)
                proj = norm_matmul(x, norm_mix_pre[i], mod3, 0, 1, w_in, None, BF16, grp, tm, PROJ_TILE_N)
                dec = jnp.stack([ret_decay_fwd[j], ret_decay_bwd[j]]).astype(F32)
                s0f = None if is_prompt else state_ret_fwd
                s0b = None if is_prompt else state_ret_bwd
                assert is_prompt or state_ret_fwd.shape[1] == 1
                res = retention_core(proj.reshape(grp.b, grp.l, -1), dec, ret_gn[j], s0f, s0b,
                                     grp.b, grp.l, emit_state=is_prompt)
                if is_prompt:
                    new_rf, new_rb = res[1][:, None], res[2][:, None]
                a = res[0].reshape(grp.t, -1)
                w_out = ret_w_out[j].astype(BF16)
            elif mixer == 1:
                w_in = att_w_in[j].astype(BF16)
                proj = norm_matmul(x, norm_mix_pre[i], mod3, 0, 1, w_in, None, F32, grp, tm, PROJ_TILE_N)
                proj = proj.reshape(grp.b, grp.l, -1)
                if is_prompt:
                    res = attention_core(proj, att_q_gain[j], att_k_gain[j], None, None, None,
                                         grp.b, grp.l, min(grp.l, 256), emit_kv=True)
                    new_k = res[1].reshape(grp.b, 1, grp.l, ATT_KV, ATT_HD)
                    new_v = res[2].reshape(grp.b, 1, grp.l, ATT_KV, ATT_HD)
                else:
                    assert cache_k.shape[1] == 1
                    ck = cache_k.reshape(grp.b, cache_k.shape[2], ATT_KV * ATT_HD)
                    cv = cache_v.reshape(grp.b, cache_v.shape[2], ATT_KV * ATT_HD)
                    res = attention_core(proj, att_q_gain[j], att_k_gain[j], _rope_tables(grp.l), ck, cv,
                                         grp.b, grp.l, min(grp.l, 256), emit_kv=False)
                a = res[0].reshape(grp.t, -1)
                w_out = att_w_out[j].astype(BF16)
            elif mixer == 2:
                hh = ML_HEADS
                n_main = ml_w_in.shape[2] - 4 * hh
                w_main = ml_w_in[j][:, :n_main].astype(BF16)
                w_gate = jnp.pad(ml_w_in[j][:, n_main:], ((0, 0), (0, LANES - 4 * hh))).astype(BF16)
                b_gate = jnp.pad(ml_gate_b[j], (0, LANES - 4 * hh))
                proj = norm_matmul(x, norm_mix_pre[i], mod3, 0, 1, w_main, None, BF16, grp, tm, PROJ_TILE_N)
                gates = norm_matmul(x, norm_mix_pre[i], mod3, 0, 1, w_gate, b_gate, F32, grp, tm, LANES)
                gates = gates[:, :4 * hh].reshape(grp.b, grp.l, 4, hh)
                gates_t = jnp.pad(gates.transpose(0, 3, 2, 1), ((0, 0), (0, 0), (0, 4), (0, 0)))
                state = None
                if not is_prompt:
                    assert state_ml_C_fwd.shape[1] == 1
                    state = (state_ml_C_fwd, state_ml_n_fwd, state_ml_m_fwd,
                             state_ml_C_bwd, state_ml_n_bwd, state_ml_m_bwd)
                res = mlstm_core(proj.reshape(grp.b, grp.l, -1), gates_t, ml_gn[j], state,
                                 grp.b, grp.l, emit_state=is_prompt)
                if is_prompt:
                    dqk = res[2].shape[-1]
                    new_ml = (res[1][:, None], res[2].reshape(grp.b, 1, hh, dqk), res[3].reshape(grp.b, 1, hh),
                              res[4][:, None], res[5].reshape(grp.b, 1, hh, dqk), res[6].reshape(grp.b, 1, hh))
                a = res[0].reshape(grp.t, -1)
                w_out = ml_w_out[j].astype(BF16)
            else:
                w_in = hy_w_in[j].astype(BF16)
                proj = norm_matmul(x, norm_mix_pre[i], mod3, 0, 1, w_in, hy_b_in[j], F32, grp, tm, PROJ_TILE_N)
                z, x0 = hyena_short_conv(proj.reshape(grp.b, grp.l, 3 * d), hy_conv_w[j], hy_conv_b[j],
                                         grp.b, grp.l, d)
                f_sum, f_dif = hyena_filters(grp.l, d, hy_f_w1[j], hy_f_b1[j], hy_f_w2[j], hy_f_b2[j],
                                             hy_f_w3[j], hy_sin_freq[j])
                cos_kt, sin_kt, cos_tk, sin_tk = _dft_tables(grp.l)
                g_r, g_s = hyena_filter_spectrum(cos_kt, sin_kt, f_sum, f_dif, hy_f_bias[j])
                y_r, y_s = hyena_forward_dft(cos_kt, sin_kt, z, g_r, g_s)
                a = hyena_inverse_dft(cos_tk, sin_tk, y_r, y_s, x0).reshape(grp.t, d)
                w_out = hy_w_out[j].astype(BF16)
            x = matmul_resnorm(a, w_out, x, mod3, 2, norm_mix_post[i], grp, tm)
            x = mlp_block(x, norm_ffn_pre[i], norm_ffn_post[i], mod3,
                          mlp_w1[i].astype(BF16), mlp_w2[i].astype(BF16), grp, tm, MLP_TILE_F)
            ys.append(x)
        xs = ys

    y_prompt = xs[0].reshape(bp, lp, d)
    y_sample = xs[1].reshape(bs, ls, d)
    return (y_prompt, y_sample, new_k, new_v, new_rf, new_rb) + tuple(new_ml)
```

```python
import functools
import math

import jax
import jax.numpy as jnp
import numpy as np
from jax import lax
from jax.experimental import pallas as pl
from jax.experimental.pallas import tpu as pltpu

F32 = jnp.float32
BF16 = jnp.bfloat16

EPS = 1e-6
CHUNK = 128
RET_CHUNK = 256
GRID_W = 64
ROPE_THETA = 10000.0

RET_HEADS = 4
ATT_HEADS = 8
ATT_KV = 2
ATT_HD = 128
ML_HEADS = 4

HY_BANDS = 16
HY_SHORT = 3
HY_FAST_DECAY = 0.3
HY_SLOW_DECAY = 1.5
HY_TARGET = 1e-2

VMEM_LIMIT_BYTES = 56 * 1024 * 1024
LANES = 128
PROJ_TILE_N = 1536
MLP_TILE_F = 1024


def _params(*sem):
    return pltpu.CompilerParams(dimension_semantics=sem, vmem_limit_bytes=VMEM_LIMIT_BYTES)


def _dot(a, b):
    return jnp.dot(a, b, preferred_element_type=F32)


def _dot_nt(a, b):
    return lax.dot_general(a, b, (((1,), (1,)), ((), ())), preferred_element_type=F32)


def _dot_tn(a, b):
    return lax.dot_general(a, b, (((0,), (0,)), ((), ())), preferred_element_type=F32)


def _rms(x, g):
    return x * lax.rsqrt(jnp.mean(x * x, axis=-1, keepdims=True) + EPS) * g


def _log_sigmoid(x):
    return jnp.minimum(x, 0.0) - jnp.log1p(jnp.exp(-jnp.abs(x)))


def _chunk_offset(ch, c):
    return ch * c if isinstance(ch, int) else pl.multiple_of(ch * c, c)


def _two_phase_loop(n_chunks, body, fin, max_unroll=2):
    if n_chunks == 1:
        body(0, 0)
        fin(0)
        return
    assert n_chunks % 2 == 0
    half = n_chunks // 2
    unroll = math.gcd(half, max_unroll)

    def body_fin(i, carry):
        body(i, carry)
        fin(i)
        fin(n_chunks - 1 - i)
        return carry

    lax.fori_loop(0, half, body, 0, unroll=unroll)
    lax.fori_loop(half, n_chunks, body_fin, 0, unroll=unroll)


class Group:
    def __init__(self, b, l, row0, per_batch):
        self.b, self.l, self.row0, self.per_batch = b, l, row0, per_batch
        self.t = b * l

    def mod_spec(self, chunk, tm, d):
        row0, per_batch, l = self.row0, self.per_batch, self.l
        if per_batch:
            return pl.BlockSpec((1, 1, d), lambda i, *_: (row0 + (i * tm) // l, 0, chunk))
        return pl.BlockSpec((1, 1, d), lambda i, *_: (row0, 0, chunk))


def _adaln_kernel(c_ref, w_ref, b_ref, o_ref):
    s = jax.nn.silu(c_ref[...])
    o_ref[0] = _dot(s.astype(BF16), w_ref[0].astype(BF16)) + b_ref[0]


def adaln_all(cond, mod_w, mod_b):
    depth, d, n = mod_w.shape
    rows = cond.shape[0]
    tn = 768
    return pl.pallas_call(
        _adaln_kernel,
        grid=(depth, n // tn),
        in_specs=[
            pl.BlockSpec((rows, d), lambda l, j: (0, 0)),
            pl.BlockSpec((1, d, tn), lambda l, j: (l, 0, j)),
            pl.BlockSpec((1, 1, tn), lambda l, j: (l, 0, j)),
        ],
        out_specs=pl.BlockSpec((1, rows, tn), lambda l, j: (l, 0, j)),
        out_shape=jax.ShapeDtypeStruct((depth, rows, n), F32),
        compiler_params=_params("parallel", "parallel"),
        name="adaln",
    )(cond, mod_w, mod_b.reshape(depth, 1, n))


def _norm_mm_kernel(x_ref, g_ref, sh_ref, sc_ref, w_ref, *rest, has_bias):
    if has_bias:
        b_ref, o_ref, h_scr = rest
    else:
        o_ref, h_scr = rest

    @pl.when(pl.program_id(1) == 0)
    def _():
        y = _rms(x_ref[...], g_ref[...])
        h_scr[...] = (y * (1.0 + sc_ref[0]) + sh_ref[0]).astype(BF16)

    acc = _dot(h_scr[...], w_ref[...])
    if has_bias:
        acc = acc + b_ref[...]
    o_ref[...] = acc.astype(o_ref.dtype)


def norm_matmul(x, gain, mod3, sh_idx, sc_idx, w, bias, out_dtype, grp, tm, tn):
    t, d = x.shape
    n = w.shape[1]
    in_specs = [
        pl.BlockSpec((tm, d), lambda i, j: (i, 0)),
        pl.BlockSpec((1, d), lambda i, j: (0, 0)),
        grp.mod_spec(sh_idx, tm, d),
        grp.mod_spec(sc_idx, tm, d),
        pl.BlockSpec((d, tn), lambda i, j: (0, j)),
    ]
    args = [x, gain.reshape(1, d), mod3, mod3, w]
    if bias is not None:
        in_specs.append(pl.BlockSpec((1, tn), lambda i, j: (0, j)))
        args.append(bias.reshape(1, n))
    return pl.pallas_call(
        functools.partial(_norm_mm_kernel, has_bias=bias is not None),
        grid=(t // tm, n // tn),
        in_specs=in_specs,
        out_specs=pl.BlockSpec((tm, tn), lambda i, j: (i, j)),
        out_shape=jax.ShapeDtypeStruct((t, n), out_dtype),
        scratch_shapes=[pltpu.VMEM((tm, d), BF16)],
        compiler_params=_params("parallel", "arbitrary"),
        name="norm_matmul",
    )(*args)


def _mm_res_kernel(a_ref, w_ref, x_ref, gate_ref, pg_ref, ng_ref, sh_ref, sc_ref, o_ref, h_ref):
    y = _dot(a_ref[...], w_ref[...])
    x1 = x_ref[...] + gate_ref[0] * _rms(y, pg_ref[...])
    o_ref[...] = x1
    h_ref[...] = (_rms(x1, ng_ref[...]) * (1.0 + sc_ref[0]) + sh_ref[0]).astype(h_ref.dtype)


def matmul_resnorm(a, w, x, mod3, post_gain, next_gain, grp, tm):
    t, k = a.shape
    d = w.shape[1]
    row = pl.BlockSpec((tm, d), lambda i: (i, 0))
    vec = pl.BlockSpec((1, d), lambda i: (0, 0))
    return pl.pallas_call(
        _mm_res_kernel,
        grid=(t // tm,),
        in_specs=[
            pl.BlockSpec((tm, k), lambda i: (i, 0)),
            pl.BlockSpec((k, d), lambda i: (0, 0)),
            row,
            grp.mod_spec(2, tm, d),
            vec,
            vec,
            grp.mod_spec(3, tm, d),
            grp.mod_spec(4, tm, d),
        ],
        out_specs=[row, row],
        out_shape=[jax.ShapeDtypeStruct((t, d), F32), jax.ShapeDtypeStruct((t, d), BF16)],
        compiler_params=_params("parallel"),
        name="out_proj",
    )(a, w, x, mod3, post_gain.reshape(1, d), next_gain.reshape(1, d), mod3, mod3)


def _mlp_kernel(h_ref, x_ref, w1_ref, w2_ref, gate_ref, pg_ref, o_ref, acc_scr):
    j = pl.program_id(1)

    @pl.when(j == 0)
    def _():
        acc_scr[...] = jnp.zeros_like(acc_scr)

    u = _dot(h_ref[...], w1_ref[...])
    u = jnp.square(jnp.maximum(u, 0.0)).astype(BF16)
    acc_scr[...] += _dot(u, w2_ref[...])

    @pl.when(j == pl.num_programs(1) - 1)
    def _():
        o_ref[...] = x_ref[...] + gate_ref[0] * _rms(acc_scr[...], pg_ref[...])


def mlp_block(h, x, post_gain, mod3, w1, w2, grp, tm, tf):
    t, d = x.shape
    f = w1.shape[1]
    return pl.pallas_call(
        _mlp_kernel,
        grid=(t // tm, f // tf),
        in_specs=[
            pl.BlockSpec((tm, d), lambda i, j: (i, 0)),
            pl.BlockSpec((tm, d), lambda i, j: (i, 0)),
            pl.BlockSpec((d, tf), lambda i, j: (0, j)),
            pl.BlockSpec((tf, d), lambda i, j: (j, 0)),
            grp.mod_spec(5, tm, d),
            pl.BlockSpec((1, d), lambda i, j: (0, 0)),
        ],
        out_specs=pl.BlockSpec((tm, d), lambda i, j: (i, 0)),
        out_shape=jax.ShapeDtypeStruct((t, d), F32),
        scratch_shapes=[pltpu.VMEM((tm, d), F32)],
        compiler_params=_params("parallel", "arbitrary"),
        name="mlp",
    )(h, x, w1, w2, mod3, post_gain.reshape(1, d))


def _ret_kernel(dec_ref, q_ref, k_ref, v_ref, g_ref, gn_ref, *rest, n_chunks, has_state, emit_state):
    rest = list(rest)
    s0f_ref = s0b_ref = sf_ref = sb_ref = None
    if has_state:
        s0f_ref, s0b_ref = rest[:2]
        rest = rest[2:]
    o_ref = rest.pop(0)
    if emit_state:
        sf_ref, sb_ref = rest[:2]
        rest = rest[2:]
    st_f, st_b, of_scr, ob_scr = rest

    c = RET_CHUNK
    dk = q_ref.shape[-1]
    h = pl.program_id(1)
    lg_f = _log_sigmoid(jnp.full((1, 1), dec_ref[0, h], F32))
    lg_b = _log_sigmoid(jnp.full((1, 1), dec_ref[1, h], F32))
    ri = lax.broadcasted_iota(jnp.int32, (c, c), 0)
    ci = lax.broadcasted_iota(jnp.int32, (c, c), 1)
    rel = (ri - ci).astype(F32)
    intra_f = jnp.where(rel >= 0, jnp.exp(lg_f * jnp.maximum(rel, 0.0)), 0.0)
    intra_b = jnp.where(rel <= 0, jnp.exp(lg_b * jnp.maximum(-rel, 0.0)), 0.0)
    idx = lax.broadcasted_iota(jnp.int32, (c, 1), 0).astype(F32)
    qdec_f = jnp.exp(lg_f * (idx + 1.0))
    kdec_f = jnp.exp(lg_f * (c - 1.0 - idx))
    qdec_b = jnp.exp(lg_b * (c - idx))
    kdec_b = jnp.exp(lg_b * idx)
    cdec_f = jnp.exp(lg_f * c)
    cdec_b = jnp.exp(lg_b * c)
    q_scale = dk ** -0.5

    if has_state:
        st_f[...] = s0f_ref[0, 0, 0]
        st_b[...] = s0b_ref[0, 0, 0]
    else:
        st_f[...] = jnp.zeros_like(st_f)
        st_b[...] = jnp.zeros_like(st_b)

    def step(ch, st, out_scr, intra, qdec, kdec, cdec):
        rows = pl.ds(_chunk_offset(ch, c), c)
        qc = q_ref[0, rows, :]
        kc = k_ref[0, rows, :]
        vc = v_ref[0, rows, :]
        s_old = st[...]
        sc = _dot_nt(qc, kc) * (intra * q_scale)
        o = _dot(sc.astype(BF16), vc) + _dot(qc, s_old.astype(BF16)) * (qdec * q_scale)
        kw = (kc.astype(F32) * kdec).astype(BF16)
        st[...] = s_old * cdec + _dot_tn(kw, vc)
        out_scr[rows, :] = o

    def fin(ch):
        rows = pl.ds(_chunk_offset(ch, c), c)
        o = _rms(of_scr[rows, :] + ob_scr[rows, :], gn_ref[...])
        o_ref[0, rows, :] = (jax.nn.silu(g_ref[0, rows, :].astype(F32)) * o).astype(o_ref.dtype)

    def body(i, carry):
        step(i, st_f, of_scr, intra_f, qdec_f, kdec_f, cdec_f)
        step(n_chunks - 1 - i, st_b, ob_scr, intra_b, qdec_b, kdec_b, cdec_b)
        return carry

    _two_phase_loop(n_chunks, body, fin)

    if emit_state:
        sf_ref[0, 0] = st_f[...]
        sb_ref[0, 0] = st_b[...]


def retention_core(proj, dec, gn, s0f, s0b, b, l, emit_state):
    hh = RET_HEADS
    width = proj.shape[-1]
    dk = width // (6 * hh)
    dv = 2 * dk
    has_state = s0f is not None
    n_chunks = l // RET_CHUNK
    in_specs = [
        pl.BlockSpec(memory_space=pltpu.SMEM),
        pl.BlockSpec((1, l, dk), lambda bi, h: (bi, 0, h)),
        pl.BlockSpec((1, l, dk), lambda bi, h: (bi, 0, hh + h)),
        pl.BlockSpec((1, l, dv), lambda bi, h: (bi, 0, hh + h)),
        pl.BlockSpec((1, l, dv), lambda bi, h: (bi, 0, 2 * hh + h)),
        pl.BlockSpec((1, dv), lambda bi, h: (0, h)),
    ]
    args = [dec, proj, proj, proj, proj, gn.reshape(1, hh * dv)]
    if has_state:
        st_spec = pl.BlockSpec((1, 1, 1, dk, dv), lambda bi, h: (bi, 0, h, 0, 0))
        in_specs += [st_spec, st_spec]
        args += [s0f, s0b]
    out_specs = [pl.BlockSpec((1, l, dv), lambda bi, h: (bi, 0, h))]
    out_shape = [jax.ShapeDtypeStruct((b, l, hh * dv), BF16)]
    if emit_state:
        so_spec = pl.BlockSpec((1, 1, dk, dv), lambda bi, h: (bi, h, 0, 0))
        out_specs += [so_spec, so_spec]
        out_shape += [jax.ShapeDtypeStruct((b, hh, dk, dv), F32)] * 2
    return pl.pallas_call(
        functools.partial(_ret_kernel, n_chunks=n_chunks, has_state=has_state, emit_state=emit_state),
        grid=(b, hh),
        in_specs=in_specs,
        out_specs=out_specs,
        out_shape=out_shape,
        scratch_shapes=[pltpu.VMEM((dk, dv), F32), pltpu.VMEM((dk, dv), F32),
                        pltpu.VMEM((l, dv), F32), pltpu.VMEM((l, dv), F32)],
        compiler_params=_params("parallel", "parallel"),
        name="retention",
    )(*args)


def _rope_rot(x, cos_t, sin_t):
    lane = lax.broadcasted_iota(jnp.int32, x.shape, x.ndim - 1)
    nxt = pltpu.roll(x, LANES - 1, axis=x.ndim - 1)
    prv = pltpu.roll(x, 1, axis=x.ndim - 1)
    swapped = jnp.where(jnp.bitwise_and(lane, 1) == 0, nxt, prv)
    return x * cos_t + swapped * sin_t


def _att_kernel(q_ref, k_ref, v_ref, qg_ref, kg_ref, *rest, rope, has_cache, emit_kv, groups):
    rest = list(rest)
    cosq_ref = sinq_ref = cosk_ref = sink_ref = ck_ref = cv_ref = kn_ref = vo_ref = None
    if rope:
        cosq_ref, sinq_ref, cosk_ref, sink_ref = rest[:4]
        rest = rest[4:]
    if has_cache:
        ck_ref, cv_ref = rest[:2]
        rest = rest[2:]
    o_ref = rest.pop(0)
    if emit_kv:
        kn_ref, vo_ref = rest[:2]
        rest = rest[2:]
    k_scr, v_scr = rest
    hd = ATT_HD
    exp2_scale = hd ** -0.5 * math.log2(math.e)

    @pl.when(pl.program_id(2) == 0)
    def _():
        kn = _rms(k_ref[0], kg_ref[...])
        if emit_kv:
            kn_ref[0] = kn
            vo_ref[0] = v_ref[0]
        if rope:
            kn = _rope_rot(kn, cosk_ref[...], sink_ref[...])
        k_scr[...] = kn.astype(BF16)
        v_scr[:, :hd] = v_ref[0].astype(BF16)
        v_scr[:, hd:] = jnp.ones((v_scr.shape[0], hd), BF16)

    if has_cache:
        ck = ck_ref[0].astype(BF16)
        cv = jnp.concatenate([cv_ref[0].astype(BF16), jnp.ones((cv_ref.shape[1], hd), BF16)], axis=1)

    for g in range(groups):
        cols = slice(g * hd, (g + 1) * hd)
        qn = _rms(q_ref[0, :, cols], qg_ref[...])
        if rope:
            qn = _rope_rot(qn, cosq_ref[...], sinq_ref[...])
        qb = qn.astype(BF16)
        s1 = _dot_nt(qb, k_scr[...])
        m = jnp.max(s1, axis=-1, keepdims=True)
        if has_cache:
            s2 = _dot_nt(qb, ck)
            m = jnp.maximum(m, jnp.max(s2, axis=-1, keepdims=True))
        p1 = jnp.exp2((s1 - m) * exp2_scale)
        nd = _dot(p1.astype(BF16), v_scr[...])
        if has_cache:
            p2 = jnp.exp2((s2 - m) * exp2_scale)
            nd = nd + _dot(p2.astype(BF16), cv)
        o_ref[0, :, cols] = (nd[:, :hd] * (1.0 / nd[:, hd:])).astype(o_ref.dtype)


def attention_core(proj, q_gain, k_gain, rope_tabs, cache_k, cache_v, b, l, tq, emit_kv):
    hd, kv, heads = ATT_HD, ATT_KV, ATT_HEADS
    groups = heads // kv
    rope = rope_tabs is not None
    has_cache = cache_k is not None
    in_specs = [
        pl.BlockSpec((1, tq, groups * hd), lambda bi, kh, qi: (bi, qi, kh)),
        pl.BlockSpec((1, l, hd), lambda bi, kh, qi: (bi, 0, heads + kh)),
        pl.BlockSpec((1, l, hd), lambda bi, kh, qi: (bi, 0, heads + kv + kh)),
        pl.BlockSpec((1, hd), lambda bi, kh, qi: (0, 0)),
        pl.BlockSpec((1, hd), lambda bi, kh, qi: (0, 0)),
    ]
    args = [proj, proj, proj, q_gain.reshape(1, hd), k_gain.reshape(1, hd)]
    if rope:
        cos_t, sin_t = rope_tabs
        in_specs += [
            pl.BlockSpec((tq, hd), lambda bi, kh, qi: (qi, 0)),
            pl.BlockSpec((tq, hd), lambda bi, kh, qi: (qi, 0)),
            pl.BlockSpec((l, hd), lambda bi, kh, qi: (0, 0)),
            pl.BlockSpec((l, hd), lambda bi, kh, qi: (0, 0)),
        ]
        args += [cos_t, sin_t, cos_t, sin_t]
    if has_cache:
        past = cache_k.shape[1]
        c_spec = pl.BlockSpec((1, past, hd), lambda bi, kh, qi: (bi, 0, kh))
        in_specs += [c_spec, c_spec]
        args += [cache_k, cache_v]
    out_specs = [pl.BlockSpec((1, tq, groups * hd), lambda bi, kh, qi: (bi, qi, kh))]
    out_shape = [jax.ShapeDtypeStruct((b, l, heads * hd), BF16)]
    if emit_kv:
        kv_spec = pl.BlockSpec((1, l, hd), lambda bi, kh, qi: (bi, 0, kh))
        out_specs += [kv_spec, kv_spec]
        out_shape += [jax.ShapeDtypeStruct((b, l, kv * hd), F32)] * 2
    return pl.pallas_call(
        functools.partial(_att_kernel, rope=rope, has_cache=has_cache, emit_kv=emit_kv, groups=groups),
        grid=(b, kv, l // tq),
        in_specs=in_specs,
        out_specs=out_specs,
        out_shape=out_shape,
        scratch_shapes=[pltpu.VMEM((l, hd), BF16), pltpu.VMEM((l, 2 * hd), BF16)],
        compiler_params=_params("parallel", "parallel", "arbitrary"),
        name="attention",
    )(*args)


def _rope_tables(l):
    rows = l // GRID_W
    row = jnp.repeat(jnp.arange(rows, dtype=F32), GRID_W)
    col = jnp.tile(jnp.arange(GRID_W, dtype=F32), rows)
    half = ATT_HD // 2
    inv = ROPE_THETA ** (-jnp.arange(0, half, 2, dtype=F32) / half)
    ang = jnp.concatenate([row[:, None] * inv, col[:, None] * inv], axis=-1)
    cos_t = jnp.repeat(jnp.cos(ang), 2, axis=-1)
    sin_h = jnp.sin(ang)
    sin_t = jnp.stack([-sin_h, sin_h], axis=-1).reshape(l, ATT_HD)
    return cos_t, sin_t


def _ml_kernel(q_ref, k_ref, v_ref, og_ref, gt_ref, gn_ref, *rest, n_chunks, has_state, emit_state):
    rest = list(rest)
    c0f_ref = n0f_ref = m0f_ref = c0b_ref = n0b_ref = m0b_ref = None
    if has_state:
        c0f_ref, n0f_ref, m0f_ref, c0b_ref, n0b_ref, m0b_ref = rest[:6]
        rest = rest[6:]
    o_ref = rest.pop(0)
    outs = None
    if emit_state:
        outs = rest[:6]
        rest = rest[6:]
    cm_f, cm_b, nv_f, nv_b, hf_scr, hb_scr, fl_f, fl_b, br_f, br_b, mo_f, mo_b, mn_f, mn_b = rest

    c = CHUNK
    dqk = q_ref.shape[-1]
    k_scale = dqk ** -0.5
    ri = lax.broadcasted_iota(jnp.int32, (c, c), 0)
    ci = lax.broadcasted_iota(jnp.int32, (c, c), 1)
    mask_f = ci <= ri
    mask_b = ci >= ri
    assert c == LANES and dqk == LANES
    dv_tiles = v_ref.shape[-1] // LANES

    def bf3(m):
        m = jnp.where(m, 1.0, 0.0).astype(BF16)
        return jnp.concatenate([m, m, m], axis=1), jnp.concatenate([m, m, m], axis=0)

    mf3, mf3_t = bf3(mask_f)
    mb3, mb3_t = bf3(mask_b)
    eye3, _ = bf3(ci == ri)

    def split3(x):
        hi = x.astype(BF16)
        r1 = x - hi.astype(F32)
        mid = r1.astype(BF16)
        lo = (r1 - mid.astype(F32)).astype(BF16)
        return jnp.concatenate([hi, mid, lo], axis=1)

    def wide(x):
        return jnp.concatenate([x] * dv_tiles, axis=1)

    if has_state:
        cm_f[...] = c0f_ref[0, 0, 0]
        cm_b[...] = c0b_ref[0, 0, 0]
        nv_f[...] = n0f_ref[0, 0]
        nv_b[...] = n0b_ref[0, 0]
        m0_f = jnp.broadcast_to(m0f_ref[0, 0], (1, LANES))
        m0_b = jnp.broadcast_to(m0b_ref[0, 0], (1, LANES))
    else:
        for r in (cm_f, cm_b, nv_f, nv_b):
            r[...] = jnp.zeros_like(r)
        m0_f = m0_b = jnp.zeros((1, LANES), F32)

    def gate_pass(row_i, row_f, m3_t, m0, fl, br, mo, mn, order):
        f_all = _log_sigmoid(gt_ref[0, 0, row_f])
        fl[...] = f_all
        b_all = _dot(split3(f_all), m3_t)
        br[...] = b_all
        b_end = jnp.sum(f_all, axis=-1, keepdims=True)
        w_max = jnp.max(b_end - b_all + gt_ref[0, 0, row_i], axis=-1, keepdims=True)
        m = m0
        for r in order:
            mo[r:r + 1, :] = m
            m = jnp.maximum(b_end[r:r + 1, :] + m, w_max[r:r + 1, :])
            mn[r:r + 1, :] = m
        return m

    m_last_f = gate_pass(0, 1, mb3_t, m0_f, fl_f, br_f, mo_f, mn_f, range(n_chunks))
    m_last_b = gate_pass(2, 3, mf3_t, m0_b, fl_b, br_b, mo_b, mn_b, reversed(range(n_chunks)))

    def step(ch, cm, nv, out_scr, mask, m3, fl, br, mo, mn, row_i):
        rows = pl.ds(_chunk_offset(ch, c), c)
        one = pl.ds(ch, 1)
        qc = q_ref[0, rows, :]
        kc = k_ref[0, rows, :]
        vc = v_ref[0, rows, :]
        i_row = gt_ref[0, 0, row_i, one, :]
        f_row = fl[one, :]
        b_row = br[one, :]
        m_old = mo[one, :]
        m_new = mn[one, :]
        b_q = _dot_nt(m3, jnp.broadcast_to(split3(f_row), (LANES, 3 * c)))
        i_q = _dot_nt(eye3, jnp.broadcast_to(split3(i_row), (LANES, 3 * c)))
        dlog = jnp.where(mask, b_q - b_row + i_row, -jnp.inf)
        inter = b_q + m_old
        m_q = jnp.maximum(inter, jnp.max(dlog, axis=-1, keepdims=True))
        w_intra = jnp.exp(dlog - m_q)
        w_x = jnp.exp(inter - m_q)
        s = _dot_nt(qc, kc) * (w_intra * k_scale)
        c_old = cm[...]
        n_old = nv[...]
        num = _dot(s.astype(BF16), vc) + _dot(qc, c_old.astype(BF16)) * wide(w_x)
        qn = _dot_nt(qc, jnp.broadcast_to(n_old, (LANES, dqk)).astype(BF16))
        den = jnp.sum(s, axis=-1, keepdims=True) + qn * w_x
        den = jnp.maximum(jnp.abs(den), jnp.exp(-m_q))
        out_scr[rows, :] = num * wide(1.0 / den)
        b_end = jnp.sum(f_row, axis=-1, keepdims=True)
        wlog = b_end - b_q + i_q
        carry_dec = jnp.exp(b_end + m_old - m_new)
        kw = kc.astype(F32) * (jnp.exp(wlog - m_new) * k_scale)
        cm[...] = c_old * wide(carry_dec) + _dot_tn(kw.astype(BF16), vc)
        nv[...] = n_old * carry_dec + jnp.sum(kw, axis=0, keepdims=True)

    def fin(ch):
        rows = pl.ds(_chunk_offset(ch, c), c)
        hn = _rms(hf_scr[rows, :] + hb_scr[rows, :], gn_ref[...])
        o_ref[0, rows, :] = (jax.nn.sigmoid(og_ref[0, rows, :].astype(F32)) * hn).astype(o_ref.dtype)

    def body(i, carry):
        step(i, cm_f, nv_f, hf_scr, mask_f, mf3, fl_f, br_f, mo_f, mn_f, 0)
        step(n_chunks - 1 - i, cm_b, nv_b, hb_scr, mask_b, mb3, fl_b, br_b, mo_b, mn_b, 2)
        return carry

    _two_phase_loop(n_chunks, body, fin)

    if emit_state:
        for dst, src in zip(outs[0::3] + outs[1::3], (cm_f, cm_b, nv_f, nv_b)):
            dst[0, 0] = src[...]
        for dst, m_last in zip(outs[2::3], (m_last_f, m_last_b)):
            dst[0, 0] = m_last[:, 0:1]


def mlstm_core(proj, gates_t, gn, state, b, l, emit_state):
    hh = ML_HEADS
    width = proj.shape[-1]
    dqk = width // (6 * hh)
    dv = 2 * dqk
    has_state = state is not None
    n_chunks = l // CHUNK
    chunk_rows = gates_t.shape[3]
    in_specs = [
        pl.BlockSpec((1, l, dqk), lambda bi, h: (bi, 0, h)),
        pl.BlockSpec((1, l, dqk), lambda bi, h: (bi, 0, hh + h)),
        pl.BlockSpec((1, l, dv), lambda bi, h: (bi, 0, hh + h)),
        pl.BlockSpec((1, l, dv), lambda bi, h: (bi, 0, 2 * hh + h)),
        pl.BlockSpec((1, 1, 4, chunk_rows, CHUNK), lambda bi, h: (bi, h, 0, 0, 0)),
        pl.BlockSpec((1, dv), lambda bi, h: (0, h)),
    ]
    args = [proj, proj, proj, proj, gates_t, gn.reshape(1, hh * dv)]
    c_spec_in = pl.BlockSpec((1, 1, 1, dqk, dv), lambda bi, h: (bi, 0, h, 0, 0))
    n_spec = pl.BlockSpec((1, 1, 1, dqk), lambda bi, h: (bi, h, 0, 0))
    m_spec = pl.BlockSpec((1, 1, 1, 1), lambda bi, h: (bi, h, 0, 0))
    if has_state:
        cf, nf, mf, cb, nb, mb = state
        in_specs += [c_spec_in, n_spec, m_spec, c_spec_in, n_spec, m_spec]
        args += [cf, nf.reshape(b, hh, 1, dqk), mf.reshape(b, hh, 1, 1),
                 cb, nb.reshape(b, hh, 1, dqk), mb.reshape(b, hh, 1, 1)]
    out_specs = [pl.BlockSpec((1, l, dv), lambda bi, h: (bi, 0, h))]
    out_shape = [jax.ShapeDtypeStruct((b, l, hh * dv), BF16)]
    if emit_state:
        c_spec_out = pl.BlockSpec((1, 1, dqk, dv), lambda bi, h: (bi, h, 0, 0))
        out_specs += [c_spec_out, n_spec, m_spec] * 2
        out_shape += [jax.ShapeDtypeStruct((b, hh, dqk, dv), F32),
                      jax.ShapeDtypeStruct((b, hh, 1, dqk), F32),
                      jax.ShapeDtypeStruct((b, hh, 1, 1), F32)] * 2
    return pl.pallas_call(
        functools.partial(_ml_kernel, n_chunks=n_chunks, has_state=has_state, emit_state=emit_state),
        grid=(b, hh),
        in_specs=in_specs,
        out_specs=out_specs,
        out_shape=out_shape,
        scratch_shapes=[pltpu.VMEM((dqk, dv), F32), pltpu.VMEM((dqk, dv), F32),
                        pltpu.VMEM((1, dqk), F32), pltpu.VMEM((1, dqk), F32),
                        pltpu.VMEM((l, dv), F32), pltpu.VMEM((l, dv), F32)]
                       + [pltpu.VMEM((chunk_rows, CHUNK), F32)] * 4 + [pltpu.VMEM((chunk_rows, LANES), F32)] * 4,
        compiler_params=_params("parallel", "parallel"),
        name="mlstm",
    )(*args)


def _hy_filter_kernel(feat_ref, w1_ref, b1_ref, w2_ref, b2_ref, w3_ref, fr_ref, win_ref, sum_ref, dif_ref):
    d = win_ref.shape[-1]
    z = jnp.sin(fr_ref[0:1, :] * (_dot(feat_ref[...].astype(BF16), w1_ref[...].astype(BF16)) + b1_ref[...]))
    z = jnp.sin(fr_ref[1:2, :] * (_dot(z.astype(BF16), w2_ref[...].astype(BF16)) + b2_ref[...]))
    filt = _dot(z.astype(BF16), w3_ref[...].astype(BF16))
    win = win_ref[...]
    ff = filt[:, :d] * win
    fb = filt[:, d:] * win
    sum_ref[...] = (ff + fb).astype(sum_ref.dtype)
    dif_ref[...] = (ff - fb).astype(dif_ref.dtype)


def hyena_filters(l, d, w1, b1, w2, b2, w3, freq):
    t = jnp.linspace(0.0, 1.0, l, dtype=F32)[:, None]
    pos = jnp.arange(l, dtype=F32)[:, None]
    bands = jnp.linspace(1e-4, HY_BANDS - 1, HY_BANDS, dtype=F32)
    ang = 2.0 * math.pi * pos * bands / l
    feats = jnp.concatenate([t, jnp.cos(ang), -jnp.sin(ang)], axis=-1)
    emb = feats.shape[1]
    feats = jnp.pad(feats, ((0, 0), (0, LANES - emb)))
    w1p = jnp.pad(w1, ((0, LANES - emb), (0, 0)))
    deltas = jnp.abs(jnp.linspace(math.log(HY_TARGET) / HY_SLOW_DECAY,
                                  math.log(HY_TARGET) / HY_FAST_DECAY, d, dtype=F32))
    window = jnp.exp(-t * deltas)
    hid = w2.shape[0]
    tl = min(l, 512)
    full = lambda shp: pl.BlockSpec(shp, lambda i: (0,) * len(shp))
    return pl.pallas_call(
        _hy_filter_kernel,
        grid=(l // tl,),
        in_specs=[
            pl.BlockSpec((tl, LANES), lambda i: (i, 0)),
            full((LANES, hid)), full((1, hid)), full((hid, hid)), full((1, hid)), full((hid, 2 * d)),
            full((2, hid)),
            pl.BlockSpec((tl, d), lambda i: (i, 0)),
        ],
        out_specs=[pl.BlockSpec((tl, d), lambda i: (i, 0))] * 2,
        out_shape=[jax.ShapeDtypeStruct((l, d), BF16)] * 2,
        compiler_params=_params("parallel"),
        name="hyena_filter",
    )(feats, w1p, b1.reshape(1, hid), w2, b2.reshape(1, hid), w3, freq, window)


def _dft_tables(l):
    n = 2 * l
    blk = 64
    assert l % blk == 0

    def trig(step, count):
        k = lax.broadcasted_iota(jnp.int32, (l, count), 0)
        t = lax.broadcasted_iota(jnp.int32, (l, count), 1)
        ang = (((2 * k + 1) * step * t) % (2 * n)).astype(F32) * (math.pi / n)
        return jnp.cos(ang), jnp.sin(ang)

    c1, s1 = trig(blk, l // blk)
    c0, s0 = trig(1, blk)
    cos_kt = (c1[:, :, None] * c0[:, None, :] - s1[:, :, None] * s0[:, None, :]).reshape(l, l).astype(BF16)
    sin_kt = (s1[:, :, None] * c0[:, None, :] + c1[:, :, None] * s0[:, None, :]).reshape(l, l).astype(BF16)
    c1t, s1t, c0t, s0t = c1.T, s1.T, c0.T, s0.T
    cos_tk = (c1t[:, None, :] * c0t[None, :, :] - s1t[:, None, :] * s0t[None, :, :]).reshape(l, l).astype(BF16)
    sin_tk = (s1t[:, None, :] * c0t[None, :, :] + c1t[:, None, :] * s0t[None, :, :]).reshape(l, l).astype(BF16)
    return cos_kt, sin_kt, cos_tk, sin_tk


def _hy_spec_kernel(c_ref, s_ref, fs_ref, fd_ref, bias_ref, gr_ref, gs_ref):
    gr_ref[...] = _dot(c_ref[...], fs_ref[...]) + bias_ref[...]
    gs_ref[...] = _dot(s_ref[...], fd_ref[...])


def hyena_filter_spectrum(cos_kt, sin_kt, f_sum, f_dif, f_bias):
    l, d = f_sum.shape
    tm = min(l, 512)
    tn = min(d, 512)
    return pl.pallas_call(
        _hy_spec_kernel,
        grid=(l // tm, d // tn),
        in_specs=[
            pl.BlockSpec((tm, l), lambda i, j: (i, 0)),
            pl.BlockSpec((tm, l), lambda i, j: (i, 0)),
            pl.BlockSpec((l, tn), lambda i, j: (0, j)),
            pl.BlockSpec((l, tn), lambda i, j: (0, j)),
            pl.BlockSpec((1, tn), lambda i, j: (0, j)),
        ],
        out_specs=[pl.BlockSpec((tm, tn), lambda i, j: (i, j))] * 2,
        out_shape=[jax.ShapeDtypeStruct((l, d), F32)] * 2,
        compiler_params=_params("parallel", "parallel"),
        name="hyena_filter_spectrum",
    )(cos_kt, sin_kt, f_sum, f_dif, f_bias.reshape(1, d))


def _hy_conv_kernel(p0_ref, p1_ref, pv_ref, w0_ref, w1_ref, wv_ref, b0_ref, b1_ref, bv_ref, z_ref, x0_ref):
    l = p0_ref.shape[1]
    t = lax.broadcasted_iota(jnp.int32, (l, 1), 0)

    def conv(p_ref, w_ref, b_ref):
        p = p_ref[0].astype(F32)
        prev = jnp.where(t == 0, 0.0, pltpu.roll(p, 1, axis=0))
        nxt = jnp.where(t == l - 1, 0.0, pltpu.roll(p, l - 1, axis=0))
        return b_ref[...] + prev * w_ref[0:1, :] + p * w_ref[1:2, :] + nxt * w_ref[2:3, :]

    x0_ref[0] = conv(p0_ref, w0_ref, b0_ref).astype(x0_ref.dtype)
    z_ref[0] = (conv(pv_ref, wv_ref, bv_ref) * conv(p1_ref, w1_ref, b1_ref)).astype(z_ref.dtype)


def hyena_short_conv(proj, conv_w, conv_b, b, l, d):
    tc = min(d, 512)
    nc = d // tc
    p_spec = lambda off: pl.BlockSpec((1, l, tc), lambda bi, j: (bi, 0, off * nc + j))
    w_spec = lambda off: pl.BlockSpec((HY_SHORT, tc), lambda bi, j: (0, off * nc + j))
    b_spec = lambda off: pl.BlockSpec((1, tc), lambda bi, j: (0, off * nc + j))
    cb = conv_b.reshape(1, 3 * d)
    return pl.pallas_call(
        _hy_conv_kernel,
        grid=(b, nc),
        in_specs=[p_spec(0), p_spec(1), p_spec(2), w_spec(0), w_spec(1), w_spec(2),
                  b_spec(0), b_spec(1), b_spec(2)],
        out_specs=[pl.BlockSpec((1, l, tc), lambda bi, j: (bi, 0, j))] * 2,
        out_shape=[jax.ShapeDtypeStruct((b, l, d), BF16), jax.ShapeDtypeStruct((b, l, d), BF16)],
        compiler_params=_params("parallel", "parallel"),
        name="hyena_short_conv",
    )(proj, proj, proj, conv_w, conv_w, conv_w, cb, cb, cb)


def _hy_fwd_kernel(c_ref, s_ref, z_ref, gr_ref, gs_ref, yr_ref, ys_ref):
    z = z_ref[0]
    zr = _dot(c_ref[...], z)
    zs = _dot(s_ref[...], z)
    gr = gr_ref[...]
    gs = gs_ref[...]
    yr_ref[0] = (zr * gr - zs * gs).astype(yr_ref.dtype)
    ys_ref[0] = (zr * gs + zs * gr).astype(ys_ref.dtype)


def hyena_forward_dft(cos_kt, sin_kt, z, g_r, g_s):
    b, l, d = z.shape
    tm = min(l, 512)
    tn = min(d, 512)
    return pl.pallas_call(
        _hy_fwd_kernel,
        grid=(l // tm, b, d // tn),
        in_specs=[
            pl.BlockSpec((tm, l), lambda i, bi, j: (i, 0)),
            pl.BlockSpec((tm, l), lambda i, bi, j: (i, 0)),
            pl.BlockSpec((1, l, tn), lambda i, bi, j: (bi, 0, j)),
            pl.BlockSpec((tm, tn), lambda i, bi, j: (i, j)),
            pl.BlockSpec((tm, tn), lambda i, bi, j: (i, j)),
        ],
        out_specs=[pl.BlockSpec((1, tm, tn), lambda i, bi, j: (bi, i, j))] * 2,
        out_shape=[jax.ShapeDtypeStruct((b, l, d), BF16)] * 2,
        compiler_params=_params("parallel", "parallel", "parallel"),
        name="hyena_forward_dft",
    )(cos_kt, sin_kt, z, g_r, g_s)


def _hy_inv_kernel(ct_ref, st_ref, yr_ref, ys_ref, x0_ref, o_ref, *, inv_scale):
    y = _dot(ct_ref[...], yr_ref[0]) + _dot(st_ref[...], ys_ref[0])
    o_ref[0] = (y * inv_scale * x0_ref[0]).astype(o_ref.dtype)


def hyena_inverse_dft(cos_tk, sin_tk, y_r, y_s, x0):
    b, l, d = y_r.shape
    tm = min(l, 512)
    tn = min(d, 512)
    return pl.pallas_call(
        functools.partial(_hy_inv_kernel, inv_scale=1.0 / l),
        grid=(l // tm, b, d // tn),
        in_specs=[
            pl.BlockSpec((tm, l), lambda i, bi, j: (i, 0)),
            pl.BlockSpec((tm, l), lambda i, bi, j: (i, 0)),
            pl.BlockSpec((1, l, tn), lambda i, bi, j: (bi, 0, j)),
            pl.BlockSpec((1, l, tn), lambda i, bi, j: (bi, 0, j)),
            pl.BlockSpec((1, tm, tn), lambda i, bi, j: (bi, i, j)),
        ],
        out_specs=pl.BlockSpec((1, tm, tn), lambda i, bi, j: (bi, i, j)),
        out_shape=jax.ShapeDtypeStruct((b, l, d), BF16),
        compiler_params=_params("parallel", "parallel", "parallel"),
        name="hyena_inverse_dft",
    )(cos_tk, sin_tk, y_r, y_s, x0)


def _tile_rows(grp):
    span = grp.l if grp.per_batch else grp.t
    return next(tm for tm in (1024, 512, 256, 128) if span % tm == 0)


def kernel(x_prompt, x_sample, cache_k, cache_v, state_ret_fwd, state_ret_bwd, state_ml_C_fwd, state_ml_n_fwd, state_ml_m_fwd, state_ml_C_bwd, state_ml_n_bwd, state_ml_m_bwd, c, c_ctx, mod_w, mod_b, norm_mix_pre, norm_mix_post, norm_ffn_pre, norm_ffn_post, mlp_w1, mlp_w2, ret_w_in, ret_decay_fwd, ret_decay_bwd, ret_gn, ret_w_out, att_w_in, att_q_gain, att_k_gain, att_w_out, ml_w_in, ml_gate_b, ml_gn, ml_w_out, hy_w_in, hy_b_in, hy_conv_w, hy_conv_b, hy_f_w1, hy_f_b1, hy_f_w2, hy_f_b2, hy_f_w3, hy_sin_freq, hy_f_bias, hy_w_out):
    bp, lp, d = x_prompt.shape
    bs, ls, _ = x_sample.shape
    depth = mod_w.shape[0]
    n_mixers = 4
    mod_rows = 16
    assert 1 + bs <= mod_rows

    grp_p = Group(bp, lp, 0, False)
    grp_s = Group(bs, ls, 1, True)
    groups = (grp_p, grp_s)

    cond = jnp.concatenate([c_ctx[None, :], c, jnp.zeros((mod_rows - 1 - bs, d), F32)], axis=0)
    mod_all = adaln_all(cond, mod_w, mod_b)

    xs = [x_prompt.reshape(grp_p.t, d), x_sample.reshape(grp_s.t, d)]
    new_k = new_v = new_rf = new_rb = None
    new_ml = None

    for i in range(depth):
        mixer = i % n_mixers
        j = i // n_mixers
        mod3 = mod_all[i].reshape(mod_rows, 1, 6 * d)
        ys = []
        for gi, grp in enumerate(groups):
            x = xs[gi]
            tm = _tile_rows(grp)
            is_prompt = gi == 0
            if mixer == 0:
                w_in = ret_w_in[j].astype(BF16)
                proj = norm_matmul(x, norm_mix_pre[i], mod3, 0, 1, w_in, None, BF16, grp, tm, PROJ_TILE_N)
                dec = jnp.stack([ret_decay_fwd[j], ret_decay_bwd[j]]).astype(F32)
                s0f = None if is_prompt else state_ret_fwd
                s0b = None if is_prompt else state_ret_bwd
                assert is_prompt or state_ret_fwd.shape[1] == 1
                res = retention_core(proj.reshape(grp.b, grp.l, -1), dec, ret_gn[j], s0f, s0b,
                                     grp.b, grp.l, emit_state=is_prompt)
                if is_prompt:
                    new_rf, new_rb = res[1][:, None], res[2][:, None]
                a = res[0].reshape(grp.t, -1)
                w_out = ret_w_out[j].astype(BF16)
            elif mixer == 1:
                w_in = att_w_in[j].astype(BF16)
                proj = norm_matmul(x, norm_mix_pre[i], mod3, 0, 1, w_in, None, F32, grp, tm, PROJ_TILE_N)
                proj = proj.reshape(grp.b, grp.l, -1)
                if is_prompt:
                    res = attention_core(proj, att_q_gain[j], att_k_gain[j], None, None, None,
                                         grp.b, grp.l, min(grp.l, 256), emit_kv=True)
                    new_k = res[1].reshape(grp.b, 1, grp.l, ATT_KV, ATT_HD)
                    new_v = res[2].reshape(grp.b, 1, grp.l, ATT_KV, ATT_HD)
                else:
                    assert cache_k.shape[1] == 1
                    ck = cache_k.reshape(grp.b, cache_k.shape[2], ATT_KV * ATT_HD)
                    cv = cache_v.reshape(grp.b, cache_v.shape[2], ATT_KV * ATT_HD)
                    res = attention_core(proj, att_q_gain[j], att_k_gain[j], _rope_tables(grp.l), ck, cv,
                                         grp.b, grp.l, min(grp.l, 256), emit_kv=False)
                a = res[0].reshape(grp.t, -1)
                w_out = att_w_out[j].astype(BF16)
            elif mixer == 2:
                hh = ML_HEADS
                n_main = ml_w_in.shape[2] - 4 * hh
                w_main = ml_w_in[j][:, :n_main].astype(BF16)
                w_gate = jnp.pad(ml_w_in[j][:, n_main:], ((0, 0), (0, LANES - 4 * hh))).astype(BF16)
                b_gate = jnp.pad(ml_gate_b[j], (0, LANES - 4 * hh))
                proj = norm_matmul(x, norm_mix_pre[i], mod3, 0, 1, w_main, None, BF16, grp, tm, PROJ_TILE_N)
                gates = norm_matmul(x, norm_mix_pre[i], mod3, 0, 1, w_gate, b_gate, F32, grp, tm, LANES)
                gates = gates[:, :4 * hh].reshape(grp.b, grp.l, 4, hh)
                n_chunks = grp.l // CHUNK
                gates_t = gates.transpose(0, 3, 2, 1).reshape(grp.b, hh, 4, n_chunks, CHUNK)
                gates_t = jnp.pad(gates_t, ((0, 0), (0, 0), (0, 0), (0, -n_chunks % 16), (0, 0)))
                state = None
                if not is_prompt:
                    assert state_ml_C_fwd.shape[1] == 1
                    state = (state_ml_C_fwd, state_ml_n_fwd, state_ml_m_fwd,
                             state_ml_C_bwd, state_ml_n_bwd, state_ml_m_bwd)
                res = mlstm_core(proj.reshape(grp.b, grp.l, -1), gates_t, ml_gn[j], state,
                                 grp.b, grp.l, emit_state=is_prompt)
                if is_prompt:
                    dqk = res[2].shape[-1]
                    new_ml = (res[1][:, None], res[2].reshape(grp.b, 1, hh, dqk), res[3].reshape(grp.b, 1, hh),
                              res[4][:, None], res[5].reshape(grp.b, 1, hh, dqk), res[6].reshape(grp.b, 1, hh))
                a = res[0].reshape(grp.t, -1)
                w_out = ml_w_out[j].astype(BF16)
            else:
                w_in = hy_w_in[j].astype(BF16)
                proj = norm_matmul(x, norm_mix_pre[i], mod3, 0, 1, w_in, hy_b_in[j], BF16, grp, tm, PROJ_TILE_N)
                z, x0 = hyena_short_conv(proj.reshape(grp.b, grp.l, 3 * d), hy_conv_w[j], hy_conv_b[j],
                                         grp.b, grp.l, d)
                f_sum, f_dif = hyena_filters(grp.l, d, hy_f_w1[j], hy_f_b1[j], hy_f_w2[j], hy_f_b2[j],
                                             hy_f_w3[j], hy_sin_freq[j])
                cos_kt, sin_kt, cos_tk, sin_tk = _dft_tables(grp.l)
                g_r, g_s = hyena_filter_spectrum(cos_kt, sin_kt, f_sum, f_dif, hy_f_bias[j])
                y_r, y_s = hyena_forward_dft(cos_kt, sin_kt, z, g_r, g_s)
                a = hyena_inverse_dft(cos_tk, sin_tk, y_r, y_s, x0).reshape(grp.t, d)
                w_out = hy_w_out[j].astype(BF16)
            x, h_mlp = matmul_resnorm(a, w_out, x, mod3, norm_mix_post[i], norm_ffn_pre[i], grp, tm)
            x = mlp_block(h_mlp, x, norm_ffn_post[i], mod3,
                          mlp_w1[i].astype(BF16), mlp_w2[i].astype(BF16), grp, tm, MLP_TILE_F)
            ys.append(x)
        xs = ys

    y_prompt = xs[0].reshape(bp, lp, d)
    y_sample = xs[1].reshape(bs, ls, d)
    return (y_prompt, y_sample, new_k, new_v, new_rf, new_rb) + tuple(new_ml)
```

```python
import functools
import math

import jax
import jax.numpy as jnp
import numpy as np
from jax import lax
from jax.experimental import pallas as pl
from jax.experimental.pallas import tpu as pltpu

F32 = jnp.float32
BF16 = jnp.bfloat16

EPS = 1e-6
CHUNK = 128
RET_CHUNK = 256
GRID_W = 64
ROPE_THETA = 10000.0

RET_HEADS = 4
ATT_HEADS = 8
ATT_KV = 2
ATT_HD = 128
ML_HEADS = 4

HY_BANDS = 16
HY_SHORT = 3
HY_FAST_DECAY = 0.3
HY_SLOW_DECAY = 1.5
HY_TARGET = 1e-2

VMEM_LIMIT_BYTES = 56 * 1024 * 1024
LANES = 128
PROJ_TILE_N = 1536
MLP_TILE_F = 1024


def _params(*sem):
    return pltpu.CompilerParams(dimension_semantics=sem, vmem_limit_bytes=VMEM_LIMIT_BYTES)


def _dot(a, b):
    return jnp.dot(a, b, preferred_element_type=F32)


def _dot_nt(a, b):
    return lax.dot_general(a, b, (((1,), (1,)), ((), ())), preferred_element_type=F32)


def _dot_tn(a, b):
    return lax.dot_general(a, b, (((0,), (0,)), ((), ())), preferred_element_type=F32)


def _rms(x, g):
    return x * lax.rsqrt(jnp.mean(x * x, axis=-1, keepdims=True) + EPS) * g


def _log_sigmoid(x):
    return jnp.minimum(x, 0.0) - jnp.log1p(jnp.exp(-jnp.abs(x)))


def _chunk_offset(ch, c):
    return ch * c if isinstance(ch, int) else pl.multiple_of(ch * c, c)


def _two_phase_loop(n_chunks, body, fin, max_unroll=2):
    if n_chunks == 1:
        body(0, 0)
        fin(0)
        return
    assert n_chunks % 2 == 0
    half = n_chunks // 2
    unroll = math.gcd(half, max_unroll)

    def body_fin(i, carry):
        body(i, carry)
        fin(i)
        fin(n_chunks - 1 - i)
        return carry

    lax.fori_loop(0, half, body, 0, unroll=unroll)
    lax.fori_loop(half, n_chunks, body_fin, 0, unroll=unroll)


class Group:
    def __init__(self, b, l, row0, per_batch):
        self.b, self.l, self.row0, self.per_batch = b, l, row0, per_batch
        self.t = b * l

    def mod_spec(self, chunk, tm, d):
        row0, per_batch, l = self.row0, self.per_batch, self.l
        if per_batch:
            return pl.BlockSpec((1, 1, d), lambda i, *_: (row0 + (i * tm) // l, 0, chunk))
        return pl.BlockSpec((1, 1, d), lambda i, *_: (row0, 0, chunk))


def _adaln_kernel(c_ref, w_ref, b_ref, o_ref):
    s = jax.nn.silu(c_ref[...])
    o_ref[0] = _dot(s.astype(BF16), w_ref[0].astype(BF16)) + b_ref[0]


def adaln_all(cond, mod_w, mod_b):
    depth, d, n = mod_w.shape
    rows = cond.shape[0]
    tn = 768
    return pl.pallas_call(
        _adaln_kernel,
        grid=(depth, n // tn),
        in_specs=[
            pl.BlockSpec((rows, d), lambda l, j: (0, 0)),
            pl.BlockSpec((1, d, tn), lambda l, j: (l, 0, j)),
            pl.BlockSpec((1, 1, tn), lambda l, j: (l, 0, j)),
        ],
        out_specs=pl.BlockSpec((1, rows, tn), lambda l, j: (l, 0, j)),
        out_shape=jax.ShapeDtypeStruct((depth, rows, n), F32),
        compiler_params=_params("parallel", "parallel"),
        name="adaln",
    )(cond, mod_w, mod_b.reshape(depth, 1, n))


def _norm_mm_kernel(x_ref, g_ref, sh_ref, sc_ref, w_ref, *rest, has_bias):
    if has_bias:
        b_ref, o_ref, h_scr = rest
    else:
        o_ref, h_scr = rest

    @pl.when(pl.program_id(1) == 0)
    def _():
        y = _rms(x_ref[...], g_ref[...])
        h_scr[...] = (y * (1.0 + sc_ref[0]) + sh_ref[0]).astype(BF16)

    acc = _dot(h_scr[...], w_ref[...])
    if has_bias:
        acc = acc + b_ref[...]
    o_ref[...] = acc.astype(o_ref.dtype)


def norm_matmul(x, gain, mod3, sh_idx, sc_idx, w, bias, out_dtype, grp, tm, tn):
    t, d = x.shape
    n = w.shape[1]
    in_specs = [
        pl.BlockSpec((tm, d), lambda i, j: (i, 0)),
        pl.BlockSpec((1, d), lambda i, j: (0, 0)),
        grp.mod_spec(sh_idx, tm, d),
        grp.mod_spec(sc_idx, tm, d),
        pl.BlockSpec((d, tn), lambda i, j: (0, j)),
    ]
    args = [x, gain.reshape(1, d), mod3, mod3, w]
    if bias is not None:
        in_specs.append(pl.BlockSpec((1, tn), lambda i, j: (0, j)))
        args.append(bias.reshape(1, n))
    return pl.pallas_call(
        functools.partial(_norm_mm_kernel, has_bias=bias is not None),
        grid=(t // tm, n // tn),
        in_specs=in_specs,
        out_specs=pl.BlockSpec((tm, tn), lambda i, j: (i, j)),
        out_shape=jax.ShapeDtypeStruct((t, n), out_dtype),
        scratch_shapes=[pltpu.VMEM((tm, d), BF16)],
        compiler_params=_params("parallel", "arbitrary"),
        name="norm_matmul",
    )(*args)


def _mm_res_kernel(a_ref, w_ref, x_ref, gate_ref, pg_ref, ng_ref, sh_ref, sc_ref, o_ref, h_ref):
    y = _dot(a_ref[...], w_ref[...])
    x1 = x_ref[...] + gate_ref[0] * _rms(y, pg_ref[...])
    o_ref[...] = x1
    h_ref[...] = (_rms(x1, ng_ref[...]) * (1.0 + sc_ref[0]) + sh_ref[0]).astype(h_ref.dtype)


def matmul_resnorm(a, w, x, mod3, post_gain, next_gain, grp, tm):
    t, k = a.shape
    d = w.shape[1]
    row = pl.BlockSpec((tm, d), lambda i: (i, 0))
    vec = pl.BlockSpec((1, d), lambda i: (0, 0))
    return pl.pallas_call(
        _mm_res_kernel,
        grid=(t // tm,),
        in_specs=[
            pl.BlockSpec((tm, k), lambda i: (i, 0)),
            pl.BlockSpec((k, d), lambda i: (0, 0)),
            row,
            grp.mod_spec(2, tm, d),
            vec,
            vec,
            grp.mod_spec(3, tm, d),
            grp.mod_spec(4, tm, d),
        ],
        out_specs=[row, row],
        out_shape=[jax.ShapeDtypeStruct((t, d), F32), jax.ShapeDtypeStruct((t, d), BF16)],
        compiler_params=_params("parallel"),
        name="out_proj",
    )(a, w, x, mod3, post_gain.reshape(1, d), next_gain.reshape(1, d), mod3, mod3)


def _mlp_kernel(h_ref, x_ref, w1_ref, w2_ref, gate_ref, pg_ref, o_ref, acc_scr):
    j = pl.program_id(1)

    @pl.when(j == 0)
    def _():
        acc_scr[...] = jnp.zeros_like(acc_scr)

    u = _dot(h_ref[...], w1_ref[...].astype(BF16))
    u = jnp.square(jnp.maximum(u, 0.0)).astype(BF16)
    acc_scr[...] += _dot(u, w2_ref[...].astype(BF16))

    @pl.when(j == pl.num_programs(1) - 1)
    def _():
        o_ref[...] = x_ref[...] + gate_ref[0] * _rms(acc_scr[...], pg_ref[...])


def mlp_block(h, x, post_gain, mod3, w1, w2, layer, grp, tm, tf):
    t, d = x.shape
    f = w1.shape[2]
    return pl.pallas_call(
        _mlp_kernel,
        grid=(t // tm, f // tf),
        in_specs=[
            pl.BlockSpec((tm, d), lambda i, j: (i, 0)),
            pl.BlockSpec((tm, d), lambda i, j: (i, 0)),
            pl.BlockSpec((None, d, tf), lambda i, j: (layer, 0, j)),
            pl.BlockSpec((None, tf, d), lambda i, j: (layer, j, 0)),
            grp.mod_spec(5, tm, d),
            pl.BlockSpec((1, d), lambda i, j: (0, 0)),
        ],
        out_specs=pl.BlockSpec((tm, d), lambda i, j: (i, 0)),
        out_shape=jax.ShapeDtypeStruct((t, d), F32),
        scratch_shapes=[pltpu.VMEM((tm, d), F32)],
        compiler_params=_params("parallel", "arbitrary"),
        name="mlp",
    )(h, x, w1, w2, mod3, post_gain.reshape(1, d))


def _ret_kernel(dec_ref, q_ref, k_ref, v_ref, g_ref, gn_ref, *rest, n_chunks, has_state, emit_state):
    rest = list(rest)
    s0f_ref = s0b_ref = sf_ref = sb_ref = None
    if has_state:
        s0f_ref, s0b_ref = rest[:2]
        rest = rest[2:]
    o_ref = rest.pop(0)
    if emit_state:
        sf_ref, sb_ref = rest[:2]
        rest = rest[2:]
    st_f, st_b, of_scr, ob_scr = rest

    c = RET_CHUNK
    dk = q_ref.shape[-1]
    h = pl.program_id(1)
    lg_f = _log_sigmoid(jnp.full((1, 1), dec_ref[0, h], F32))
    lg_b = _log_sigmoid(jnp.full((1, 1), dec_ref[1, h], F32))
    ri = lax.broadcasted_iota(jnp.int32, (c, c), 0)
    ci = lax.broadcasted_iota(jnp.int32, (c, c), 1)
    rel = (ri - ci).astype(F32)
    intra_f = jnp.where(rel >= 0, jnp.exp(lg_f * jnp.maximum(rel, 0.0)), 0.0)
    intra_b = jnp.where(rel <= 0, jnp.exp(lg_b * jnp.maximum(-rel, 0.0)), 0.0)
    idx = lax.broadcasted_iota(jnp.int32, (c, 1), 0).astype(F32)
    qdec_f = jnp.exp(lg_f * (idx + 1.0))
    kdec_f = jnp.exp(lg_f * (c - 1.0 - idx))
    qdec_b = jnp.exp(lg_b * (c - idx))
    kdec_b = jnp.exp(lg_b * idx)
    cdec_f = jnp.exp(lg_f * c)
    cdec_b = jnp.exp(lg_b * c)
    q_scale = dk ** -0.5

    if has_state:
        st_f[...] = s0f_ref[0, 0, 0]
        st_b[...] = s0b_ref[0, 0, 0]
    else:
        st_f[...] = jnp.zeros_like(st_f)
        st_b[...] = jnp.zeros_like(st_b)

    def step(ch, st, out_scr, intra, qdec, kdec, cdec):
        rows = pl.ds(_chunk_offset(ch, c), c)
        qc = q_ref[0, rows, :]
        kc = k_ref[0, rows, :]
        vc = v_ref[0, rows, :]
        s_old = st[...]
        sc = _dot_nt(qc, kc) * (intra * q_scale)
        o = _dot(sc.astype(BF16), vc) + _dot(qc, s_old.astype(BF16)) * (qdec * q_scale)
        kw = (kc.astype(F32) * kdec).astype(BF16)
        st[...] = s_old * cdec + _dot_tn(kw, vc)
        out_scr[rows, :] = o

    def fin(ch):
        rows = pl.ds(_chunk_offset(ch, c), c)
        o = _rms(of_scr[rows, :] + ob_scr[rows, :], gn_ref[...])
        o_ref[0, rows, :] = (jax.nn.silu(g_ref[0, rows, :].astype(F32)) * o).astype(o_ref.dtype)

    def body(i, carry):
        step(i, st_f, of_scr, intra_f, qdec_f, kdec_f, cdec_f)
        step(n_chunks - 1 - i, st_b, ob_scr, intra_b, qdec_b, kdec_b, cdec_b)
        return carry

    _two_phase_loop(n_chunks, body, fin)

    if emit_state:
        sf_ref[0, 0, 0] = st_f[...]
        sb_ref[0, 0, 0] = st_b[...]


def retention_core(proj, dec, gn, s0f, s0b, b, l, emit_state):
    hh = RET_HEADS
    width = proj.shape[-1]
    dk = width // (6 * hh)
    dv = 2 * dk
    has_state = s0f is not None
    n_chunks = l // RET_CHUNK
    in_specs = [
        pl.BlockSpec(memory_space=pltpu.SMEM),
        pl.BlockSpec((1, l, dk), lambda bi, h: (bi, 0, h)),
        pl.BlockSpec((1, l, dk), lambda bi, h: (bi, 0, hh + h)),
        pl.BlockSpec((1, l, dv), lambda bi, h: (bi, 0, hh + h)),
        pl.BlockSpec((1, l, dv), lambda bi, h: (bi, 0, 2 * hh + h)),
        pl.BlockSpec((1, dv), lambda bi, h: (0, h)),
    ]
    args = [dec, proj, proj, proj, proj, gn.reshape(1, hh * dv)]
    if has_state:
        st_spec = pl.BlockSpec((1, 1, 1, dk, dv), lambda bi, h: (bi, 0, h, 0, 0))
        in_specs += [st_spec, st_spec]
        args += [s0f, s0b]
    out_specs = [pl.BlockSpec((1, l, dv), lambda bi, h: (bi, 0, h))]
    out_shape = [jax.ShapeDtypeStruct((b, l, hh * dv), BF16)]
    if emit_state:
        so_spec = pl.BlockSpec((1, 1, 1, dk, dv), lambda bi, h: (bi, 0, h, 0, 0))
        out_specs += [so_spec, so_spec]
        out_shape += [jax.ShapeDtypeStruct((b, 1, hh, dk, dv), F32)] * 2
    return pl.pallas_call(
        functools.partial(_ret_kernel, n_chunks=n_chunks, has_state=has_state, emit_state=emit_state),
        grid=(b, hh),
        in_specs=in_specs,
        out_specs=out_specs,
        out_shape=out_shape,
        scratch_shapes=[pltpu.VMEM((dk, dv), F32), pltpu.VMEM((dk, dv), F32),
                        pltpu.VMEM((l, dv), F32), pltpu.VMEM((l, dv), F32)],
        compiler_params=_params("parallel", "parallel"),
        name="retention",
    )(*args)


def _rope_rot(x, cos_t, sin_t):
    lane = lax.broadcasted_iota(jnp.int32, x.shape, x.ndim - 1)
    nxt = pltpu.roll(x, LANES - 1, axis=x.ndim - 1)
    prv = pltpu.roll(x, 1, axis=x.ndim - 1)
    swapped = jnp.where(jnp.bitwise_and(lane, 1) == 0, nxt, prv)
    return x * cos_t + swapped * sin_t


def _att_kernel(q_ref, k_ref, v_ref, qg_ref, kg_ref, *rest, rope, has_cache, emit_kv, groups):
    rest = list(rest)
    cosq_ref = sinq_ref = cosk_ref = sink_ref = ck_ref = cv_ref = kn_ref = vo_ref = None
    if rope:
        cosq_ref, sinq_ref, cosk_ref, sink_ref = rest[:4]
        rest = rest[4:]
    if has_cache:
        ck_ref, cv_ref = rest[:2]
        rest = rest[2:]
    o_ref = rest.pop(0)
    if emit_kv:
        kn_ref, vo_ref = rest[:2]
        rest = rest[2:]
    k_scr, v_scr = rest
    hd = ATT_HD
    exp2_scale = hd ** -0.5 * math.log2(math.e)

    @pl.when(pl.program_id(2) == 0)
    def _():
        kn = _rms(k_ref[0], kg_ref[...])
        if emit_kv:
            kn_ref[0] = kn
            vo_ref[0] = v_ref[0]
        if rope:
            kn = _rope_rot(kn, cosk_ref[...], sink_ref[...])
        k_scr[...] = kn.astype(BF16)
        v_scr[:, :hd] = v_ref[0].astype(BF16)
        v_scr[:, hd:] = jnp.ones((v_scr.shape[0], hd), BF16)

    if has_cache:
        ck = ck_ref[0].astype(BF16)
        cv = jnp.concatenate([cv_ref[0].astype(BF16), jnp.ones((cv_ref.shape[1], hd), BF16)], axis=1)

    for g in range(groups):
        cols = slice(g * hd, (g + 1) * hd)
        qn = _rms(q_ref[0, :, cols], qg_ref[...])
        if rope:
            qn = _rope_rot(qn, cosq_ref[...], sinq_ref[...])
        qb = qn.astype(BF16)
        s1 = _dot_nt(qb, k_scr[...])
        m = jnp.max(s1, axis=-1, keepdims=True)
        if has_cache:
            s2 = _dot_nt(qb, ck)
            m = jnp.maximum(m, jnp.max(s2, axis=-1, keepdims=True))
        p1 = jnp.exp2((s1 - m) * exp2_scale)
        nd = _dot(p1.astype(BF16), v_scr[...])
        if has_cache:
            p2 = jnp.exp2((s2 - m) * exp2_scale)
            nd = nd + _dot(p2.astype(BF16), cv)
        o_ref[0, :, cols] = (nd[:, :hd] * (1.0 / nd[:, hd:])).astype(o_ref.dtype)


def attention_core(proj, q_gain, k_gain, rope_tabs, cache_k, cache_v, b, l, tq, emit_kv):
    hd, kv, heads = ATT_HD, ATT_KV, ATT_HEADS
    groups = heads // kv
    rope = rope_tabs is not None
    has_cache = cache_k is not None
    in_specs = [
        pl.BlockSpec((1, tq, groups * hd), lambda bi, kh, qi: (bi, qi, kh)),
        pl.BlockSpec((1, l, hd), lambda bi, kh, qi: (bi, 0, heads + kh)),
        pl.BlockSpec((1, l, hd), lambda bi, kh, qi: (bi, 0, heads + kv + kh)),
        pl.BlockSpec((1, hd), lambda bi, kh, qi: (0, 0)),
        pl.BlockSpec((1, hd), lambda bi, kh, qi: (0, 0)),
    ]
    args = [proj, proj, proj, q_gain.reshape(1, hd), k_gain.reshape(1, hd)]
    if rope:
        cos_t, sin_t = rope_tabs
        in_specs += [
            pl.BlockSpec((tq, hd), lambda bi, kh, qi: (qi, 0)),
            pl.BlockSpec((tq, hd), lambda bi, kh, qi: (qi, 0)),
            pl.BlockSpec((l, hd), lambda bi, kh, qi: (0, 0)),
            pl.BlockSpec((l, hd), lambda bi, kh, qi: (0, 0)),
        ]
        args += [cos_t, sin_t, cos_t, sin_t]
    if has_cache:
        past = cache_k.shape[1]
        c_spec = pl.BlockSpec((1, past, hd), lambda bi, kh, qi: (bi, 0, kh))
        in_specs += [c_spec, c_spec]
        args += [cache_k, cache_v]
    out_specs = [pl.BlockSpec((1, tq, groups * hd), lambda bi, kh, qi: (bi, qi, kh))]
    out_shape = [jax.ShapeDtypeStruct((b, l, heads * hd), BF16)]
    if emit_kv:
        kv_spec = pl.BlockSpec((1, l, hd), lambda bi, kh, qi: (bi, 0, kh))
        out_specs += [kv_spec, kv_spec]
        out_shape += [jax.ShapeDtypeStruct((b, l, kv * hd), F32)] * 2
    return pl.pallas_call(
        functools.partial(_att_kernel, rope=rope, has_cache=has_cache, emit_kv=emit_kv, groups=groups),
        grid=(b, kv, l // tq),
        in_specs=in_specs,
        out_specs=out_specs,
        out_shape=out_shape,
        scratch_shapes=[pltpu.VMEM((l, hd), BF16), pltpu.VMEM((l, 2 * hd), BF16)],
        compiler_params=_params("parallel", "parallel", "arbitrary"),
        name="attention",
    )(*args)


def _rope_tables(l):
    rows = l // GRID_W
    row = jnp.repeat(jnp.arange(rows, dtype=F32), GRID_W)
    col = jnp.tile(jnp.arange(GRID_W, dtype=F32), rows)
    half = ATT_HD // 2
    inv = ROPE_THETA ** (-jnp.arange(0, half, 2, dtype=F32) / half)
    ang = jnp.concatenate([row[:, None] * inv, col[:, None] * inv], axis=-1)
    cos_t = jnp.repeat(jnp.cos(ang), 2, axis=-1)
    sin_h = jnp.sin(ang)
    sin_t = jnp.stack([-sin_h, sin_h], axis=-1).reshape(l, ATT_HD)
    return cos_t, sin_t


def _ml_kernel(q_ref, k_ref, v_ref, og_ref, gt_ref, gn_ref, *rest, n_chunks, has_state, emit_state):
    rest = list(rest)
    c0f_ref = n0f_ref = m0f_ref = c0b_ref = n0b_ref = m0b_ref = None
    if has_state:
        c0f_ref, n0f_ref, m0f_ref, c0b_ref, n0b_ref, m0b_ref = rest[:6]
        rest = rest[6:]
    o_ref = rest.pop(0)
    outs = None
    if emit_state:
        outs = rest[:6]
        rest = rest[6:]
    cm_f, cm_b, nv_f, nv_b, hf_scr, hb_scr, fl_f, fl_b, br_f, br_b, mo_f, mo_b, mn_f, mn_b = rest

    c = CHUNK
    dqk = q_ref.shape[-1]
    k_scale = dqk ** -0.5
    ri = lax.broadcasted_iota(jnp.int32, (c, c), 0)
    ci = lax.broadcasted_iota(jnp.int32, (c, c), 1)
    mask_f = ci <= ri
    mask_b = ci >= ri
    assert c == LANES and dqk == LANES
    dv_tiles = v_ref.shape[-1] // LANES

    def bf3(m):
        m = jnp.where(m, 1.0, 0.0).astype(BF16)
        return jnp.concatenate([m, m, m], axis=1), jnp.concatenate([m, m, m], axis=0)

    mf3, mf3_t = bf3(mask_f)
    mb3, mb3_t = bf3(mask_b)
    eye3, _ = bf3(ci == ri)

    def split3(x):
        hi = x.astype(BF16)
        r1 = x - hi.astype(F32)
        mid = r1.astype(BF16)
        lo = (r1 - mid.astype(F32)).astype(BF16)
        return jnp.concatenate([hi, mid, lo], axis=1)

    def wide(x):
        return jnp.concatenate([x] * dv_tiles, axis=1)

    if has_state:
        cm_f[...] = c0f_ref[0, 0, 0]
        cm_b[...] = c0b_ref[0, 0, 0]
        nv_f[...] = n0f_ref[0, 0]
        nv_b[...] = n0b_ref[0, 0]
        m0_f = jnp.broadcast_to(m0f_ref[0, 0], (1, LANES))
        m0_b = jnp.broadcast_to(m0b_ref[0, 0], (1, LANES))
    else:
        for r in (cm_f, cm_b, nv_f, nv_b):
            r[...] = jnp.zeros_like(r)
        m0_f = m0_b = jnp.zeros((1, LANES), F32)

    def gate_pass(row_i, row_f, m3_t, m0, fl, br, mo, mn, order):
        f_all = _log_sigmoid(gt_ref[0, 0, row_f])
        fl[...] = f_all
        b_all = _dot(split3(f_all), m3_t)
        br[...] = b_all
        b_end = jnp.sum(f_all, axis=-1, keepdims=True)
        w_max = jnp.max(b_end - b_all + gt_ref[0, 0, row_i], axis=-1, keepdims=True)
        m = m0
        for r in order:
            mo[r:r + 1, :] = m
            m = jnp.maximum(b_end[r:r + 1, :] + m, w_max[r:r + 1, :])
            mn[r:r + 1, :] = m
        return m

    m_last_f = gate_pass(0, 1, mb3_t, m0_f, fl_f, br_f, mo_f, mn_f, range(n_chunks))
    m_last_b = gate_pass(2, 3, mf3_t, m0_b, fl_b, br_b, mo_b, mn_b, reversed(range(n_chunks)))

    def step(ch, cm, nv, out_scr, mask, m3, fl, br, mo, mn, row_i):
        rows = pl.ds(_chunk_offset(ch, c), c)
        one = pl.ds(ch, 1)
        qc = q_ref[0, rows, :]
        kc = k_ref[0, rows, :]
        vc = v_ref[0, rows, :]
        i_row = gt_ref[0, 0, row_i, one, :]
        f_row = fl[one, :]
        b_row = br[one, :]
        m_old = mo[one, :]
        m_new = mn[one, :]
        b_q = _dot_nt(m3, jnp.broadcast_to(split3(f_row), (LANES, 3 * c)))
        i_q = _dot_nt(eye3, jnp.broadcast_to(split3(i_row), (LANES, 3 * c)))
        dlog = jnp.where(mask, b_q - b_row + i_row, -jnp.inf)
        inter = b_q + m_old
        m_q = jnp.maximum(inter, jnp.max(dlog, axis=-1, keepdims=True))
        w_intra = jnp.exp(dlog - m_q)
        w_x = jnp.exp(inter - m_q)
        s = _dot_nt(qc, kc) * (w_intra * k_scale)
        c_old = cm[...]
        n_old = nv[...]
        num = _dot(s.astype(BF16), vc) + _dot(qc, c_old.astype(BF16)) * wide(w_x)
        qn = _dot_nt(qc, jnp.broadcast_to(n_old, (LANES, dqk)).astype(BF16))
        den = jnp.sum(s, axis=-1, keepdims=True) + qn * w_x
        den = jnp.maximum(jnp.abs(den), jnp.exp(-m_q))
        out_scr[rows, :] = num * wide(1.0 / den)
        b_end = jnp.sum(f_row, axis=-1, keepdims=True)
        wlog = b_end - b_q + i_q
        carry_dec = jnp.exp(b_end + m_old - m_new)
        kw = kc.astype(F32) * (jnp.exp(wlog - m_new) * k_scale)
        cm[...] = c_old * wide(carry_dec) + _dot_tn(kw.astype(BF16), vc)
        nv[...] = n_old * carry_dec + jnp.sum(kw, axis=0, keepdims=True)

    def fin(ch):
        rows = pl.ds(_chunk_offset(ch, c), c)
        hn = _rms(hf_scr[rows, :] + hb_scr[rows, :], gn_ref[...])
        o_ref[0, rows, :] = (jax.nn.sigmoid(og_ref[0, rows, :].astype(F32)) * hn).astype(o_ref.dtype)

    def body(i, carry):
        step(i, cm_f, nv_f, hf_scr, mask_f, mf3, fl_f, br_f, mo_f, mn_f, 0)
        step(n_chunks - 1 - i, cm_b, nv_b, hb_scr, mask_b, mb3, fl_b, br_b, mo_b, mn_b, 2)
        return carry

    _two_phase_loop(n_chunks, body, fin)

    if emit_state:
        for dst, src in zip(outs[0::3], (cm_f, cm_b)):
            dst[0, 0, 0] = src[...]
        for dst, src in zip(outs[1::3], (nv_f, nv_b)):
            dst[0, 0] = src[...]
        for dst, m_last in zip(outs[2::3], (m_last_f, m_last_b)):
            dst[0, 0] = m_last[:, 0:1]


def mlstm_core(proj, gates_t, gn, state, b, l, emit_state):
    hh = ML_HEADS
    width = proj.shape[-1]
    dqk = width // (6 * hh)
    dv = 2 * dqk
    has_state = state is not None
    n_chunks = l // CHUNK
    chunk_rows = gates_t.shape[3]
    in_specs = [
        pl.BlockSpec((1, l, dqk), lambda bi, h: (bi, 0, h)),
        pl.BlockSpec((1, l, dqk), lambda bi, h: (bi, 0, hh + h)),
        pl.BlockSpec((1, l, dv), lambda bi, h: (bi, 0, hh + h)),
        pl.BlockSpec((1, l, dv), lambda bi, h: (bi, 0, 2 * hh + h)),
        pl.BlockSpec((1, 1, 4, chunk_rows, CHUNK), lambda bi, h: (bi, h, 0, 0, 0)),
        pl.BlockSpec((1, dv), lambda bi, h: (0, h)),
    ]
    args = [proj, proj, proj, proj, gates_t, gn.reshape(1, hh * dv)]
    c_spec_in = pl.BlockSpec((1, 1, 1, dqk, dv), lambda bi, h: (bi, 0, h, 0, 0))
    n_spec = pl.BlockSpec((1, 1, 1, dqk), lambda bi, h: (bi, h, 0, 0))
    m_spec = pl.BlockSpec((1, 1, 1, 1), lambda bi, h: (bi, h, 0, 0))
    if has_state:
        cf, nf, mf, cb, nb, mb = state
        in_specs += [c_spec_in, n_spec, m_spec, c_spec_in, n_spec, m_spec]
        args += [cf, nf.reshape(b, hh, 1, dqk), mf.reshape(b, hh, 1, 1),
                 cb, nb.reshape(b, hh, 1, dqk), mb.reshape(b, hh, 1, 1)]
    out_specs = [pl.BlockSpec((1, l, dv), lambda bi, h: (bi, 0, h))]
    out_shape = [jax.ShapeDtypeStruct((b, l, hh * dv), BF16)]
    if emit_state:
        out_specs += [c_spec_in, n_spec, m_spec] * 2
        out_shape += [jax.ShapeDtypeStruct((b, 1, hh, dqk, dv), F32),
                      jax.ShapeDtypeStruct((b, hh, 1, dqk), F32),
                      jax.ShapeDtypeStruct((b, hh, 1, 1), F32)] * 2
    return pl.pallas_call(
        functools.partial(_ml_kernel, n_chunks=n_chunks, has_state=has_state, emit_state=emit_state),
        grid=(b, hh),
        in_specs=in_specs,
        out_specs=out_specs,
        out_shape=out_shape,
        scratch_shapes=[pltpu.VMEM((dqk, dv), F32), pltpu.VMEM((dqk, dv), F32),
                        pltpu.VMEM((1, dqk), F32), pltpu.VMEM((1, dqk), F32),
                        pltpu.VMEM((l, dv), F32), pltpu.VMEM((l, dv), F32)]
                       + [pltpu.VMEM((chunk_rows, CHUNK), F32)] * 4 + [pltpu.VMEM((chunk_rows, LANES), F32)] * 4,
        compiler_params=_params("parallel", "parallel"),
        name="mlstm",
    )(*args)


def _hy_filter_kernel(feat_ref, w1_ref, b1_ref, w2_ref, b2_ref, w3_ref, fr_ref, win_ref, sum_ref, dif_ref):
    d = win_ref.shape[-1]
    z = jnp.sin(fr_ref[0:1, :] * (_dot(feat_ref[...].astype(BF16), w1_ref[...].astype(BF16)) + b1_ref[...]))
    z = jnp.sin(fr_ref[1:2, :] * (_dot(z.astype(BF16), w2_ref[...].astype(BF16)) + b2_ref[...]))
    filt = _dot(z.astype(BF16), w3_ref[...].astype(BF16))
    win = win_ref[...]
    ff = filt[:, :d] * win
    fb = filt[:, d:] * win
    sum_ref[...] = (ff + fb).astype(sum_ref.dtype)
    dif_ref[...] = (ff - fb).astype(dif_ref.dtype)


def hyena_filters(l, d, w1, b1, w2, b2, w3, freq):
    t = jnp.linspace(0.0, 1.0, l, dtype=F32)[:, None]
    pos = jnp.arange(l, dtype=F32)[:, None]
    bands = jnp.linspace(1e-4, HY_BANDS - 1, HY_BANDS, dtype=F32)
    ang = 2.0 * math.pi * pos * bands / l
    feats = jnp.concatenate([t, jnp.cos(ang), -jnp.sin(ang)], axis=-1)
    emb = feats.shape[1]
    feats = jnp.pad(feats, ((0, 0), (0, LANES - emb)))
    w1p = jnp.pad(w1, ((0, LANES - emb), (0, 0)))
    deltas = jnp.abs(jnp.linspace(math.log(HY_TARGET) / HY_SLOW_DECAY,
                                  math.log(HY_TARGET) / HY_FAST_DECAY, d, dtype=F32))
    window = jnp.exp(-t * deltas)
    hid = w2.shape[0]
    tl = min(l, 512)
    full = lambda shp: pl.BlockSpec(shp, lambda i: (0,) * len(shp))
    return pl.pallas_call(
        _hy_filter_kernel,
        grid=(l // tl,),
        in_specs=[
            pl.BlockSpec((tl, LANES), lambda i: (i, 0)),
            full((LANES, hid)), full((1, hid)), full((hid, hid)), full((1, hid)), full((hid, 2 * d)),
            full((2, hid)),
            pl.BlockSpec((tl, d), lambda i: (i, 0)),
        ],
        out_specs=[pl.BlockSpec((tl, d), lambda i: (i, 0))] * 2,
        out_shape=[jax.ShapeDtypeStruct((l, d), BF16)] * 2,
        compiler_params=_params("parallel"),
        name="hyena_filter",
    )(feats, w1p, b1.reshape(1, hid), w2, b2.reshape(1, hid), w3, freq, window)


def _dft_tables(l):
    n = 2 * l
    blk = 64
    assert l % blk == 0

    def trig(step, count):
        k = lax.broadcasted_iota(jnp.int32, (l, count), 0)
        t = lax.broadcasted_iota(jnp.int32, (l, count), 1)
        ang = (((2 * k + 1) * step * t) % (2 * n)).astype(F32) * (math.pi / n)
        return jnp.cos(ang), jnp.sin(ang)

    c1, s1 = trig(blk, l // blk)
    c0, s0 = trig(1, blk)
    cos_kt = (c1[:, :, None] * c0[:, None, :] - s1[:, :, None] * s0[:, None, :]).reshape(l, l).astype(BF16)
    sin_kt = (s1[:, :, None] * c0[:, None, :] + c1[:, :, None] * s0[:, None, :]).reshape(l, l).astype(BF16)
    c1t, s1t, c0t, s0t = c1.T, s1.T, c0.T, s0.T
    cos_tk = (c1t[:, None, :] * c0t[None, :, :] - s1t[:, None, :] * s0t[None, :, :]).reshape(l, l).astype(BF16)
    sin_tk = (s1t[:, None, :] * c0t[None, :, :] + c1t[:, None, :] * s0t[None, :, :]).reshape(l, l).astype(BF16)
    return cos_kt, sin_kt, cos_tk, sin_tk


def _hy_spec_kernel(c_ref, s_ref, fs_ref, fd_ref, bias_ref, gr_ref, gs_ref):
    gr_ref[...] = _dot(c_ref[...], fs_ref[...]) + bias_ref[...]
    gs_ref[...] = _dot(s_ref[...], fd_ref[...])


def hyena_filter_spectrum(cos_kt, sin_kt, f_sum, f_dif, f_bias):
    l, d = f_sum.shape
    tm = min(l, 512)
    tn = min(d, 512)
    return pl.pallas_call(
        _hy_spec_kernel,
        grid=(l // tm, d // tn),
        in_specs=[
            pl.BlockSpec((tm, l), lambda i, j: (i, 0)),
            pl.BlockSpec((tm, l), lambda i, j: (i, 0)),
            pl.BlockSpec((l, tn), lambda i, j: (0, j)),
            pl.BlockSpec((l, tn), lambda i, j: (0, j)),
            pl.BlockSpec((1, tn), lambda i, j: (0, j)),
        ],
        out_specs=[pl.BlockSpec((tm, tn), lambda i, j: (i, j))] * 2,
        out_shape=[jax.ShapeDtypeStruct((l, d), F32)] * 2,
        compiler_params=_params("parallel", "parallel"),
        name="hyena_filter_spectrum",
    )(cos_kt, sin_kt, f_sum, f_dif, f_bias.reshape(1, d))


def _hy_conv_kernel(p0_ref, p1_ref, pv_ref, w0_ref, w1_ref, wv_ref, b0_ref, b1_ref, bv_ref, z_ref, x0_ref):
    l = p0_ref.shape[1]
    t = lax.broadcasted_iota(jnp.int32, (l, 1), 0)

    def conv(p_ref, w_ref, b_ref):
        p = p_ref[0].astype(F32)
        prev = jnp.where(t == 0, 0.0, pltpu.roll(p, 1, axis=0))
        nxt = jnp.where(t == l - 1, 0.0, pltpu.roll(p, l - 1, axis=0))
        return b_ref[...] + prev * w_ref[0:1, :] + p * w_ref[1:2, :] + nxt * w_ref[2:3, :]

    x0_ref[0] = conv(p0_ref, w0_ref, b0_ref).astype(x0_ref.dtype)
    z_ref[0] = (conv(pv_ref, wv_ref, bv_ref) * conv(p1_ref, w1_ref, b1_ref)).astype(z_ref.dtype)


def hyena_short_conv(proj, conv_w, conv_b, b, l, d):
    tc = min(d, 512)
    nc = d // tc
    p_spec = lambda off: pl.BlockSpec((1, l, tc), lambda bi, j: (bi, 0, off * nc + j))
    w_spec = lambda off: pl.BlockSpec((HY_SHORT, tc), lambda bi, j: (0, off * nc + j))
    b_spec = lambda off: pl.BlockSpec((1, tc), lambda bi, j: (0, off * nc + j))
    cb = conv_b.reshape(1, 3 * d)
    return pl.pallas_call(
        _hy_conv_kernel,
        grid=(b, nc),
        in_specs=[p_spec(0), p_spec(1), p_spec(2), w_spec(0), w_spec(1), w_spec(2),
                  b_spec(0), b_spec(1), b_spec(2)],
        out_specs=[pl.BlockSpec((1, l, tc), lambda bi, j: (bi, 0, j))] * 2,
        out_shape=[jax.ShapeDtypeStruct((b, l, d), BF16), jax.ShapeDtypeStruct((b, l, d), BF16)],
        compiler_params=_params("parallel", "parallel"),
        name="hyena_short_conv",
    )(proj, proj, proj, conv_w, conv_w, conv_w, cb, cb, cb)


def _hy_fwd_kernel(c_ref, s_ref, z_ref, gr_ref, gs_ref, yr_ref, ys_ref):
    z = z_ref[0]
    zr = _dot(c_ref[...], z)
    zs = _dot(s_ref[...], z)
    gr = gr_ref[...]
    gs = gs_ref[...]
    yr_ref[0] = (zr * gr - zs * gs).astype(yr_ref.dtype)
    ys_ref[0] = (zr * gs + zs * gr).astype(ys_ref.dtype)


def hyena_forward_dft(cos_kt, sin_kt, z, g_r, g_s):
    b, l, d = z.shape
    tm = min(l, 512)
    tn = min(d, 512)
    return pl.pallas_call(
        _hy_fwd_kernel,
        grid=(l // tm, b, d // tn),
        in_specs=[
            pl.BlockSpec((tm, l), lambda i, bi, j: (i, 0)),
            pl.BlockSpec((tm, l), lambda i, bi, j: (i, 0)),
            pl.BlockSpec((1, l, tn), lambda i, bi, j: (bi, 0, j)),
            pl.BlockSpec((tm, tn), lambda i, bi, j: (i, j)),
            pl.BlockSpec((tm, tn), lambda i, bi, j: (i, j)),
        ],
        out_specs=[pl.BlockSpec((1, tm, tn), lambda i, bi, j: (bi, i, j))] * 2,
        out_shape=[jax.ShapeDtypeStruct((b, l, d), BF16)] * 2,
        compiler_params=_params("parallel", "parallel", "parallel"),
        name="hyena_forward_dft",
    )(cos_kt, sin_kt, z, g_r, g_s)


def _hy_inv_kernel(ct_ref, st_ref, yr_ref, ys_ref, x0_ref, o_ref, *, inv_scale):
    y = _dot(ct_ref[...], yr_ref[0]) + _dot(st_ref[...], ys_ref[0])
    o_ref[0] = (y * inv_scale * x0_ref[0]).astype(o_ref.dtype)


def hyena_inverse_dft(cos_tk, sin_tk, y_r, y_s, x0):
    b, l, d = y_r.shape
    tm = min(l, 512)
    tn = min(d, 512)
    return pl.pallas_call(
        functools.partial(_hy_inv_kernel, inv_scale=1.0 / l),
        grid=(l // tm, b, d // tn),
        in_specs=[
            pl.BlockSpec((tm, l), lambda i, bi, j: (i, 0)),
            pl.BlockSpec((tm, l), lambda i, bi, j: (i, 0)),
            pl.BlockSpec((1, l, tn), lambda i, bi, j: (bi, 0, j)),
            pl.BlockSpec((1, l, tn), lambda i, bi, j: (bi, 0, j)),
            pl.BlockSpec((1, tm, tn), lambda i, bi, j: (bi, i, j)),
        ],
        out_specs=pl.BlockSpec((1, tm, tn), lambda i, bi, j: (bi, i, j)),
        out_shape=jax.ShapeDtypeStruct((b, l, d), BF16),
        compiler_params=_params("parallel", "parallel", "parallel"),
        name="hyena_inverse_dft",
    )(cos_tk, sin_tk, y_r, y_s, x0)


def _tile_rows(grp):
    span = grp.l if grp.per_batch else grp.t
    return next(tm for tm in (1024, 512, 256, 128) if span % tm == 0)


def kernel(x_prompt, x_sample, cache_k, cache_v, state_ret_fwd, state_ret_bwd, state_ml_C_fwd, state_ml_n_fwd, state_ml_m_fwd, state_ml_C_bwd, state_ml_n_bwd, state_ml_m_bwd, c, c_ctx, mod_w, mod_b, norm_mix_pre, norm_mix_post, norm_ffn_pre, norm_ffn_post, mlp_w1, mlp_w2, ret_w_in, ret_decay_fwd, ret_decay_bwd, ret_gn, ret_w_out, att_w_in, att_q_gain, att_k_gain, att_w_out, ml_w_in, ml_gate_b, ml_gn, ml_w_out, hy_w_in, hy_b_in, hy_conv_w, hy_conv_b, hy_f_w1, hy_f_b1, hy_f_w2, hy_f_b2, hy_f_w3, hy_sin_freq, hy_f_bias, hy_w_out):
    bp, lp, d = x_prompt.shape
    bs, ls, _ = x_sample.shape
    depth = mod_w.shape[0]
    n_mixers = 4
    mod_rows = 16
    assert 1 + bs <= mod_rows

    grp_p = Group(bp, lp, 0, False)
    grp_s = Group(bs, ls, 1, True)
    groups = (grp_p, grp_s)

    cond = jnp.concatenate([c_ctx[None, :], c, jnp.zeros((mod_rows - 1 - bs, d), F32)], axis=0)
    mod_all = adaln_all(cond, mod_w, mod_b)

    xs = [x_prompt.reshape(grp_p.t, d), x_sample.reshape(grp_s.t, d)]
    new_k = new_v = new_rf = new_rb = None
    new_ml = None

    for i in range(depth):
        mixer = i % n_mixers
        j = i // n_mixers
        mod3 = mod_all[i].reshape(mod_rows, 1, 6 * d)
        ys = []
        for gi, grp in enumerate(groups):
            x = xs[gi]
            tm = _tile_rows(grp)
            is_prompt = gi == 0
            if mixer == 0:
                w_in = ret_w_in[j].astype(BF16)
                proj = norm_matmul(x, norm_mix_pre[i], mod3, 0, 1, w_in, None, BF16, grp, tm, PROJ_TILE_N)
                dec = jnp.stack([ret_decay_fwd[j], ret_decay_bwd[j]]).astype(F32)
                s0f = None if is_prompt else state_ret_fwd
                s0b = None if is_prompt else state_ret_bwd
                assert is_prompt or state_ret_fwd.shape[1] == 1
                res = retention_core(proj.reshape(grp.b, grp.l, -1), dec, ret_gn[j], s0f, s0b,
                                     grp.b, grp.l, emit_state=is_prompt)
                if is_prompt:
                    new_rf, new_rb = res[1], res[2]
                a = res[0].reshape(grp.t, -1)
                w_out = ret_w_out[j].astype(BF16)
            elif mixer == 1:
                w_in = att_w_in[j].astype(BF16)
                proj = norm_matmul(x, norm_mix_pre[i], mod3, 0, 1, w_in, None, F32, grp, tm, PROJ_TILE_N)
                proj = proj.reshape(grp.b, grp.l, -1)
                if is_prompt:
                    res = attention_core(proj, att_q_gain[j], att_k_gain[j], None, None, None,
                                         grp.b, grp.l, min(grp.l, 256), emit_kv=True)
                    new_k = res[1].reshape(grp.b, 1, grp.l, ATT_KV, ATT_HD)
                    new_v = res[2].reshape(grp.b, 1, grp.l, ATT_KV, ATT_HD)
                else:
                    assert cache_k.shape[1] == 1
                    ck = cache_k.reshape(grp.b, cache_k.shape[2], ATT_KV * ATT_HD)
                    cv = cache_v.reshape(grp.b, cache_v.shape[2], ATT_KV * ATT_HD)
                    res = attention_core(proj, att_q_gain[j], att_k_gain[j], _rope_tables(grp.l), ck, cv,
                                         grp.b, grp.l, min(grp.l, 256), emit_kv=False)
                a = res[0].reshape(grp.t, -1)
                w_out = att_w_out[j].astype(BF16)
            elif mixer == 2:
                hh = ML_HEADS
                n_main = ml_w_in.shape[2] - 4 * hh
                w_main = ml_w_in[j][:, :n_main].astype(BF16)
                w_gate = jnp.pad(ml_w_in[j][:, n_main:], ((0, 0), (0, LANES - 4 * hh))).astype(BF16)
                b_gate = jnp.pad(ml_gate_b[j], (0, LANES - 4 * hh))
                proj = norm_matmul(x, norm_mix_pre[i], mod3, 0, 1, w_main, None, BF16, grp, tm, PROJ_TILE_N)
                gates = norm_matmul(x, norm_mix_pre[i], mod3, 0, 1, w_gate, b_gate, F32, grp, tm, LANES)
                gates = gates[:, :4 * hh].reshape(grp.b, grp.l, 4, hh)
                n_chunks = grp.l // CHUNK
                gates_t = gates.transpose(0, 3, 2, 1).reshape(grp.b, hh, 4, n_chunks, CHUNK)
                gates_t = jnp.pad(gates_t, ((0, 0), (0, 0), (0, 0), (0, -n_chunks % 16), (0, 0)))
                state = None
                if not is_prompt:
                    assert state_ml_C_fwd.shape[1] == 1
                    state = (state_ml_C_fwd, state_ml_n_fwd, state_ml_m_fwd,
                             state_ml_C_bwd, state_ml_n_bwd, state_ml_m_bwd)
                res = mlstm_core(proj.reshape(grp.b, grp.l, -1), gates_t, ml_gn[j], state,
                                 grp.b, grp.l, emit_state=is_prompt)
                if is_prompt:
                    dqk = res[2].shape[-1]
                    new_ml = (res[1], res[2].reshape(grp.b, 1, hh, dqk), res[3].reshape(grp.b, 1, hh),
                              res[4], res[5].reshape(grp.b, 1, hh, dqk), res[6].reshape(grp.b, 1, hh))
                a = res[0].reshape(grp.t, -1)
                w_out = ml_w_out[j].astype(BF16)
            else:
                w_in = hy_w_in[j].astype(BF16)
                proj = norm_matmul(x, norm_mix_pre[i], mod3, 0, 1, w_in, hy_b_in[j], BF16, grp, tm, PROJ_TILE_N)
                z, x0 = hyena_short_conv(proj.reshape(grp.b, grp.l, 3 * d), hy_conv_w[j], hy_conv_b[j],
                                         grp.b, grp.l, d)
                f_sum, f_dif = hyena_filters(grp.l, d, hy_f_w1[j], hy_f_b1[j], hy_f_w2[j], hy_f_b2[j],
                                             hy_f_w3[j], hy_sin_freq[j])
                cos_kt, sin_kt, cos_tk, sin_tk = _dft_tables(grp.l)
                g_r, g_s = hyena_filter_spectrum(cos_kt, sin_kt, f_sum, f_dif, hy_f_bias[j])
                y_r, y_s = hyena_forward_dft(cos_kt, sin_kt, z, g_r, g_s)
                a = hyena_inverse_dft(cos_tk, sin_tk, y_r, y_s, x0).reshape(grp.t, d)
                w_out = hy_w_out[j].astype(BF16)
            x, h_mlp = matmul_resnorm(a, w_out, x, mod3, norm_mix_post[i], norm_ffn_pre[i], grp, tm)
            x = mlp_block(h_mlp, x, norm_ffn_post[i], mod3,
                          mlp_w1, mlp_w2, i, grp, tm, MLP_TILE_F)
            ys.append(x)
        xs = ys

    y_prompt = xs[0].reshape(bp, lp, d)
    y_sample = xs[1].reshape(bs, ls, d)
    return (y_prompt, y_sample, new_k, new_v, new_rf, new_rb) + tuple(new_ml)
```

```python
import functools
import math

import jax
import jax.numpy as jnp
import numpy as np
from jax import lax
from jax.experimental import pallas as pl
from jax.experimental.pallas import tpu as pltpu

F32 = jnp.float32
BF16 = jnp.bfloat16

EPS = 1e-6
CHUNK = 128
RET_CHUNK = 256
GRID_W = 64
ROPE_THETA = 10000.0

RET_HEADS = 4
ATT_HEADS = 8
ATT_KV = 2
ATT_HD = 128
ML_HEADS = 4

HY_BANDS = 16
HY_SHORT = 3
HY_FAST_DECAY = 0.3
HY_SLOW_DECAY = 1.5
HY_TARGET = 1e-2

VMEM_LIMIT_BYTES = 56 * 1024 * 1024
LANES = 128
PROJ_TILE_N = 1536
MLP_TILE_F = 1024


def _params(*sem):
    return pltpu.CompilerParams(dimension_semantics=sem, vmem_limit_bytes=VMEM_LIMIT_BYTES)


def _dot(a, b):
    return jnp.dot(a, b, preferred_element_type=F32)


def _dot_nt(a, b):
    return lax.dot_general(a, b, (((1,), (1,)), ((), ())), preferred_element_type=F32)


def _dot_tn(a, b):
    return lax.dot_general(a, b, (((0,), (0,)), ((), ())), preferred_element_type=F32)


def _rms(x, g):
    return x * lax.rsqrt(jnp.mean(x * x, axis=-1, keepdims=True) + EPS) * g


def _log_sigmoid(x):
    return jnp.minimum(x, 0.0) - jnp.log1p(jnp.exp(-jnp.abs(x)))


def _chunk_offset(ch, c):
    return ch * c if isinstance(ch, int) else pl.multiple_of(ch * c, c)


def _two_phase_loop(n_chunks, body, fin, max_unroll=2):
    if n_chunks == 1:
        body(0, 0)
        fin(0)
        return
    assert n_chunks % 2 == 0
    half = n_chunks // 2
    unroll = math.gcd(half, max_unroll)

    def body_fin(i, carry):
        body(i, carry)
        fin(i)
        fin(n_chunks - 1 - i)
        return carry

    lax.fori_loop(0, half, body, 0, unroll=unroll)
    lax.fori_loop(half, n_chunks, body_fin, 0, unroll=unroll)


class Group:
    def __init__(self, b, l, row0, per_batch):
        self.b, self.l, self.row0, self.per_batch = b, l, row0, per_batch
        self.t = b * l

    def mod_spec(self, chunk, tm, d):
        row0, per_batch, l = self.row0, self.per_batch, self.l
        if per_batch:
            return pl.BlockSpec((1, 1, d), lambda i, *_: (row0 + (i * tm) // l, 0, chunk))
        return pl.BlockSpec((1, 1, d), lambda i, *_: (row0, 0, chunk))


def _adaln_kernel(c_ref, w_ref, b_ref, o_ref):
    s = jax.nn.silu(c_ref[...])
    o_ref[0] = _dot(s.astype(BF16), w_ref[0].astype(BF16)) + b_ref[0]


def adaln_all(cond, mod_w, mod_b):
    depth, d, n = mod_w.shape
    rows = cond.shape[0]
    tn = 768
    return pl.pallas_call(
        _adaln_kernel,
        grid=(depth, n // tn),
        in_specs=[
            pl.BlockSpec((rows, d), lambda l, j: (0, 0)),
            pl.BlockSpec((1, d, tn), lambda l, j: (l, 0, j)),
            pl.BlockSpec((1, 1, tn), lambda l, j: (l, 0, j)),
        ],
        out_specs=pl.BlockSpec((1, rows, tn), lambda l, j: (l, 0, j)),
        out_shape=jax.ShapeDtypeStruct((depth, rows, n), F32),
        compiler_params=_params("parallel", "parallel"),
        name="adaln",
    )(cond, mod_w, mod_b.reshape(depth, 1, n))


def _norm_mm_kernel(x_ref, g_ref, sh_ref, sc_ref, w_ref, *rest, has_bias):
    if has_bias:
        b_ref, o_ref, h_scr = rest
    else:
        o_ref, h_scr = rest

    @pl.when(pl.program_id(1) == 0)
    def _():
        y = _rms(x_ref[...], g_ref[...])
        h_scr[...] = (y * (1.0 + sc_ref[0]) + sh_ref[0]).astype(BF16)

    acc = _dot(h_scr[...], w_ref[...])
    if has_bias:
        acc = acc + b_ref[...]
    o_ref[...] = acc.astype(o_ref.dtype)


def norm_matmul(x, gain, mod3, sh_idx, sc_idx, w, bias, out_dtype, grp, tm, tn):
    t, d = x.shape
    n = w.shape[1]
    in_specs = [
        pl.BlockSpec((tm, d), lambda i, j: (i, 0)),
        pl.BlockSpec((1, d), lambda i, j: (0, 0)),
        grp.mod_spec(sh_idx, tm, d),
        grp.mod_spec(sc_idx, tm, d),
        pl.BlockSpec((d, tn), lambda i, j: (0, j)),
    ]
    args = [x, gain.reshape(1, d), mod3, mod3, w]
    if bias is not None:
        in_specs.append(pl.BlockSpec((1, tn), lambda i, j: (0, j)))
        args.append(bias.reshape(1, n))
    return pl.pallas_call(
        functools.partial(_norm_mm_kernel, has_bias=bias is not None),
        grid=(t // tm, n // tn),
        in_specs=in_specs,
        out_specs=pl.BlockSpec((tm, tn), lambda i, j: (i, j)),
        out_shape=jax.ShapeDtypeStruct((t, n), out_dtype),
        scratch_shapes=[pltpu.VMEM((tm, d), BF16)],
        compiler_params=_params("parallel", "arbitrary"),
        name="norm_matmul",
    )(*args)


def _mm_res_kernel(a_ref, w_ref, x_ref, gate_ref, pg_ref, ng_ref, sh_ref, sc_ref, o_ref, h_ref):
    y = _dot(a_ref[...], w_ref[...])
    x1 = x_ref[...] + gate_ref[0] * _rms(y, pg_ref[...])
    o_ref[...] = x1
    h_ref[...] = (_rms(x1, ng_ref[...]) * (1.0 + sc_ref[0]) + sh_ref[0]).astype(h_ref.dtype)


def matmul_resnorm(a, w, x, mod3, post_gain, next_gain, grp, tm):
    t, k = a.shape
    d = w.shape[1]
    row = pl.BlockSpec((tm, d), lambda i: (i, 0))
    vec = pl.BlockSpec((1, d), lambda i: (0, 0))
    return pl.pallas_call(
        _mm_res_kernel,
        grid=(t // tm,),
        in_specs=[
            pl.BlockSpec((tm, k), lambda i: (i, 0)),
            pl.BlockSpec((k, d), lambda i: (0, 0)),
            row,
            grp.mod_spec(2, tm, d),
            vec,
            vec,
            grp.mod_spec(3, tm, d),
            grp.mod_spec(4, tm, d),
        ],
        out_specs=[row, row],
        out_shape=[jax.ShapeDtypeStruct((t, d), F32), jax.ShapeDtypeStruct((t, d), BF16)],
        compiler_params=_params("parallel"),
        name="out_proj",
    )(a, w, x, mod3, post_gain.reshape(1, d), next_gain.reshape(1, d), mod3, mod3)


def _mlp_kernel(h_ref, x_ref, w1_ref, w2_ref, gate_ref, pg_ref, o_ref, acc_scr):
    j = pl.program_id(1)

    @pl.when(j == 0)
    def _():
        acc_scr[...] = jnp.zeros_like(acc_scr)

    u = _dot(h_ref[...], w1_ref[...].astype(BF16))
    u = jnp.square(jnp.maximum(u, 0.0)).astype(BF16)
    acc_scr[...] += _dot(u, w2_ref[...].astype(BF16))

    @pl.when(j == pl.num_programs(1) - 1)
    def _():
        o_ref[...] = x_ref[...] + gate_ref[0] * _rms(acc_scr[...], pg_ref[...])


def mlp_block(h, x, post_gain, mod3, w1, w2, layer, grp, tm, tf):
    t, d = x.shape
    f = w1.shape[2]
    return pl.pallas_call(
        _mlp_kernel,
        grid=(t // tm, f // tf),
        in_specs=[
            pl.BlockSpec((tm, d), lambda i, j: (i, 0)),
            pl.BlockSpec((tm, d), lambda i, j: (i, 0)),
            pl.BlockSpec((None, d, tf), lambda i, j: (layer, 0, j)),
            pl.BlockSpec((None, tf, d), lambda i, j: (layer, j, 0)),
            grp.mod_spec(5, tm, d),
            pl.BlockSpec((1, d), lambda i, j: (0, 0)),
        ],
        out_specs=pl.BlockSpec((tm, d), lambda i, j: (i, 0)),
        out_shape=jax.ShapeDtypeStruct((t, d), F32),
        scratch_shapes=[pltpu.VMEM((tm, d), F32)],
        compiler_params=_params("parallel", "arbitrary"),
        name="mlp",
    )(h, x, w1, w2, mod3, post_gain.reshape(1, d))


def _ret_kernel(dec_ref, q_ref, k_ref, v_ref, g_ref, gn_ref, *rest, n_chunks, has_state, emit_state):
    rest = list(rest)
    s0f_ref = s0b_ref = sf_ref = sb_ref = None
    if has_state:
        s0f_ref, s0b_ref = rest[:2]
        rest = rest[2:]
    o_ref = rest.pop(0)
    if emit_state:
        sf_ref, sb_ref = rest[:2]
        rest = rest[2:]
    st_f, st_b, of_scr, ob_scr = rest

    c = RET_CHUNK
    dk = q_ref.shape[-1]
    h = pl.program_id(1)
    lg_f = _log_sigmoid(jnp.full((1, 1), dec_ref[0, h], F32))
    lg_b = _log_sigmoid(jnp.full((1, 1), dec_ref[1, h], F32))
    ri = lax.broadcasted_iota(jnp.int32, (c, c), 0)
    ci = lax.broadcasted_iota(jnp.int32, (c, c), 1)
    rel = (ri - ci).astype(F32)
    intra_f = jnp.where(rel >= 0, jnp.exp(lg_f * jnp.maximum(rel, 0.0)), 0.0)
    intra_b = jnp.where(rel <= 0, jnp.exp(lg_b * jnp.maximum(-rel, 0.0)), 0.0)
    idx = lax.broadcasted_iota(jnp.int32, (c, 1), 0).astype(F32)
    qdec_f = jnp.exp(lg_f * (idx + 1.0))
    kdec_f = jnp.exp(lg_f * (c - 1.0 - idx))
    qdec_b = jnp.exp(lg_b * (c - idx))
    kdec_b = jnp.exp(lg_b * idx)
    cdec_f = jnp.exp(lg_f * c)
    cdec_b = jnp.exp(lg_b * c)
    q_scale = dk ** -0.5

    if has_state:
        st_f[...] = s0f_ref[0, 0, 0]
        st_b[...] = s0b_ref[0, 0, 0]
    else:
        st_f[...] = jnp.zeros_like(st_f)
        st_b[...] = jnp.zeros_like(st_b)

    def step(ch, st, out_scr, intra, qdec, kdec, cdec):
        rows = pl.ds(_chunk_offset(ch, c), c)
        qc = q_ref[0, rows, :]
        kc = k_ref[0, rows, :]
        vc = v_ref[0, rows, :]
        s_old = st[...]
        sc = _dot_nt(qc, kc) * (intra * q_scale)
        o = _dot(sc.astype(BF16), vc) + _dot(qc, s_old.astype(BF16)) * (qdec * q_scale)
        kw = (kc.astype(F32) * kdec).astype(BF16)
        st[...] = s_old * cdec + _dot_tn(kw, vc)
        out_scr[rows, :] = o

    def fin(ch):
        rows = pl.ds(_chunk_offset(ch, c), c)
        o = _rms(of_scr[rows, :] + ob_scr[rows, :], gn_ref[...])
        o_ref[0, rows, :] = (jax.nn.silu(g_ref[0, rows, :].astype(F32)) * o).astype(o_ref.dtype)

    def body(i, carry):
        step(i, st_f, of_scr, intra_f, qdec_f, kdec_f, cdec_f)
        step(n_chunks - 1 - i, st_b, ob_scr, intra_b, qdec_b, kdec_b, cdec_b)
        return carry

    _two_phase_loop(n_chunks, body, fin)

    if emit_state:
        sf_ref[0, 0, 0] = st_f[...]
        sb_ref[0, 0, 0] = st_b[...]


def retention_core(proj, dec, gn, s0f, s0b, b, l, emit_state):
    hh = RET_HEADS
    width = proj.shape[-1]
    dk = width // (6 * hh)
    dv = 2 * dk
    has_state = s0f is not None
    n_chunks = l // RET_CHUNK
    in_specs = [
        pl.BlockSpec(memory_space=pltpu.SMEM),
        pl.BlockSpec((1, l, dk), lambda bi, h: (bi, 0, h)),
        pl.BlockSpec((1, l, dk), lambda bi, h: (bi, 0, hh + h)),
        pl.BlockSpec((1, l, dv), lambda bi, h: (bi, 0, hh + h)),
        pl.BlockSpec((1, l, dv), lambda bi, h: (bi, 0, 2 * hh + h)),
        pl.BlockSpec((1, dv), lambda bi, h: (0, h)),
    ]
    args = [dec, proj, proj, proj, proj, gn.reshape(1, hh * dv)]
    if has_state:
        st_spec = pl.BlockSpec((1, 1, 1, dk, dv), lambda bi, h: (bi, 0, h, 0, 0))
        in_specs += [st_spec, st_spec]
        args += [s0f, s0b]
    out_specs = [pl.BlockSpec((1, l, dv), lambda bi, h: (bi, 0, h))]
    out_shape = [jax.ShapeDtypeStruct((b, l, hh * dv), BF16)]
    if emit_state:
        so_spec = pl.BlockSpec((1, 1, 1, dk, dv), lambda bi, h: (bi, 0, h, 0, 0))
        out_specs += [so_spec, so_spec]
        out_shape += [jax.ShapeDtypeStruct((b, 1, hh, dk, dv), F32)] * 2
    return pl.pallas_call(
        functools.partial(_ret_kernel, n_chunks=n_chunks, has_state=has_state, emit_state=emit_state),
        grid=(b, hh),
        in_specs=in_specs,
        out_specs=out_specs,
        out_shape=out_shape,
        scratch_shapes=[pltpu.VMEM((dk, dv), F32), pltpu.VMEM((dk, dv), F32),
                        pltpu.VMEM((l, dv), F32), pltpu.VMEM((l, dv), F32)],
        compiler_params=_params("parallel", "parallel"),
        name="retention",
    )(*args)


def _rope_rot(x, cos_t, sin_t):
    lane = lax.broadcasted_iota(jnp.int32, x.shape, x.ndim - 1)
    nxt = pltpu.roll(x, LANES - 1, axis=x.ndim - 1)
    prv = pltpu.roll(x, 1, axis=x.ndim - 1)
    swapped = jnp.where(jnp.bitwise_and(lane, 1) == 0, nxt, prv)
    return x * cos_t + swapped * sin_t


def _att_kernel(q_ref, k_ref, v_ref, qg_ref, kg_ref, *rest, rope, has_cache, emit_kv, groups):
    rest = list(rest)
    cosq_ref = sinq_ref = cosk_ref = sink_ref = ck_ref = cv_ref = kn_ref = vo_ref = None
    if rope:
        cosq_ref, sinq_ref, cosk_ref, sink_ref = rest[:4]
        rest = rest[4:]
    if has_cache:
        ck_ref, cv_ref = rest[:2]
        rest = rest[2:]
    o_ref = rest.pop(0)
    if emit_kv:
        kn_ref, vo_ref = rest[:2]
        rest = rest[2:]
    k_scr, v_scr = rest
    hd = ATT_HD
    exp2_scale = hd ** -0.5 * math.log2(math.e)

    @pl.when(pl.program_id(2) == 0)
    def _():
        kn = _rms(k_ref[0], kg_ref[...])
        if emit_kv:
            kn_ref[0] = kn
            vo_ref[0] = v_ref[0]
        if rope:
            kn = _rope_rot(kn, cosk_ref[...], sink_ref[...])
        k_scr[...] = kn.astype(BF16)
        v_scr[:, :hd] = v_ref[0].astype(BF16)
        v_scr[:, hd:] = jnp.ones((v_scr.shape[0], hd), BF16)

    if has_cache:
        ck = ck_ref[0].astype(BF16)
        cv = jnp.concatenate([cv_ref[0].astype(BF16), jnp.ones((cv_ref.shape[1], hd), BF16)], axis=1)

    for g in range(groups):
        cols = slice(g * hd, (g + 1) * hd)
        qn = _rms(q_ref[0, :, cols], qg_ref[...])
        if rope:
            qn = _rope_rot(qn, cosq_ref[...], sinq_ref[...])
        qb = qn.astype(BF16)
        s1 = _dot_nt(qb, k_scr[...])
        m = jnp.max(s1, axis=-1, keepdims=True)
        if has_cache:
            s2 = _dot_nt(qb, ck)
            m = jnp.maximum(m, jnp.max(s2, axis=-1, keepdims=True))
        p1 = jnp.exp2((s1 - m) * exp2_scale)
        nd = _dot(p1.astype(BF16), v_scr[...])
        if has_cache:
            p2 = jnp.exp2((s2 - m) * exp2_scale)
            nd = nd + _dot(p2.astype(BF16), cv)
        o_ref[0, :, cols] = (nd[:, :hd] * (1.0 / nd[:, hd:])).astype(o_ref.dtype)


def attention_core(proj, q_gain, k_gain, rope_tabs, cache_k, cache_v, b, l, tq, emit_kv):
    hd, kv, heads = ATT_HD, ATT_KV, ATT_HEADS
    groups = heads // kv
    rope = rope_tabs is not None
    has_cache = cache_k is not None
    in_specs = [
        pl.BlockSpec((1, tq, groups * hd), lambda bi, kh, qi: (bi, qi, kh)),
        pl.BlockSpec((1, l, hd), lambda bi, kh, qi: (bi, 0, heads + kh)),
        pl.BlockSpec((1, l, hd), lambda bi, kh, qi: (bi, 0, heads + kv + kh)),
        pl.BlockSpec((1, hd), lambda bi, kh, qi: (0, 0)),
        pl.BlockSpec((1, hd), lambda bi, kh, qi: (0, 0)),
    ]
    args = [proj, proj, proj, q_gain.reshape(1, hd), k_gain.reshape(1, hd)]
    if rope:
        cos_t, sin_t = rope_tabs
        in_specs += [
            pl.BlockSpec((tq, hd), lambda bi, kh, qi: (qi, 0)),
            pl.BlockSpec((tq, hd), lambda bi, kh, qi: (qi, 0)),
            pl.BlockSpec((l, hd), lambda bi, kh, qi: (0, 0)),
            pl.BlockSpec((l, hd), lambda bi, kh, qi: (0, 0)),
        ]
        args += [cos_t, sin_t, cos_t, sin_t]
    if has_cache:
        past = cache_k.shape[1]
        c_spec = pl.BlockSpec((1, past, hd), lambda bi, kh, qi: (bi, 0, kh))
        in_specs += [c_spec, c_spec]
        args += [cache_k, cache_v]
    out_specs = [pl.BlockSpec((1, tq, groups * hd), lambda bi, kh, qi: (bi, qi, kh))]
    out_shape = [jax.ShapeDtypeStruct((b, l, heads * hd), BF16)]
    if emit_kv:
        kv_spec = pl.BlockSpec((1, l, hd), lambda bi, kh, qi: (bi, 0, kh))
        out_specs += [kv_spec, kv_spec]
        out_shape += [jax.ShapeDtypeStruct((b, l, kv * hd), F32)] * 2
    return pl.pallas_call(
        functools.partial(_att_kernel, rope=rope, has_cache=has_cache, emit_kv=emit_kv, groups=groups),
        grid=(b, kv, l // tq),
        in_specs=in_specs,
        out_specs=out_specs,
        out_shape=out_shape,
        scratch_shapes=[pltpu.VMEM((l, hd), BF16), pltpu.VMEM((l, 2 * hd), BF16)],
        compiler_params=_params("parallel", "parallel", "arbitrary"),
        name="attention",
    )(*args)


def _rope_tables(l):
    rows = l // GRID_W
    row = jnp.repeat(jnp.arange(rows, dtype=F32), GRID_W)
    col = jnp.tile(jnp.arange(GRID_W, dtype=F32), rows)
    half = ATT_HD // 2
    inv = ROPE_THETA ** (-jnp.arange(0, half, 2, dtype=F32) / half)
    ang = jnp.concatenate([row[:, None] * inv, col[:, None] * inv], axis=-1)
    cos_t = jnp.repeat(jnp.cos(ang), 2, axis=-1)
    sin_h = jnp.sin(ang)
    sin_t = jnp.stack([-sin_h, sin_h], axis=-1).reshape(l, ATT_HD)
    return cos_t, sin_t


def _ml_kernel(q_ref, k_ref, v_ref, og_ref, gt_ref, gn_ref, *rest, n_chunks, has_state, emit_state):
    rest = list(rest)
    c0f_ref = n0f_ref = m0f_ref = c0b_ref = n0b_ref = m0b_ref = None
    if has_state:
        c0f_ref, n0f_ref, m0f_ref, c0b_ref, n0b_ref, m0b_ref = rest[:6]
        rest = rest[6:]
    o_ref = rest.pop(0)
    outs = None
    if emit_state:
        outs = rest[:6]
        rest = rest[6:]
    cm_f, cm_b, nv_f, nv_b, hf_scr, hb_scr, fl_f, fl_b, br_f, br_b, mo_f, mo_b, mn_f, mn_b = rest[:14]
    sw_f, sw_b, kw_f, kw_b = rest[14:18]
    vt_f, vt_b = rest[18:23], rest[23:28]

    c = CHUNK
    dqk = q_ref.shape[-1]
    k_scale = dqk ** -0.5
    ri = lax.broadcasted_iota(jnp.int32, (c, c), 0)
    ci = lax.broadcasted_iota(jnp.int32, (c, c), 1)
    mask_f = ci <= ri
    mask_b = ci >= ri
    assert c == LANES and dqk == LANES
    dv_tiles = v_ref.shape[-1] // LANES

    def bf3(m):
        m = jnp.where(m, 1.0, 0.0).astype(BF16)
        return jnp.concatenate([m, m, m], axis=1), jnp.concatenate([m, m, m], axis=0)

    mf3, mf3_t = bf3(mask_f)
    mb3, mb3_t = bf3(mask_b)
    eye3, _ = bf3(ci == ri)

    def split3(x):
        hi = x.astype(BF16)
        r1 = x - hi.astype(F32)
        mid = r1.astype(BF16)
        lo = (r1 - mid.astype(F32)).astype(BF16)
        return jnp.concatenate([hi, mid, lo], axis=1)

    def wide(x):
        return jnp.concatenate([x] * dv_tiles, axis=1)

    if has_state:
        cm_f[...] = c0f_ref[0, 0, 0]
        cm_b[...] = c0b_ref[0, 0, 0]
        nv_f[...] = n0f_ref[0, 0]
        nv_b[...] = n0b_ref[0, 0]
        m0_f = jnp.broadcast_to(m0f_ref[0, 0], (1, LANES))
        m0_b = jnp.broadcast_to(m0b_ref[0, 0], (1, LANES))
    else:
        for r in (cm_f, cm_b, nv_f, nv_b):
            r[...] = jnp.zeros_like(r)
        m0_f = m0_b = jnp.zeros((1, LANES), F32)

    def gate_pass(row_i, row_f, m3_t, m0, fl, br, mo, mn, order):
        f_all = _log_sigmoid(gt_ref[0, 0, row_f])
        fl[...] = f_all
        b_all = _dot(split3(f_all), m3_t)
        br[...] = b_all
        b_end = jnp.sum(f_all, axis=-1, keepdims=True)
        w_max = jnp.max(b_end - b_all + gt_ref[0, 0, row_i], axis=-1, keepdims=True)
        m = m0
        for r in order:
            mo[r:r + 1, :] = m
            m = jnp.maximum(b_end[r:r + 1, :] + m, w_max[r:r + 1, :])
            mn[r:r + 1, :] = m
        return m

    m_last_f = gate_pass(0, 1, mb3_t, m0_f, fl_f, br_f, mo_f, mn_f, range(n_chunks))
    m_last_b = gate_pass(2, 3, mf3_t, m0_b, fl_b, br_b, mo_b, mn_b, reversed(range(n_chunks)))

    def weigh(ch, mask, m3, fl, br, mo, mn, row_i, sw, kw_s, wx_s, rs_s, em_s, cd_s, ks_s):
        rows = pl.ds(_chunk_offset(ch, c), c)
        one = pl.ds(ch, 1)
        qc = q_ref[0, rows, :]
        kc = k_ref[0, rows, :]
        i_row = gt_ref[0, 0, row_i, one, :]
        f_row = fl[one, :]
        b_row = br[one, :]
        m_old = mo[one, :]
        m_new = mn[one, :]
        b_q = _dot_nt(m3, jnp.broadcast_to(split3(f_row), (LANES, 3 * c)))
        i_q = _dot_nt(eye3, jnp.broadcast_to(split3(i_row), (LANES, 3 * c)))
        dlog = jnp.where(mask, b_q - b_row + i_row, -jnp.inf)
        inter = b_q + m_old
        m_q = jnp.maximum(inter, jnp.max(dlog, axis=-1, keepdims=True))
        s = _dot_nt(qc, kc) * (jnp.exp(dlog - m_q) * k_scale)
        sw[ch] = s.astype(BF16)
        rs_s[rows, :] = jnp.broadcast_to(jnp.sum(s, axis=-1, keepdims=True), (c, LANES))
        wx_s[rows, :] = jnp.exp(inter - m_q)
        em_s[rows, :] = jnp.exp(-m_q)
        b_end = jnp.sum(f_row, axis=-1, keepdims=True)
        wlog = b_end - b_q + i_q
        cd_s[one, :] = jnp.exp(b_end + m_old - m_new)
        kw = kc.astype(F32) * (jnp.exp(wlog - m_new) * k_scale)
        kw_s[ch] = kw.T.astype(BF16)
        ks_s[one, :] = jnp.sum(kw, axis=0, keepdims=True)

    def advance(ch, cm, nv, out_scr, sw, kw_s, wx_s, rs_s, em_s, cd_s, ks_s):
        rows = pl.ds(_chunk_offset(ch, c), c)
        one = pl.ds(ch, 1)
        qc = q_ref[0, rows, :]
        vc = v_ref[0, rows, :]
        c_old = cm[...]
        n_old = nv[...]
        w_x = wx_s[rows, :]
        num = _dot(sw[ch], vc) + _dot(qc, c_old.astype(BF16)) * wide(w_x)
        qn = _dot_nt(qc, jnp.broadcast_to(n_old, (LANES, dqk)).astype(BF16))
        den = jnp.maximum(jnp.abs(rs_s[rows, :] + qn * w_x), em_s[rows, :])
        out_scr[rows, :] = num * wide(1.0 / den)
        carry_dec = cd_s[one, :]
        cm[...] = c_old * wide(carry_dec) + _dot(kw_s[ch], vc)
        nv[...] = n_old * carry_dec + ks_s[one, :]

    def fin(ch):
        rows = pl.ds(_chunk_offset(ch, c), c)
        hn = _rms(hf_scr[rows, :] + hb_scr[rows, :], gn_ref[...])
        o_ref[0, rows, :] = (jax.nn.sigmoid(og_ref[0, rows, :].astype(F32)) * hn).astype(o_ref.dtype)

    tmp_f = (sw_f, kw_f) + tuple(vt_f)
    tmp_b = (sw_b, kw_b) + tuple(vt_b)

    def weigh_both(i):
        weigh(i, mask_f, mf3, fl_f, br_f, mo_f, mn_f, 0, *tmp_f)
        weigh(n_chunks - 1 - i, mask_b, mb3, fl_b, br_b, mo_b, mn_b, 2, *tmp_b)

    def advance_both(i):
        advance(i, cm_f, nv_f, hf_scr, *tmp_f)
        advance(n_chunks - 1 - i, cm_b, nv_b, hb_scr, *tmp_b)

    def advance_pair(i):
        advance_both(i)
        advance_both(i + 1)

    def fin_pair(i):
        chunks = (i, n_chunks - 1 - i, i + 1, n_chunks - 2 - i)
        for ch in (dict.fromkeys(chunks) if isinstance(i, int) else chunks):
            fin(ch)

    def body(p, carry):
        advance_pair(2 * p)
        weigh_both(2 * p + 2)
        weigh_both(2 * p + 3)
        return carry

    def body_fin(p, carry):
        advance_pair(2 * p)
        fin_pair(2 * p)
        weigh_both(2 * p + 2)
        weigh_both(2 * p + 3)
        return carry

    pairs = n_chunks // 2
    assert n_chunks == 2 or n_chunks % 4 == 0
    weigh_both(0)
    weigh_both(1)
    if pairs > 1:
        lax.fori_loop(0, pairs // 2, body, 0)
        lax.fori_loop(pairs // 2, pairs - 1, body_fin, 0)
    advance_pair(n_chunks - 2)
    fin_pair(n_chunks - 2)

    if emit_state:
        for dst, src in zip(outs[0::3], (cm_f, cm_b)):
            dst[0, 0, 0] = src[...]
        for dst, src in zip(outs[1::3], (nv_f, nv_b)):
            dst[0, 0] = src[...]
        for dst, m_last in zip(outs[2::3], (m_last_f, m_last_b)):
            dst[0, 0] = m_last[:, 0:1]


def mlstm_core(proj, gates_t, gn, state, b, l, emit_state):
    hh = ML_HEADS
    width = proj.shape[-1]
    dqk = width // (6 * hh)
    dv = 2 * dqk
    has_state = state is not None
    n_chunks = l // CHUNK
    chunk_rows = gates_t.shape[3]
    in_specs = [
        pl.BlockSpec((1, l, dqk), lambda bi, h: (bi, 0, h)),
        pl.BlockSpec((1, l, dqk), lambda bi, h: (bi, 0, hh + h)),
        pl.BlockSpec((1, l, dv), lambda bi, h: (bi, 0, hh + h)),
        pl.BlockSpec((1, l, dv), lambda bi, h: (bi, 0, 2 * hh + h)),
        pl.BlockSpec((1, 1, 4, chunk_rows, CHUNK), lambda bi, h: (bi, h, 0, 0, 0)),
        pl.BlockSpec((1, dv), lambda bi, h: (0, h)),
    ]
    args = [proj, proj, proj, proj, gates_t, gn.reshape(1, hh * dv)]
    c_spec_in = pl.BlockSpec((1, 1, 1, dqk, dv), lambda bi, h: (bi, 0, h, 0, 0))
    n_spec = pl.BlockSpec((1, 1, 1, dqk), lambda bi, h: (bi, h, 0, 0))
    m_spec = pl.BlockSpec((1, 1, 1, 1), lambda bi, h: (bi, h, 0, 0))
    if has_state:
        cf, nf, mf, cb, nb, mb = state
        in_specs += [c_spec_in, n_spec, m_spec, c_spec_in, n_spec, m_spec]
        args += [cf, nf.reshape(b, hh, 1, dqk), mf.reshape(b, hh, 1, 1),
                 cb, nb.reshape(b, hh, 1, dqk), mb.reshape(b, hh, 1, 1)]
    out_specs = [pl.BlockSpec((1, l, dv), lambda bi, h: (bi, 0, h))]
    out_shape = [jax.ShapeDtypeStruct((b, l, hh * dv), BF16)]
    if emit_state:
        out_specs += [c_spec_in, n_spec, m_spec] * 2
        out_shape += [jax.ShapeDtypeStruct((b, 1, hh, dqk, dv), F32),
                      jax.ShapeDtypeStruct((b, hh, 1, dqk), F32),
                      jax.ShapeDtypeStruct((b, hh, 1, 1), F32)] * 2
    return pl.pallas_call(
        functools.partial(_ml_kernel, n_chunks=n_chunks, has_state=has_state, emit_state=emit_state),
        grid=(b, hh),
        in_specs=in_specs,
        out_specs=out_specs,
        out_shape=out_shape,
        scratch_shapes=[pltpu.VMEM((dqk, dv), F32), pltpu.VMEM((dqk, dv), F32),
                        pltpu.VMEM((1, dqk), F32), pltpu.VMEM((1, dqk), F32),
                        pltpu.VMEM((l, dv), F32), pltpu.VMEM((l, dv), F32)]
                       + [pltpu.VMEM((chunk_rows, CHUNK), F32)] * 4 + [pltpu.VMEM((chunk_rows, LANES), F32)] * 4
                       + [pltpu.VMEM((n_chunks, CHUNK, CHUNK), BF16)] * 2 + [pltpu.VMEM((n_chunks, dqk, CHUNK), BF16)] * 2
                       + [pltpu.VMEM((l, LANES), F32)] * 3 + [pltpu.VMEM((chunk_rows, LANES), F32)] * 2
                       + [pltpu.VMEM((l, LANES), F32)] * 3 + [pltpu.VMEM((chunk_rows, LANES), F32)] * 2,
        compiler_params=_params("parallel", "parallel"),
        name="mlstm",
    )(*args)


def _hy_filter_kernel(feat_ref, w1_ref, b1_ref, w2_ref, b2_ref, w3_ref, fr_ref, win_ref, sum_ref, dif_ref):
    d = win_ref.shape[-1]
    z = jnp.sin(fr_ref[0:1, :] * (_dot(feat_ref[...].astype(BF16), w1_ref[...].astype(BF16)) + b1_ref[...]))
    z = jnp.sin(fr_ref[1:2, :] * (_dot(z.astype(BF16), w2_ref[...].astype(BF16)) + b2_ref[...]))
    filt = _dot(z.astype(BF16), w3_ref[...].astype(BF16))
    win = win_ref[...]
    ff = filt[:, :d] * win
    fb = filt[:, d:] * win
    sum_ref[...] = (ff + fb).astype(sum_ref.dtype)
    dif_ref[...] = (ff - fb).astype(dif_ref.dtype)


def hyena_filters(l, d, w1, b1, w2, b2, w3, freq):
    t = jnp.linspace(0.0, 1.0, l, dtype=F32)[:, None]
    pos = jnp.arange(l, dtype=F32)[:, None]
    bands = jnp.linspace(1e-4, HY_BANDS - 1, HY_BANDS, dtype=F32)
    ang = 2.0 * math.pi * pos * bands / l
    feats = jnp.concatenate([t, jnp.cos(ang), -jnp.sin(ang)], axis=-1)
    emb = feats.shape[1]
    feats = jnp.pad(feats, ((0, 0), (0, LANES - emb)))
    w1p = jnp.pad(w1, ((0, LANES - emb), (0, 0)))
    deltas = jnp.abs(jnp.linspace(math.log(HY_TARGET) / HY_SLOW_DECAY,
                                  math.log(HY_TARGET) / HY_FAST_DECAY, d, dtype=F32))
    window = jnp.exp(-t * deltas)
    hid = w2.shape[0]
    tl = min(l, 512)
    full = lambda shp: pl.BlockSpec(shp, lambda i: (0,) * len(shp))
    return pl.pallas_call(
        _hy_filter_kernel,
        grid=(l // tl,),
        in_specs=[
            pl.BlockSpec((tl, LANES), lambda i: (i, 0)),
            full((LANES, hid)), full((1, hid)), full((hid, hid)), full((1, hid)), full((hid, 2 * d)),
            full((2, hid)),
            pl.BlockSpec((tl, d), lambda i: (i, 0)),
        ],
        out_specs=[pl.BlockSpec((tl, d), lambda i: (i, 0))] * 2,
        out_shape=[jax.ShapeDtypeStruct((l, d), BF16)] * 2,
        compiler_params=_params("parallel"),
        name="hyena_filter",
    )(feats, w1p, b1.reshape(1, hid), w2, b2.reshape(1, hid), w3, freq, window)


def _dft_tables(l):
    n = 2 * l
    blk = 64
    assert l % blk == 0

    def trig(step, count):
        k = lax.broadcasted_iota(jnp.int32, (l, count), 0)
        t = lax.broadcasted_iota(jnp.int32, (l, count), 1)
        ang = (((2 * k + 1) * step * t) % (2 * n)).astype(F32) * (math.pi / n)
        return jnp.cos(ang), jnp.sin(ang)

    c1, s1 = trig(blk, l // blk)
    c0, s0 = trig(1, blk)
    cos_kt = (c1[:, :, None] * c0[:, None, :] - s1[:, :, None] * s0[:, None, :]).reshape(l, l).astype(BF16)
    sin_kt = (s1[:, :, None] * c0[:, None, :] + c1[:, :, None] * s0[:, None, :]).reshape(l, l).astype(BF16)
    c1t, s1t, c0t, s0t = c1.T, s1.T, c0.T, s0.T
    cos_tk = (c1t[:, None, :] * c0t[None, :, :] - s1t[:, None, :] * s0t[None, :, :]).reshape(l, l).astype(BF16)
    sin_tk = (s1t[:, None, :] * c0t[None, :, :] + c1t[:, None, :] * s0t[None, :, :]).reshape(l, l).astype(BF16)
    return cos_kt, sin_kt, cos_tk, sin_tk


def _hy_spec_kernel(c_ref, s_ref, fs_ref, fd_ref, bias_ref, gr_ref, gs_ref):
    gr_ref[...] = _dot(c_ref[...], fs_ref[...]) + bias_ref[...]
    gs_ref[...] = _dot(s_ref[...], fd_ref[...])


def hyena_filter_spectrum(cos_kt, sin_kt, f_sum, f_dif, f_bias):
    l, d = f_sum.shape
    tm = min(l, 512)
    tn = min(d, 512)
    return pl.pallas_call(
        _hy_spec_kernel,
        grid=(l // tm, d // tn),
        in_specs=[
            pl.BlockSpec((tm, l), lambda i, j: (i, 0)),
            pl.BlockSpec((tm, l), lambda i, j: (i, 0)),
            pl.BlockSpec((l, tn), lambda i, j: (0, j)),
            pl.BlockSpec((l, tn), lambda i, j: (0, j)),
            pl.BlockSpec((1, tn), lambda i, j: (0, j)),
        ],
        out_specs=[pl.BlockSpec((tm, tn), lambda i, j: (i, j))] * 2,
        out_shape=[jax.ShapeDtypeStruct((l, d), F32)] * 2,
        compiler_params=_params("parallel", "parallel"),
        name="hyena_filter_spectrum",
    )(cos_kt, sin_kt, f_sum, f_dif, f_bias.reshape(1, d))


def _hy_conv_kernel(p0_ref, p1_ref, pv_ref, w0_ref, w1_ref, wv_ref, b0_ref, b1_ref, bv_ref, z_ref, x0_ref):
    l = p0_ref.shape[1]
    t = lax.broadcasted_iota(jnp.int32, (l, 1), 0)

    def conv(p_ref, w_ref, b_ref):
        p = p_ref[0].astype(F32)
        prev = jnp.where(t == 0, 0.0, pltpu.roll(p, 1, axis=0))
        nxt = jnp.where(t == l - 1, 0.0, pltpu.roll(p, l - 1, axis=0))
        return b_ref[...] + prev * w_ref[0:1, :] + p * w_ref[1:2, :] + nxt * w_ref[2:3, :]

    x0_ref[0] = conv(p0_ref, w0_ref, b0_ref).astype(x0_ref.dtype)
    z_ref[0] = (conv(pv_ref, wv_ref, bv_ref) * conv(p1_ref, w1_ref, b1_ref)).astype(z_ref.dtype)


def hyena_short_conv(proj, conv_w, conv_b, b, l, d):
    tc = min(d, 512)
    nc = d // tc
    p_spec = lambda off: pl.BlockSpec((1, l, tc), lambda bi, j: (bi, 0, off * nc + j))
    w_spec = lambda off: pl.BlockSpec((HY_SHORT, tc), lambda bi, j: (0, off * nc + j))
    b_spec = lambda off: pl.BlockSpec((1, tc), lambda bi, j: (0, off * nc + j))
    cb = conv_b.reshape(1, 3 * d)
    return pl.pallas_call(
        _hy_conv_kernel,
        grid=(b, nc),
        in_specs=[p_spec(0), p_spec(1), p_spec(2), w_spec(0), w_spec(1), w_spec(2),
                  b_spec(0), b_spec(1), b_spec(2)],
        out_specs=[pl.BlockSpec((1, l, tc), lambda bi, j: (bi, 0, j))] * 2,
        out_shape=[jax.ShapeDtypeStruct((b, l, d), BF16), jax.ShapeDtypeStruct((b, l, d), BF16)],
        compiler_params=_params("parallel", "parallel"),
        name="hyena_short_conv",
    )(proj, proj, proj, conv_w, conv_w, conv_w, cb, cb, cb)


def _hy_fwd_kernel(c_ref, s_ref, z_ref, gr_ref, gs_ref, yr_ref, ys_ref):
    z = z_ref[0]
    zr = _dot(c_ref[...], z)
    zs = _dot(s_ref[...], z)
    gr = gr_ref[...]
    gs = gs_ref[...]
    yr_ref[0] = (zr * gr - zs * gs).astype(yr_ref.dtype)
    ys_ref[0] = (zr * gs + zs * gr).astype(ys_ref.dtype)


def hyena_forward_dft(cos_kt, sin_kt, z, g_r, g_s):
    b, l, d = z.shape
    tm = min(l, 512)
    tn = min(d, 512)
    return pl.pallas_call(
        _hy_fwd_kernel,
        grid=(l // tm, b, d // tn),
        in_specs=[
            pl.BlockSpec((tm, l), lambda i, bi, j: (i, 0)),
            pl.BlockSpec((tm, l), lambda i, bi, j: (i, 0)),
            pl.BlockSpec((1, l, tn), lambda i, bi, j: (bi, 0, j)),
            pl.BlockSpec((tm, tn), lambda i, bi, j: (i, j)),
            pl.BlockSpec((tm, tn), lambda i, bi, j: (i, j)),
        ],
        out_specs=[pl.BlockSpec((1, tm, tn), lambda i, bi, j: (bi, i, j))] * 2,
        out_shape=[jax.ShapeDtypeStruct((b, l, d), BF16)] * 2,
        compiler_params=_params("parallel", "parallel", "parallel"),
        name="hyena_forward_dft",
    )(cos_kt, sin_kt, z, g_r, g_s)


def _hy_inv_kernel(ct_ref, st_ref, yr_ref, ys_ref, x0_ref, o_ref, *, inv_scale):
    y = _dot(ct_ref[...], yr_ref[0]) + _dot(st_ref[...], ys_ref[0])
    o_ref[0] = (y * inv_scale * x0_ref[0]).astype(o_ref.dtype)


def hyena_inverse_dft(cos_tk, sin_tk, y_r, y_s, x0):
    b, l, d = y_r.shape
    tm = min(l, 512)
    tn = min(d, 512)
    return pl.pallas_call(
        functools.partial(_hy_inv_kernel, inv_scale=1.0 / l),
        grid=(l // tm, b, d // tn),
        in_specs=[
            pl.BlockSpec((tm, l), lambda i, bi, j: (i, 0)),
            pl.BlockSpec((tm, l), lambda i, bi, j: (i, 0)),
            pl.BlockSpec((1, l, tn), lambda i, bi, j: (bi, 0, j)),
            pl.BlockSpec((1, l, tn), lambda i, bi, j: (bi, 0, j)),
            pl.BlockSpec((1, tm, tn), lambda i, bi, j: (bi, i, j)),
        ],
        out_specs=pl.BlockSpec((1, tm, tn), lambda i, bi, j: (bi, i, j)),
        out_shape=jax.ShapeDtypeStruct((b, l, d), BF16),
        compiler_params=_params("parallel", "parallel", "parallel"),
        name="hyena_inverse_dft",
    )(cos_tk, sin_tk, y_r, y_s, x0)


def _tile_rows(grp):
    span = grp.l if grp.per_batch else grp.t
    return next(tm for tm in (1024, 512, 256, 128) if span % tm == 0)


def kernel(x_prompt, x_sample, cache_k, cache_v, state_ret_fwd, state_ret_bwd, state_ml_C_fwd, state_ml_n_fwd, state_ml_m_fwd, state_ml_C_bwd, state_ml_n_bwd, state_ml_m_bwd, c, c_ctx, mod_w, mod_b, norm_mix_pre, norm_mix_post, norm_ffn_pre, norm_ffn_post, mlp_w1, mlp_w2, ret_w_in, ret_decay_fwd, ret_decay_bwd, ret_gn, ret_w_out, att_w_in, att_q_gain, att_k_gain, att_w_out, ml_w_in, ml_gate_b, ml_gn, ml_w_out, hy_w_in, hy_b_in, hy_conv_w, hy_conv_b, hy_f_w1, hy_f_b1, hy_f_w2, hy_f_b2, hy_f_w3, hy_sin_freq, hy_f_bias, hy_w_out):
    bp, lp, d = x_prompt.shape
    bs, ls, _ = x_sample.shape
    depth = mod_w.shape[0]
    n_mixers = 4
    mod_rows = 16
    assert 1 + bs <= mod_rows

    grp_p = Group(bp, lp, 0, False)
    grp_s = Group(bs, ls, 1, True)
    groups = (grp_p, grp_s)

    cond = jnp.concatenate([c_ctx[None, :], c, jnp.zeros((mod_rows - 1 - bs, d), F32)], axis=0)
    mod_all = adaln_all(cond, mod_w, mod_b)

    xs = [x_prompt.reshape(grp_p.t, d), x_sample.reshape(grp_s.t, d)]
    new_k = new_v = new_rf = new_rb = None
    new_ml = None

    for i in range(depth):
        mixer = i % n_mixers
        j = i // n_mixers
        mod3 = mod_all[i].reshape(mod_rows, 1, 6 * d)
        ys = []
        for gi, grp in enumerate(groups):
            x = xs[gi]
            tm = _tile_rows(grp)
            is_prompt = gi == 0
            if mixer == 0:
                w_in = ret_w_in[j].astype(BF16)
                proj = norm_matmul(x, norm_mix_pre[i], mod3, 0, 1, w_in, None, BF16, grp, tm, PROJ_TILE_N)
                dec = jnp.stack([ret_decay_fwd[j], ret_decay_bwd[j]]).astype(F32)
                s0f = None if is_prompt else state_ret_fwd
                s0b = None if is_prompt else state_ret_bwd
                assert is_prompt or state_ret_fwd.shape[1] == 1
                res = retention_core(proj.reshape(grp.b, grp.l, -1), dec, ret_gn[j], s0f, s0b,
                                     grp.b, grp.l, emit_state=is_prompt)
                if is_prompt:
                    new_rf, new_rb = res[1], res[2]
                a = res[0].reshape(grp.t, -1)
                w_out = ret_w_out[j].astype(BF16)
            elif mixer == 1:
                w_in = att_w_in[j].astype(BF16)
                proj = norm_matmul(x, norm_mix_pre[i], mod3, 0, 1, w_in, None, F32, grp, tm, PROJ_TILE_N)
                proj = proj.reshape(grp.b, grp.l, -1)
                if is_prompt:
                    res = attention_core(proj, att_q_gain[j], att_k_gain[j], None, None, None,
                                         grp.b, grp.l, min(grp.l, 256), emit_kv=True)
                    new_k = res[1].reshape(grp.b, 1, grp.l, ATT_KV, ATT_HD)
                    new_v = res[2].reshape(grp.b, 1, grp.l, ATT_KV, ATT_HD)
                else:
                    assert cache_k.shape[1] == 1
                    ck = cache_k.reshape(grp.b, cache_k.shape[2], ATT_KV * ATT_HD)
                    cv = cache_v.reshape(grp.b, cache_v.shape[2], ATT_KV * ATT_HD)
                    res = attention_core(proj, att_q_gain[j], att_k_gain[j], _rope_tables(grp.l), ck, cv,
                                         grp.b, grp.l, min(grp.l, 256), emit_kv=False)
                a = res[0].reshape(grp.t, -1)
                w_out = att_w_out[j].astype(BF16)
            elif mixer == 2:
                hh = ML_HEADS
                n_main = ml_w_in.shape[2] - 4 * hh
                w_main = ml_w_in[j][:, :n_main].astype(BF16)
                w_gate = jnp.pad(ml_w_in[j][:, n_main:], ((0, 0), (0, LANES - 4 * hh))).astype(BF16)
                b_gate = jnp.pad(ml_gate_b[j], (0, LANES - 4 * hh))
                proj = norm_matmul(x, norm_mix_pre[i], mod3, 0, 1, w_main, None, BF16, grp, tm, PROJ_TILE_N)
                gates = norm_matmul(x, norm_mix_pre[i], mod3, 0, 1, w_gate, b_gate, F32, grp, tm, LANES)
                gates = gates[:, :4 * hh].reshape(grp.b, grp.l, 4, hh)
                n_chunks = grp.l // CHUNK
                gates_t = gates.transpose(0, 3, 2, 1).reshape(grp.b, hh, 4, n_chunks, CHUNK)
                gates_t = jnp.pad(gates_t, ((0, 0), (0, 0), (0, 0), (0, -n_chunks % 16), (0, 0)))
                state = None
                if not is_prompt:
                    assert state_ml_C_fwd.shape[1] == 1
                    state = (state_ml_C_fwd, state_ml_n_fwd, state_ml_m_fwd,
                             state_ml_C_bwd, state_ml_n_bwd, state_ml_m_bwd)
                res = mlstm_core(proj.reshape(grp.b, grp.l, -1), gates_t, ml_gn[j], state,
                                 grp.b, grp.l, emit_state=is_prompt)
                if is_prompt:
                    dqk = res[2].shape[-1]
                    new_ml = (res[1], res[2].reshape(grp.b, 1, hh, dqk), res[3].reshape(grp.b, 1, hh),
                              res[4], res[5].reshape(grp.b, 1, hh, dqk), res[6].reshape(grp.b, 1, hh))
                a = res[0].reshape(grp.t, -1)
                w_out = ml_w_out[j].astype(BF16)
            else:
                w_in = hy_w_in[j].astype(BF16)
                proj = norm_matmul(x, norm_mix_pre[i], mod3, 0, 1, w_in, hy_b_in[j], BF16, grp, tm, PROJ_TILE_N)
                z, x0 = hyena_short_conv(proj.reshape(grp.b, grp.l, 3 * d), hy_conv_w[j], hy_conv_b[j],
                                         grp.b, grp.l, d)
                f_sum, f_dif = hyena_filters(grp.l, d, hy_f_w1[j], hy_f_b1[j], hy_f_w2[j], hy_f_b2[j],
                                             hy_f_w3[j], hy_sin_freq[j])
                cos_kt, sin_kt, cos_tk, sin_tk = _dft_tables(grp.l)
                g_r, g_s = hyena_filter_spectrum(cos_kt, sin_kt, f_sum, f_dif, hy_f_bias[j])
                y_r, y_s = hyena_forward_dft(cos_kt, sin_kt, z, g_r, g_s)
                a = hyena_inverse_dft(cos_tk, sin_tk, y_r, y_s, x0).reshape(grp.t, d)
                w_out = hy_w_out[j].astype(BF16)
            x, h_mlp = matmul_resnorm(a, w_out, x, mod3, norm_mix_post[i], norm_ffn_pre[i], grp, tm)
            x = mlp_block(h_mlp, x, norm_ffn_post[i], mod3,
                          mlp_w1, mlp_w2, i, grp, tm, MLP_TILE_F)
            ys.append(x)
        xs = ys

    y_prompt = xs[0].reshape(bp, lp, d)
    y_sample = xs[1].reshape(bs, ls, d)
    return (y_prompt, y_sample, new_k, new_v, new_rf, new_rb) + tuple(new_ml)
```

```python
import functools
import math

import jax
import jax.numpy as jnp
import numpy as np
from jax import lax
from jax.experimental import pallas as pl
from jax.experimental.pallas import tpu as pltpu

F32 = jnp.float32
BF16 = jnp.bfloat16

EPS = 1e-6
CHUNK = 128
RET_CHUNK = 256
GRID_W = 64
ROPE_THETA = 10000.0

RET_HEADS = 4
ATT_HEADS = 8
ATT_KV = 2
ATT_HD = 128
ML_HEADS = 4

HY_BANDS = 16
HY_SHORT = 3
HY_FAST_DECAY = 0.3
HY_SLOW_DECAY = 1.5
HY_TARGET = 1e-2

VMEM_LIMIT_BYTES = 56 * 1024 * 1024
LANES = 128
PROJ_TILE_N = 1536
MLP_TILE_F = 1024


def _params(*sem):
    return pltpu.CompilerParams(dimension_semantics=sem, vmem_limit_bytes=VMEM_LIMIT_BYTES)


def _dot(a, b):
    return jnp.dot(a, b, preferred_element_type=F32)


def _dot_nt(a, b):
    return lax.dot_general(a, b, (((1,), (1,)), ((), ())), preferred_element_type=F32)


def _dot_tn(a, b):
    return lax.dot_general(a, b, (((0,), (0,)), ((), ())), preferred_element_type=F32)


def _rms(x, g):
    return x * lax.rsqrt(jnp.mean(x * x, axis=-1, keepdims=True) + EPS) * g


def _log_sigmoid(x):
    return jnp.minimum(x, 0.0) - jnp.log1p(jnp.exp(-jnp.abs(x)))


def _chunk_offset(ch, c):
    return ch * c if isinstance(ch, int) else pl.multiple_of(ch * c, c)


def _pipelined_scan(n_chunks, weigh_both, advance_both, fin):
    if n_chunks == 1:
        weigh_both(0)
        advance_both(0)
        fin(0)
        return
    assert n_chunks == 2 or n_chunks % 4 == 0

    def advance_pair(i):
        advance_both(i)
        advance_both(i + 1)

    def fin_pair(i):
        chunks = (i, n_chunks - 1 - i, i + 1, n_chunks - 2 - i)
        for ch in (dict.fromkeys(chunks) if isinstance(i, int) else chunks):
            fin(ch)

    def body(p, carry):
        advance_pair(2 * p)
        weigh_both(2 * p + 2)
        weigh_both(2 * p + 3)
        return carry

    def body_fin(p, carry):
        advance_pair(2 * p)
        fin_pair(2 * p)
        weigh_both(2 * p + 2)
        weigh_both(2 * p + 3)
        return carry

    pairs = n_chunks // 2
    weigh_both(0)
    weigh_both(1)
    if pairs > 1:
        lax.fori_loop(0, pairs // 2, body, 0)
        lax.fori_loop(pairs // 2, pairs - 1, body_fin, 0)
    advance_pair(n_chunks - 2)
    fin_pair(n_chunks - 2)


class Group:
    def __init__(self, b, l, row0, per_batch):
        self.b, self.l, self.row0, self.per_batch = b, l, row0, per_batch
        self.t = b * l

    def mod_spec(self, chunk, tm, d):
        row0, per_batch, l = self.row0, self.per_batch, self.l
        if per_batch:
            return pl.BlockSpec((1, 1, d), lambda i, *_: (row0 + (i * tm) // l, 0, chunk))
        return pl.BlockSpec((1, 1, d), lambda i, *_: (row0, 0, chunk))


def _adaln_kernel(c_ref, w_ref, b_ref, o_ref):
    s = jax.nn.silu(c_ref[...])
    o_ref[0] = _dot(s.astype(BF16), w_ref[0].astype(BF16)) + b_ref[0]


def adaln_all(cond, mod_w, mod_b):
    depth, d, n = mod_w.shape
    rows = cond.shape[0]
    tn = 768
    return pl.pallas_call(
        _adaln_kernel,
        grid=(depth, n // tn),
        in_specs=[
            pl.BlockSpec((rows, d), lambda l, j: (0, 0)),
            pl.BlockSpec((1, d, tn), lambda l, j: (l, 0, j)),
            pl.BlockSpec((1, 1, tn), lambda l, j: (l, 0, j)),
        ],
        out_specs=pl.BlockSpec((1, rows, tn), lambda l, j: (l, 0, j)),
        out_shape=jax.ShapeDtypeStruct((depth, rows, n), F32),
        compiler_params=_params("parallel", "parallel"),
        name="adaln",
    )(cond, mod_w, mod_b.reshape(depth, 1, n))


def _norm_mm_kernel(x_ref, g_ref, sh_ref, sc_ref, w_ref, *rest, has_bias):
    if has_bias:
        b_ref, o_ref, h_scr = rest
    else:
        o_ref, h_scr = rest

    @pl.when(pl.program_id(1) == 0)
    def _():
        y = _rms(x_ref[...], g_ref[...])
        h_scr[...] = (y * (1.0 + sc_ref[0]) + sh_ref[0]).astype(BF16)

    acc = _dot(h_scr[...], w_ref[...])
    if has_bias:
        acc = acc + b_ref[...]
    o_ref[...] = acc.astype(o_ref.dtype)


def norm_matmul(x, gain, mod3, sh_idx, sc_idx, w, bias, out_dtype, grp, tm, tn):
    t, d = x.shape
    n = w.shape[1]
    in_specs = [
        pl.BlockSpec((tm, d), lambda i, j: (i, 0)),
        pl.BlockSpec((1, d), lambda i, j: (0, 0)),
        grp.mod_spec(sh_idx, tm, d),
        grp.mod_spec(sc_idx, tm, d),
        pl.BlockSpec((d, tn), lambda i, j: (0, j)),
    ]
    args = [x, gain.reshape(1, d), mod3, mod3, w]
    if bias is not None:
        in_specs.append(pl.BlockSpec((1, tn), lambda i, j: (0, j)))
        args.append(bias.reshape(1, n))
    return pl.pallas_call(
        functools.partial(_norm_mm_kernel, has_bias=bias is not None),
        grid=(t // tm, n // tn),
        in_specs=in_specs,
        out_specs=pl.BlockSpec((tm, tn), lambda i, j: (i, j)),
        out_shape=jax.ShapeDtypeStruct((t, n), out_dtype),
        scratch_shapes=[pltpu.VMEM((tm, d), BF16)],
        compiler_params=_params("parallel", "arbitrary"),
        name="norm_matmul",
    )(*args)


def _mm_res_kernel(a_ref, w_ref, x_ref, gate_ref, pg_ref, ng_ref, sh_ref, sc_ref, o_ref, h_ref):
    y = _dot(a_ref[...], w_ref[...])
    x1 = x_ref[...] + gate_ref[0] * _rms(y, pg_ref[...])
    o_ref[...] = x1
    h_ref[...] = (_rms(x1, ng_ref[...]) * (1.0 + sc_ref[0]) + sh_ref[0]).astype(h_ref.dtype)


def matmul_resnorm(a, w, x, mod3, post_gain, next_gain, grp, tm):
    t, k = a.shape
    d = w.shape[1]
    row = pl.BlockSpec((tm, d), lambda i: (i, 0))
    vec = pl.BlockSpec((1, d), lambda i: (0, 0))
    return pl.pallas_call(
        _mm_res_kernel,
        grid=(t // tm,),
        in_specs=[
            pl.BlockSpec((tm, k), lambda i: (i, 0)),
            pl.BlockSpec((k, d), lambda i: (0, 0)),
            row,
            grp.mod_spec(2, tm, d),
            vec,
            vec,
            grp.mod_spec(3, tm, d),
            grp.mod_spec(4, tm, d),
        ],
        out_specs=[row, row],
        out_shape=[jax.ShapeDtypeStruct((t, d), F32), jax.ShapeDtypeStruct((t, d), BF16)],
        compiler_params=_params("parallel"),
        name="out_proj",
    )(a, w, x, mod3, post_gain.reshape(1, d), next_gain.reshape(1, d), mod3, mod3)


def _mlp_kernel(h_ref, x_ref, w1_ref, w2_ref, gate_ref, pg_ref, o_ref, acc_scr):
    j = pl.program_id(1)

    @pl.when(j == 0)
    def _():
        acc_scr[...] = jnp.zeros_like(acc_scr)

    u = _dot(h_ref[...], w1_ref[...].astype(BF16))
    u = jnp.square(jnp.maximum(u, 0.0)).astype(BF16)
    acc_scr[...] += _dot(u, w2_ref[...].astype(BF16))

    @pl.when(j == pl.num_programs(1) - 1)
    def _():
        o_ref[...] = x_ref[...] + gate_ref[0] * _rms(acc_scr[...], pg_ref[...])


def mlp_block(h, x, post_gain, mod3, w1, w2, layer, grp, tm, tf):
    t, d = x.shape
    f = w1.shape[2]
    return pl.pallas_call(
        _mlp_kernel,
        grid=(t // tm, f // tf),
        in_specs=[
            pl.BlockSpec((tm, d), lambda i, j: (i, 0)),
            pl.BlockSpec((tm, d), lambda i, j: (i, 0)),
            pl.BlockSpec((None, d, tf), lambda i, j: (layer, 0, j)),
            pl.BlockSpec((None, tf, d), lambda i, j: (layer, j, 0)),
            grp.mod_spec(5, tm, d),
            pl.BlockSpec((1, d), lambda i, j: (0, 0)),
        ],
        out_specs=pl.BlockSpec((tm, d), lambda i, j: (i, 0)),
        out_shape=jax.ShapeDtypeStruct((t, d), F32),
        scratch_shapes=[pltpu.VMEM((tm, d), F32)],
        compiler_params=_params("parallel", "arbitrary"),
        name="mlp",
    )(h, x, w1, w2, mod3, post_gain.reshape(1, d))


def _ret_kernel(dec_ref, q_ref, k_ref, v_ref, g_ref, gn_ref, *rest, n_chunks, has_state, emit_state):
    rest = list(rest)
    s0f_ref = s0b_ref = sf_ref = sb_ref = None
    if has_state:
        s0f_ref, s0b_ref = rest[:2]
        rest = rest[2:]
    o_ref = rest.pop(0)
    if emit_state:
        sf_ref, sb_ref = rest[:2]
        rest = rest[2:]
    st_f, st_b, of_scr, ob_scr, sw_f, sw_b, kw_f, kw_b = rest

    c = RET_CHUNK
    dk = q_ref.shape[-1]
    h = pl.program_id(1)
    lg_f = _log_sigmoid(jnp.full((1, 1), dec_ref[0, h], F32))
    lg_b = _log_sigmoid(jnp.full((1, 1), dec_ref[1, h], F32))
    ri = lax.broadcasted_iota(jnp.int32, (c, c), 0)
    ci = lax.broadcasted_iota(jnp.int32, (c, c), 1)
    rel = (ri - ci).astype(F32)
    intra_f = jnp.where(rel >= 0, jnp.exp(lg_f * jnp.maximum(rel, 0.0)), 0.0)
    intra_b = jnp.where(rel <= 0, jnp.exp(lg_b * jnp.maximum(-rel, 0.0)), 0.0)
    idx = lax.broadcasted_iota(jnp.int32, (c, 1), 0).astype(F32)
    qdec_f = jnp.exp(lg_f * (idx + 1.0))
    kdec_f = jnp.exp(lg_f * (c - 1.0 - idx))
    qdec_b = jnp.exp(lg_b * (c - idx))
    kdec_b = jnp.exp(lg_b * idx)
    cdec_f = jnp.exp(lg_f * c)
    cdec_b = jnp.exp(lg_b * c)
    q_scale = dk ** -0.5

    if has_state:
        st_f[...] = s0f_ref[0, 0, 0]
        st_b[...] = s0b_ref[0, 0, 0]
    else:
        st_f[...] = jnp.zeros_like(st_f)
        st_b[...] = jnp.zeros_like(st_b)

    def weigh(ch, intra, kdec, sw, kw_s):
        rows = pl.ds(_chunk_offset(ch, c), c)
        kc = k_ref[0, rows, :]
        sw[ch] = (_dot_nt(q_ref[0, rows, :], kc) * (intra * q_scale)).astype(BF16)
        kw_s[ch] = (kc.astype(F32) * kdec).T.astype(BF16)

    def advance(ch, st, out_scr, qdec, cdec, sw, kw_s):
        rows = pl.ds(_chunk_offset(ch, c), c)
        qc = q_ref[0, rows, :]
        vc = v_ref[0, rows, :]
        s_old = st[...]
        out_scr[rows, :] = _dot(sw[ch], vc) + _dot(qc, s_old.astype(BF16)) * (qdec * q_scale)
        st[...] = s_old * cdec + _dot(kw_s[ch], vc)

    def fin(ch):
        rows = pl.ds(_chunk_offset(ch, c), c)
        o = _rms(of_scr[rows, :] + ob_scr[rows, :], gn_ref[...])
        o_ref[0, rows, :] = (jax.nn.silu(g_ref[0, rows, :].astype(F32)) * o).astype(o_ref.dtype)

    def weigh_both(i):
        weigh(i, intra_f, kdec_f, sw_f, kw_f)
        weigh(n_chunks - 1 - i, intra_b, kdec_b, sw_b, kw_b)

    def advance_both(i):
        advance(i, st_f, of_scr, qdec_f, cdec_f, sw_f, kw_f)
        advance(n_chunks - 1 - i, st_b, ob_scr, qdec_b, cdec_b, sw_b, kw_b)

    _pipelined_scan(n_chunks, weigh_both, advance_both, fin)

    if emit_state:
        sf_ref[0, 0, 0] = st_f[...]
        sb_ref[0, 0, 0] = st_b[...]


def retention_core(proj, dec, gn, s0f, s0b, b, l, emit_state):
    hh = RET_HEADS
    width = proj.shape[-1]
    dk = width // (6 * hh)
    dv = 2 * dk
    has_state = s0f is not None
    n_chunks = l // RET_CHUNK
    in_specs = [
        pl.BlockSpec(memory_space=pltpu.SMEM),
        pl.BlockSpec((1, l, dk), lambda bi, h: (bi, 0, h)),
        pl.BlockSpec((1, l, dk), lambda bi, h: (bi, 0, hh + h)),
        pl.BlockSpec((1, l, dv), lambda bi, h: (bi, 0, hh + h)),
        pl.BlockSpec((1, l, dv), lambda bi, h: (bi, 0, 2 * hh + h)),
        pl.BlockSpec((1, dv), lambda bi, h: (0, h)),
    ]
    args = [dec, proj, proj, proj, proj, gn.reshape(1, hh * dv)]
    if has_state:
        st_spec = pl.BlockSpec((1, 1, 1, dk, dv), lambda bi, h: (bi, 0, h, 0, 0))
        in_specs += [st_spec, st_spec]
        args += [s0f, s0b]
    out_specs = [pl.BlockSpec((1, l, dv), lambda bi, h: (bi, 0, h))]
    out_shape = [jax.ShapeDtypeStruct((b, l, hh * dv), BF16)]
    if emit_state:
        so_spec = pl.BlockSpec((1, 1, 1, dk, dv), lambda bi, h: (bi, 0, h, 0, 0))
        out_specs += [so_spec, so_spec]
        out_shape += [jax.ShapeDtypeStruct((b, 1, hh, dk, dv), F32)] * 2
    return pl.pallas_call(
        functools.partial(_ret_kernel, n_chunks=n_chunks, has_state=has_state, emit_state=emit_state),
        grid=(b, hh),
        in_specs=in_specs,
        out_specs=out_specs,
        out_shape=out_shape,
        scratch_shapes=[pltpu.VMEM((dk, dv), F32), pltpu.VMEM((dk, dv), F32),
                        pltpu.VMEM((l, dv), F32), pltpu.VMEM((l, dv), F32)]
                       + [pltpu.VMEM((n_chunks, RET_CHUNK, RET_CHUNK), BF16)] * 2
                       + [pltpu.VMEM((n_chunks, dk, RET_CHUNK), BF16)] * 2,
        compiler_params=_params("parallel", "parallel"),
        name="retention",
    )(*args)


def _rope_rot(x, cos_t, sin_t):
    lane = lax.broadcasted_iota(jnp.int32, x.shape, x.ndim - 1)
    nxt = pltpu.roll(x, LANES - 1, axis=x.ndim - 1)
    prv = pltpu.roll(x, 1, axis=x.ndim - 1)
    swapped = jnp.where(jnp.bitwise_and(lane, 1) == 0, nxt, prv)
    return x * cos_t + swapped * sin_t


def _att_kernel(q_ref, k_ref, v_ref, qg_ref, kg_ref, *rest, rope, has_cache, emit_kv, groups):
    rest = list(rest)
    cosq_ref = sinq_ref = cosk_ref = sink_ref = ck_ref = cv_ref = kn_ref = vo_ref = None
    if rope:
        cosq_ref, sinq_ref, cosk_ref, sink_ref = rest[:4]
        rest = rest[4:]
    if has_cache:
        ck_ref, cv_ref = rest[:2]
        rest = rest[2:]
    o_ref = rest.pop(0)
    if emit_kv:
        kn_ref, vo_ref = rest[:2]
        rest = rest[2:]
    k_scr, v_scr = rest
    hd = ATT_HD
    exp2_scale = hd ** -0.5 * math.log2(math.e)

    @pl.when(pl.program_id(2) == 0)
    def _():
        kn = _rms(k_ref[0], kg_ref[...])
        if emit_kv:
            kn_ref[0] = kn
            vo_ref[0] = v_ref[0]
        if rope:
            kn = _rope_rot(kn, cosk_ref[...], sink_ref[...])
        k_scr[...] = kn.astype(BF16)
        v_scr[:, :hd] = v_ref[0].astype(BF16)
        v_scr[:, hd:] = jnp.ones((v_scr.shape[0], hd), BF16)

    if has_cache:
        ck = ck_ref[0].astype(BF16)
        cv = jnp.concatenate([cv_ref[0].astype(BF16), jnp.ones((cv_ref.shape[1], hd), BF16)], axis=1)

    def scores(g):
        qn = _rms(q_ref[0, :, g * hd:(g + 1) * hd], qg_ref[...])
        if rope:
            qn = _rope_rot(qn, cosq_ref[...], sinq_ref[...])
        qb = qn.astype(BF16)
        return _dot_nt(qb, k_scr[...]), (_dot_nt(qb, ck) if has_cache else None)

    nxt = scores(0)
    for g in range(groups):
        s1, s2 = nxt
        if g + 1 < groups:
            nxt = scores(g + 1)
        m = jnp.max(s1, axis=-1, keepdims=True)
        if has_cache:
            m = jnp.maximum(m, jnp.max(s2, axis=-1, keepdims=True))
        p1 = jnp.exp2((s1 - m) * exp2_scale)
        nd = _dot(p1.astype(BF16), v_scr[...])
        if has_cache:
            p2 = jnp.exp2((s2 - m) * exp2_scale)
            nd = nd + _dot(p2.astype(BF16), cv)
        o_ref[0, :, g * hd:(g + 1) * hd] = (nd[:, :hd] * (1.0 / nd[:, hd:])).astype(o_ref.dtype)


def attention_core(proj, q_gain, k_gain, rope_tabs, cache_k, cache_v, b, l, tq, emit_kv):
    hd, kv, heads = ATT_HD, ATT_KV, ATT_HEADS
    groups = heads // kv
    rope = rope_tabs is not None
    has_cache = cache_k is not None
    in_specs = [
        pl.BlockSpec((1, tq, groups * hd), lambda bi, kh, qi: (bi, qi, kh)),
        pl.BlockSpec((1, l, hd), lambda bi, kh, qi: (bi, 0, heads + kh)),
        pl.BlockSpec((1, l, hd), lambda bi, kh, qi: (bi, 0, heads + kv + kh)),
        pl.BlockSpec((1, hd), lambda bi, kh, qi: (0, 0)),
        pl.BlockSpec((1, hd), lambda bi, kh, qi: (0, 0)),
    ]
    args = [proj, proj, proj, q_gain.reshape(1, hd), k_gain.reshape(1, hd)]
    if rope:
        cos_t, sin_t = rope_tabs
        in_specs += [
            pl.BlockSpec((tq, hd), lambda bi, kh, qi: (qi, 0)),
            pl.BlockSpec((tq, hd), lambda bi, kh, qi: (qi, 0)),
            pl.BlockSpec((l, hd), lambda bi, kh, qi: (0, 0)),
            pl.BlockSpec((l, hd), lambda bi, kh, qi: (0, 0)),
        ]
        args += [cos_t, sin_t, cos_t, sin_t]
    if has_cache:
        past = cache_k.shape[1]
        c_spec = pl.BlockSpec((1, past, hd), lambda bi, kh, qi: (bi, 0, kh))
        in_specs += [c_spec, c_spec]
        args += [cache_k, cache_v]
    out_specs = [pl.BlockSpec((1, tq, groups * hd), lambda bi, kh, qi: (bi, qi, kh))]
    out_shape = [jax.ShapeDtypeStruct((b, l, heads * hd), BF16)]
    if emit_kv:
        kv_spec = pl.BlockSpec((1, l, hd), lambda bi, kh, qi: (bi, 0, kh))
        out_specs += [kv_spec, kv_spec]
        out_shape += [jax.ShapeDtypeStruct((b, l, kv * hd), F32)] * 2
    return pl.pallas_call(
        functools.partial(_att_kernel, rope=rope, has_cache=has_cache, emit_kv=emit_kv, groups=groups),
        grid=(b, kv, l // tq),
        in_specs=in_specs,
        out_specs=out_specs,
        out_shape=out_shape,
        scratch_shapes=[pltpu.VMEM((l, hd), BF16), pltpu.VMEM((l, 2 * hd), BF16)],
        compiler_params=_params("parallel", "parallel", "arbitrary"),
        name="attention",
    )(*args)


def _rope_tables(l):
    rows = l // GRID_W
    row = jnp.repeat(jnp.arange(rows, dtype=F32), GRID_W)
    col = jnp.tile(jnp.arange(GRID_W, dtype=F32), rows)
    half = ATT_HD // 2
    inv = ROPE_THETA ** (-jnp.arange(0, half, 2, dtype=F32) / half)
    ang = jnp.concatenate([row[:, None] * inv, col[:, None] * inv], axis=-1)
    cos_t = jnp.repeat(jnp.cos(ang), 2, axis=-1)
    sin_h = jnp.sin(ang)
    sin_t = jnp.stack([-sin_h, sin_h], axis=-1).reshape(l, ATT_HD)
    return cos_t, sin_t


def _ml_kernel(q_ref, k_ref, v_ref, og_ref, gt_ref, gn_ref, *rest, n_chunks, has_state, emit_state):
    rest = list(rest)
    c0f_ref = n0f_ref = m0f_ref = c0b_ref = n0b_ref = m0b_ref = None
    if has_state:
        c0f_ref, n0f_ref, m0f_ref, c0b_ref, n0b_ref, m0b_ref = rest[:6]
        rest = rest[6:]
    o_ref = rest.pop(0)
    outs = None
    if emit_state:
        outs = rest[:6]
        rest = rest[6:]
    cm_f, cm_b, nv_f, nv_b, hf_scr, hb_scr, fl_f, fl_b, br_f, br_b, mo_f, mo_b, mn_f, mn_b = rest[:14]
    sw_f, sw_b, kw_f, kw_b = rest[14:18]
    vt_f, vt_b = rest[18:23], rest[23:28]

    c = CHUNK
    dqk = q_ref.shape[-1]
    k_scale = dqk ** -0.5
    ri = lax.broadcasted_iota(jnp.int32, (c, c), 0)
    ci = lax.broadcasted_iota(jnp.int32, (c, c), 1)
    mask_f = ci <= ri
    mask_b = ci >= ri
    assert c == LANES and dqk == LANES
    dv_tiles = v_ref.shape[-1] // LANES

    def bf3(m):
        m = jnp.where(m, 1.0, 0.0).astype(BF16)
        return jnp.concatenate([m, m, m], axis=1), jnp.concatenate([m, m, m], axis=0)

    mf3, mf3_t = bf3(mask_f)
    mb3, mb3_t = bf3(mask_b)
    eye3, _ = bf3(ci == ri)

    def split3(x):
        hi = x.astype(BF16)
        r1 = x - hi.astype(F32)
        mid = r1.astype(BF16)
        lo = (r1 - mid.astype(F32)).astype(BF16)
        return jnp.concatenate([hi, mid, lo], axis=1)

    def wide(x):
        return jnp.concatenate([x] * dv_tiles, axis=1)

    if has_state:
        cm_f[...] = c0f_ref[0, 0, 0]
        cm_b[...] = c0b_ref[0, 0, 0]
        nv_f[...] = n0f_ref[0, 0]
        nv_b[...] = n0b_ref[0, 0]
        m0_f = jnp.broadcast_to(m0f_ref[0, 0], (1, LANES))
        m0_b = jnp.broadcast_to(m0b_ref[0, 0], (1, LANES))
    else:
        for r in (cm_f, cm_b, nv_f, nv_b):
            r[...] = jnp.zeros_like(r)
        m0_f = m0_b = jnp.zeros((1, LANES), F32)

    def gate_pass(row_i, row_f, m3_t, m0, fl, br, mo, mn, order):
        f_all = _log_sigmoid(gt_ref[0, 0, row_f])
        fl[...] = f_all
        b_all = _dot(split3(f_all), m3_t)
        br[...] = b_all
        b_end = jnp.sum(f_all, axis=-1, keepdims=True)
        w_max = jnp.max(b_end - b_all + gt_ref[0, 0, row_i], axis=-1, keepdims=True)
        m = m0
        for r in order:
            mo[r:r + 1, :] = m
            m = jnp.maximum(b_end[r:r + 1, :] + m, w_max[r:r + 1, :])
            mn[r:r + 1, :] = m
        return m

    m_last_f = gate_pass(0, 1, mb3_t, m0_f, fl_f, br_f, mo_f, mn_f, range(n_chunks))
    m_last_b = gate_pass(2, 3, mf3_t, m0_b, fl_b, br_b, mo_b, mn_b, reversed(range(n_chunks)))

    def weigh(ch, mask, m3, fl, br, mo, mn, row_i, sw, kw_s, wx_s, rs_s, em_s, cd_s, ks_s):
        rows = pl.ds(_chunk_offset(ch, c), c)
        one = pl.ds(ch, 1)
        qc = q_ref[0, rows, :]
        kc = k_ref[0, rows, :]
        i_row = gt_ref[0, 0, row_i, one, :]
        f_row = fl[one, :]
        b_row = br[one, :]
        m_old = mo[one, :]
        m_new = mn[one, :]
        b_q = _dot_nt(m3, jnp.broadcast_to(split3(f_row), (LANES, 3 * c)))
        i_q = _dot_nt(eye3, jnp.broadcast_to(split3(i_row), (LANES, 3 * c)))
        dlog = jnp.where(mask, b_q - b_row + i_row, -jnp.inf)
        inter = b_q + m_old
        m_q = jnp.maximum(inter, jnp.max(dlog, axis=-1, keepdims=True))
        s = _dot_nt(qc, kc) * (jnp.exp(dlog - m_q) * k_scale)
        sw[ch] = s.astype(BF16)
        rs_s[rows, :] = jnp.broadcast_to(jnp.sum(s, axis=-1, keepdims=True), (c, LANES))
        wx_s[rows, :] = jnp.exp(inter - m_q)
        em_s[rows, :] = jnp.exp(-m_q)
        b_end = jnp.sum(f_row, axis=-1, keepdims=True)
        wlog = b_end - b_q + i_q
        cd_s[one, :] = jnp.exp(b_end + m_old - m_new)
        kw = kc.astype(F32) * (jnp.exp(wlog - m_new) * k_scale)
        kw_s[ch] = kw.T.astype(BF16)
        ks_s[one, :] = jnp.sum(kw, axis=0, keepdims=True)

    def advance(ch, cm, nv, out_scr, sw, kw_s, wx_s, rs_s, em_s, cd_s, ks_s):
        rows = pl.ds(_chunk_offset(ch, c), c)
        one = pl.ds(ch, 1)
        qc = q_ref[0, rows, :]
        vc = v_ref[0, rows, :]
        c_old = cm[...]
        n_old = nv[...]
        w_x = wx_s[rows, :]
        num = _dot(sw[ch], vc) + _dot(qc, c_old.astype(BF16)) * wide(w_x)
        qn = _dot_nt(qc, jnp.broadcast_to(n_old, (LANES, dqk)).astype(BF16))
        den = jnp.maximum(jnp.abs(rs_s[rows, :] + qn * w_x), em_s[rows, :])
        out_scr[rows, :] = num * wide(1.0 / den)
        carry_dec = cd_s[one, :]
        cm[...] = c_old * wide(carry_dec) + _dot(kw_s[ch], vc)
        nv[...] = n_old * carry_dec + ks_s[one, :]

    def fin(ch):
        rows = pl.ds(_chunk_offset(ch, c), c)
        hn = _rms(hf_scr[rows, :] + hb_scr[rows, :], gn_ref[...])
        o_ref[0, rows, :] = (jax.nn.sigmoid(og_ref[0, rows, :].astype(F32)) * hn).astype(o_ref.dtype)

    tmp_f = (sw_f, kw_f) + tuple(vt_f)
    tmp_b = (sw_b, kw_b) + tuple(vt_b)

    def weigh_both(i):
        weigh(i, mask_f, mf3, fl_f, br_f, mo_f, mn_f, 0, *tmp_f)
        weigh(n_chunks - 1 - i, mask_b, mb3, fl_b, br_b, mo_b, mn_b, 2, *tmp_b)

    def advance_both(i):
        advance(i, cm_f, nv_f, hf_scr, *tmp_f)
        advance(n_chunks - 1 - i, cm_b, nv_b, hb_scr, *tmp_b)

    _pipelined_scan(n_chunks, weigh_both, advance_both, fin)

    if emit_state:
        for dst, src in zip(outs[0::3], (cm_f, cm_b)):
            dst[0, 0, 0] = src[...]
        for dst, src in zip(outs[1::3], (nv_f, nv_b)):
            dst[0, 0] = src[...]
        for dst, m_last in zip(outs[2::3], (m_last_f, m_last_b)):
            dst[0, 0] = m_last[:, 0:1]


def mlstm_core(proj, gates_t, gn, state, b, l, emit_state):
    hh = ML_HEADS
    width = proj.shape[-1]
    dqk = width // (6 * hh)
    dv = 2 * dqk
    has_state = state is not None
    n_chunks = l // CHUNK
    chunk_rows = gates_t.shape[3]
    in_specs = [
        pl.BlockSpec((1, l, dqk), lambda bi, h: (bi, 0, h)),
        pl.BlockSpec((1, l, dqk), lambda bi, h: (bi, 0, hh + h)),
        pl.BlockSpec((1, l, dv), lambda bi, h: (bi, 0, hh + h)),
        pl.BlockSpec((1, l, dv), lambda bi, h: (bi, 0, 2 * hh + h)),
        pl.BlockSpec((1, 1, 4, chunk_rows, CHUNK), lambda bi, h: (bi, h, 0, 0, 0)),
        pl.BlockSpec((1, dv), lambda bi, h: (0, h)),
    ]
    args = [proj, proj, proj, proj, gates_t, gn.reshape(1, hh * dv)]
    c_spec_in = pl.BlockSpec((1, 1, 1, dqk, dv), lambda bi, h: (bi, 0, h, 0, 0))
    n_spec = pl.BlockSpec((1, 1, 1, dqk), lambda bi, h: (bi, h, 0, 0))
    m_spec = pl.BlockSpec((1, 1, 1, 1), lambda bi, h: (bi, h, 0, 0))
    if has_state:
        cf, nf, mf, cb, nb, mb = state
        in_specs += [c_spec_in, n_spec, m_spec, c_spec_in, n_spec, m_spec]
        args += [cf, nf.reshape(b, hh, 1, dqk), mf.reshape(b, hh, 1, 1),
                 cb, nb.reshape(b, hh, 1, dqk), mb.reshape(b, hh, 1, 1)]
    out_specs = [pl.BlockSpec((1, l, dv), lambda bi, h: (bi, 0, h))]
    out_shape = [jax.ShapeDtypeStruct((b, l, hh * dv), BF16)]
    if emit_state:
        out_specs += [c_spec_in, n_spec, m_spec] * 2
        out_shape += [jax.ShapeDtypeStruct((b, 1, hh, dqk, dv), F32),
                      jax.ShapeDtypeStruct((b, hh, 1, dqk), F32),
                      jax.ShapeDtypeStruct((b, hh, 1, 1), F32)] * 2
    return pl.pallas_call(
        functools.partial(_ml_kernel, n_chunks=n_chunks, has_state=has_state, emit_state=emit_state),
        grid=(b, hh),
        in_specs=in_specs,
        out_specs=out_specs,
        out_shape=out_shape,
        scratch_shapes=[pltpu.VMEM((dqk, dv), F32), pltpu.VMEM((dqk, dv), F32),
                        pltpu.VMEM((1, dqk), F32), pltpu.VMEM((1, dqk), F32),
                        pltpu.VMEM((l, dv), F32), pltpu.VMEM((l, dv), F32)]
                       + [pltpu.VMEM((chunk_rows, CHUNK), F32)] * 4 + [pltpu.VMEM((chunk_rows, LANES), F32)] * 4
                       + [pltpu.VMEM((n_chunks, CHUNK, CHUNK), BF16)] * 2 + [pltpu.VMEM((n_chunks, dqk, CHUNK), BF16)] * 2
                       + [pltpu.VMEM((l, LANES), F32)] * 3 + [pltpu.VMEM((chunk_rows, LANES), F32)] * 2
                       + [pltpu.VMEM((l, LANES), F32)] * 3 + [pltpu.VMEM((chunk_rows, LANES), F32)] * 2,
        compiler_params=_params("parallel", "parallel"),
        name="mlstm",
    )(*args)


def _hy_filter_kernel(feat_ref, w1_ref, b1_ref, w2_ref, b2_ref, w3_ref, fr_ref, win_ref, sum_ref, dif_ref):
    d = win_ref.shape[-1]
    z = jnp.sin(fr_ref[0:1, :] * (_dot(feat_ref[...].astype(BF16), w1_ref[...].astype(BF16)) + b1_ref[...]))
    z = jnp.sin(fr_ref[1:2, :] * (_dot(z.astype(BF16), w2_ref[...].astype(BF16)) + b2_ref[...]))
    filt = _dot(z.astype(BF16), w3_ref[...].astype(BF16))
    win = win_ref[...]
    ff = filt[:, :d] * win
    fb = filt[:, d:] * win
    sum_ref[...] = (ff + fb).astype(sum_ref.dtype)
    dif_ref[...] = (ff - fb).astype(dif_ref.dtype)


def hyena_filters(l, d, w1, b1, w2, b2, w3, freq):
    t = jnp.linspace(0.0, 1.0, l, dtype=F32)[:, None]
    pos = jnp.arange(l, dtype=F32)[:, None]
    bands = jnp.linspace(1e-4, HY_BANDS - 1, HY_BANDS, dtype=F32)
    ang = 2.0 * math.pi * pos * bands / l
    feats = jnp.concatenate([t, jnp.cos(ang), -jnp.sin(ang)], axis=-1)
    emb = feats.shape[1]
    feats = jnp.pad(feats, ((0, 0), (0, LANES - emb)))
    w1p = jnp.pad(w1, ((0, LANES - emb), (0, 0)))
    deltas = jnp.abs(jnp.linspace(math.log(HY_TARGET) / HY_SLOW_DECAY,
                                  math.log(HY_TARGET) / HY_FAST_DECAY, d, dtype=F32))
    window = jnp.exp(-t * deltas)
    hid = w2.shape[0]
    tl = min(l, 512)
    full = lambda shp: pl.BlockSpec(shp, lambda i: (0,) * len(shp))
    return pl.pallas_call(
        _hy_filter_kernel,
        grid=(l // tl,),
        in_specs=[
            pl.BlockSpec((tl, LANES), lambda i: (i, 0)),
            full((LANES, hid)), full((1, hid)), full((hid, hid)), full((1, hid)), full((hid, 2 * d)),
            full((2, hid)),
            pl.BlockSpec((tl, d), lambda i: (i, 0)),
        ],
        out_specs=[pl.BlockSpec((tl, d), lambda i: (i, 0))] * 2,
        out_shape=[jax.ShapeDtypeStruct((l, d), BF16)] * 2,
        compiler_params=_params("parallel"),
        name="hyena_filter",
    )(feats, w1p, b1.reshape(1, hid), w2, b2.reshape(1, hid), w3, freq, window)


def _dft_tables(l):
    n = 2 * l
    blk = 64
    assert l % blk == 0

    def trig(step, count):
        k = lax.broadcasted_iota(jnp.int32, (l, count), 0)
        t = lax.broadcasted_iota(jnp.int32, (l, count), 1)
        ang = (((2 * k + 1) * step * t) % (2 * n)).astype(F32) * (math.pi / n)
        return jnp.cos(ang), jnp.sin(ang)

    c1, s1 = trig(blk, l // blk)
    c0, s0 = trig(1, blk)
    cos_kt = (c1[:, :, None] * c0[:, None, :] - s1[:, :, None] * s0[:, None, :]).reshape(l, l).astype(BF16)
    sin_kt = (s1[:, :, None] * c0[:, None, :] + c1[:, :, None] * s0[:, None, :]).reshape(l, l).astype(BF16)
    c1t, s1t, c0t, s0t = c1.T, s1.T, c0.T, s0.T
    cos_tk = (c1t[:, None, :] * c0t[None, :, :] - s1t[:, None, :] * s0t[None, :, :]).reshape(l, l).astype(BF16)
    sin_tk = (s1t[:, None, :] * c0t[None, :, :] + c1t[:, None, :] * s0t[None, :, :]).reshape(l, l).astype(BF16)
    return cos_kt, sin_kt, cos_tk, sin_tk


def _hy_spec_kernel(c_ref, s_ref, fs_ref, fd_ref, bias_ref, gr_ref, gs_ref):
    gr_ref[...] = _dot(c_ref[...], fs_ref[...]) + bias_ref[...]
    gs_ref[...] = _dot(s_ref[...], fd_ref[...])


def hyena_filter_spectrum(cos_kt, sin_kt, f_sum, f_dif, f_bias):
    l, d = f_sum.shape
    tm = min(l, 512)
    tn = min(d, 512)
    return pl.pallas_call(
        _hy_spec_kernel,
        grid=(l // tm, d // tn),
        in_specs=[
            pl.BlockSpec((tm, l), lambda i, j: (i, 0)),
            pl.BlockSpec((tm, l), lambda i, j: (i, 0)),
            pl.BlockSpec((l, tn), lambda i, j: (0, j)),
            pl.BlockSpec((l, tn), lambda i, j: (0, j)),
            pl.BlockSpec((1, tn), lambda i, j: (0, j)),
        ],
        out_specs=[pl.BlockSpec((tm, tn), lambda i, j: (i, j))] * 2,
        out_shape=[jax.ShapeDtypeStruct((l, d), F32)] * 2,
        compiler_params=_params("parallel", "parallel"),
        name="hyena_filter_spectrum",
    )(cos_kt, sin_kt, f_sum, f_dif, f_bias.reshape(1, d))


def _hy_conv_kernel(p0_ref, p1_ref, pv_ref, w0_ref, w1_ref, wv_ref, b0_ref, b1_ref, bv_ref, z_ref, x0_ref):
    l = p0_ref.shape[1]
    t = lax.broadcasted_iota(jnp.int32, (l, 1), 0)

    def conv(p_ref, w_ref, b_ref):
        p = p_ref[0].astype(F32)
        prev = jnp.where(t == 0, 0.0, pltpu.roll(p, 1, axis=0))
        nxt = jnp.where(t == l - 1, 0.0, pltpu.roll(p, l - 1, axis=0))
        return b_ref[...] + prev * w_ref[0:1, :] + p * w_ref[1:2, :] + nxt * w_ref[2:3, :]

    x0_ref[0] = conv(p0_ref, w0_ref, b0_ref).astype(x0_ref.dtype)
    z_ref[0] = (conv(pv_ref, wv_ref, bv_ref) * conv(p1_ref, w1_ref, b1_ref)).astype(z_ref.dtype)


def hyena_short_conv(proj, conv_w, conv_b, b, l, d):
    tc = min(d, 512)
    nc = d // tc
    p_spec = lambda off: pl.BlockSpec((1, l, tc), lambda bi, j: (bi, 0, off * nc + j))
    w_spec = lambda off: pl.BlockSpec((HY_SHORT, tc), lambda bi, j: (0, off * nc + j))
    b_spec = lambda off: pl.BlockSpec((1, tc), lambda bi, j: (0, off * nc + j))
    cb = conv_b.reshape(1, 3 * d)
    return pl.pallas_call(
        _hy_conv_kernel,
        grid=(b, nc),
        in_specs=[p_spec(0), p_spec(1), p_spec(2), w_spec(0), w_spec(1), w_spec(2),
                  b_spec(0), b_spec(1), b_spec(2)],
        out_specs=[pl.BlockSpec((1, l, tc), lambda bi, j: (bi, 0, j))] * 2,
        out_shape=[jax.ShapeDtypeStruct((b, l, d), BF16), jax.ShapeDtypeStruct((b, l, d), BF16)],
        compiler_params=_params("parallel", "parallel"),
        name="hyena_short_conv",
    )(proj, proj, proj, conv_w, conv_w, conv_w, cb, cb, cb)


def _hy_fwd_kernel(c_ref, s_ref, z_ref, gr_ref, gs_ref, yr_ref, ys_ref):
    z = z_ref[0]
    zr = _dot(c_ref[...], z)
    zs = _dot(s_ref[...], z)
    gr = gr_ref[...]
    gs = gs_ref[...]
    yr_ref[0] = (zr * gr - zs * gs).astype(yr_ref.dtype)
    ys_ref[0] = (zr * gs + zs * gr).astype(ys_ref.dtype)


def hyena_forward_dft(cos_kt, sin_kt, z, g_r, g_s):
    b, l, d = z.shape
    tm = min(l, 512)
    tn = min(d, 512)
    return pl.pallas_call(
        _hy_fwd_kernel,
        grid=(l // tm, b, d // tn),
        in_specs=[
            pl.BlockSpec((tm, l), lambda i, bi, j: (i, 0)),
            pl.BlockSpec((tm, l), lambda i, bi, j: (i, 0)),
            pl.BlockSpec((1, l, tn), lambda i, bi, j: (bi, 0, j)),
            pl.BlockSpec((tm, tn), lambda i, bi, j: (i, j)),
            pl.BlockSpec((tm, tn), lambda i, bi, j: (i, j)),
        ],
        out_specs=[pl.BlockSpec((1, tm, tn), lambda i, bi, j: (bi, i, j))] * 2,
        out_shape=[jax.ShapeDtypeStruct((b, l, d), BF16)] * 2,
        compiler_params=_params("parallel", "parallel", "parallel"),
        name="hyena_forward_dft",
    )(cos_kt, sin_kt, z, g_r, g_s)


def _hy_inv_kernel(ct_ref, st_ref, yr_ref, ys_ref, x0_ref, o_ref, *, inv_scale):
    y = _dot(ct_ref[...], yr_ref[0]) + _dot(st_ref[...], ys_ref[0])
    o_ref[0] = (y * inv_scale * x0_ref[0]).astype(o_ref.dtype)


def hyena_inverse_dft(cos_tk, sin_tk, y_r, y_s, x0):
    b, l, d = y_r.shape
    tm = min(l, 512)
    tn = min(d, 512)
    return pl.pallas_call(
        functools.partial(_hy_inv_kernel, inv_scale=1.0 / l),
        grid=(l // tm, b, d // tn),
        in_specs=[
            pl.BlockSpec((tm, l), lambda i, bi, j: (i, 0)),
            pl.BlockSpec((tm, l), lambda i, bi, j: (i, 0)),
            pl.BlockSpec((1, l, tn), lambda i, bi, j: (bi, 0, j)),
            pl.BlockSpec((1, l, tn), lambda i, bi, j: (bi, 0, j)),
            pl.BlockSpec((1, tm, tn), lambda i, bi, j: (bi, i, j)),
        ],
        out_specs=pl.BlockSpec((1, tm, tn), lambda i, bi, j: (bi, i, j)),
        out_shape=jax.ShapeDtypeStruct((b, l, d), BF16),
        compiler_params=_params("parallel", "parallel", "parallel"),
        name="hyena_inverse_dft",
    )(cos_tk, sin_tk, y_r, y_s, x0)


def _tile_rows(grp):
    span = grp.l if grp.per_batch else grp.t
    return next(tm for tm in (1024, 512, 256, 128) if span % tm == 0)


def kernel(x_prompt, x_sample, cache_k, cache_v, state_ret_fwd, state_ret_bwd, state_ml_C_fwd, state_ml_n_fwd, state_ml_m_fwd, state_ml_C_bwd, state_ml_n_bwd, state_ml_m_bwd, c, c_ctx, mod_w, mod_b, norm_mix_pre, norm_mix_post, norm_ffn_pre, norm_ffn_post, mlp_w1, mlp_w2, ret_w_in, ret_decay_fwd, ret_decay_bwd, ret_gn, ret_w_out, att_w_in, att_q_gain, att_k_gain, att_w_out, ml_w_in, ml_gate_b, ml_gn, ml_w_out, hy_w_in, hy_b_in, hy_conv_w, hy_conv_b, hy_f_w1, hy_f_b1, hy_f_w2, hy_f_b2, hy_f_w3, hy_sin_freq, hy_f_bias, hy_w_out):
    bp, lp, d = x_prompt.shape
    bs, ls, _ = x_sample.shape
    depth = mod_w.shape[0]
    n_mixers = 4
    mod_rows = 16
    assert 1 + bs <= mod_rows

    grp_p = Group(bp, lp, 0, False)
    grp_s = Group(bs, ls, 1, True)
    groups = (grp_p, grp_s)

    cond = jnp.concatenate([c_ctx[None, :], c, jnp.zeros((mod_rows - 1 - bs, d), F32)], axis=0)
    mod_all = adaln_all(cond, mod_w, mod_b)

    xs = [x_prompt.reshape(grp_p.t, d), x_sample.reshape(grp_s.t, d)]
    new_k = new_v = new_rf = new_rb = None
    new_ml = None

    for i in range(depth):
        mixer = i % n_mixers
        j = i // n_mixers
        mod3 = mod_all[i].reshape(mod_rows, 1, 6 * d)
        ys = []
        for gi, grp in enumerate(groups):
            x = xs[gi]
            tm = _tile_rows(grp)
            is_prompt = gi == 0
            if mixer == 0:
                w_in = ret_w_in[j].astype(BF16)
                proj = norm_matmul(x, norm_mix_pre[i], mod3, 0, 1, w_in, None, BF16, grp, tm, PROJ_TILE_N)
                dec = jnp.stack([ret_decay_fwd[j], ret_decay_bwd[j]]).astype(F32)
                s0f = None if is_prompt else state_ret_fwd
                s0b = None if is_prompt else state_ret_bwd
                assert is_prompt or state_ret_fwd.shape[1] == 1
                res = retention_core(proj.reshape(grp.b, grp.l, -1), dec, ret_gn[j], s0f, s0b,
                                     grp.b, grp.l, emit_state=is_prompt)
                if is_prompt:
                    new_rf, new_rb = res[1], res[2]
                a = res[0].reshape(grp.t, -1)
                w_out = ret_w_out[j].astype(BF16)
            elif mixer == 1:
                w_in = att_w_in[j].astype(BF16)
                proj = norm_matmul(x, norm_mix_pre[i], mod3, 0, 1, w_in, None, F32, grp, tm, PROJ_TILE_N)
                proj = proj.reshape(grp.b, grp.l, -1)
                if is_prompt:
                    res = attention_core(proj, att_q_gain[j], att_k_gain[j], None, None, None,
                                         grp.b, grp.l, min(grp.l, 256), emit_kv=True)
                    new_k = res[1].reshape(grp.b, 1, grp.l, ATT_KV, ATT_HD)
                    new_v = res[2].reshape(grp.b, 1, grp.l, ATT_KV, ATT_HD)
                else:
                    assert cache_k.shape[1] == 1
                    ck = cache_k.reshape(grp.b, cache_k.shape[2], ATT_KV * ATT_HD)
                    cv = cache_v.reshape(grp.b, cache_v.shape[2], ATT_KV * ATT_HD)
                    res = attention_core(proj, att_q_gain[j], att_k_gain[j], _rope_tables(grp.l), ck, cv,
                                         grp.b, grp.l, min(grp.l, 256), emit_kv=False)
                a = res[0].reshape(grp.t, -1)
                w_out = att_w_out[j].astype(BF16)
            elif mixer == 2:
                hh = ML_HEADS
                n_main = ml_w_in.shape[2] - 4 * hh
                w_main = ml_w_in[j][:, :n_main].astype(BF16)
                w_gate = jnp.pad(ml_w_in[j][:, n_main:], ((0, 0), (0, LANES - 4 * hh))).astype(BF16)
                b_gate = jnp.pad(ml_gate_b[j], (0, LANES - 4 * hh))
                proj = norm_matmul(x, norm_mix_pre[i], mod3, 0, 1, w_main, None, BF16, grp, tm, PROJ_TILE_N)
                gates = norm_matmul(x, norm_mix_pre[i], mod3, 0, 1, w_gate, b_gate, F32, grp, tm, LANES)
                gates = gates[:, :4 * hh].reshape(grp.b, grp.l, 4, hh)
                n_chunks = grp.l // CHUNK
                gates_t = gates.transpose(0, 3, 2, 1).reshape(grp.b, hh, 4, n_chunks, CHUNK)
                gates_t = jnp.pad(gates_t, ((0, 0), (0, 0), (0, 0), (0, -n_chunks % 16), (0, 0)))
                state = None
                if not is_prompt:
                    assert state_ml_C_fwd.shape[1] == 1
                    state = (state_ml_C_fwd, state_ml_n_fwd, state_ml_m_fwd,
                             state_ml_C_bwd, state_ml_n_bwd, state_ml_m_bwd)
                res = mlstm_core(proj.reshape(grp.b, grp.l, -1), gates_t, ml_gn[j], state,
                                 grp.b, grp.l, emit_state=is_prompt)
                if is_prompt:
                    dqk = res[2].shape[-1]
                    new_ml = (res[1], res[2].reshape(grp.b, 1, hh, dqk), res[3].reshape(grp.b, 1, hh),
                              res[4], res[5].reshape(grp.b, 1, hh, dqk), res[6].reshape(grp.b, 1, hh))
                a = res[0].reshape(grp.t, -1)
                w_out = ml_w_out[j].astype(BF16)
            else:
                w_in = hy_w_in[j].astype(BF16)
                proj = norm_matmul(x, norm_mix_pre[i], mod3, 0, 1, w_in, hy_b_in[j], BF16, grp, tm, PROJ_TILE_N)
                z, x0 = hyena_short_conv(proj.reshape(grp.b, grp.l, 3 * d), hy_conv_w[j], hy_conv_b[j],
                                         grp.b, grp.l, d)
                f_sum, f_dif = hyena_filters(grp.l, d, hy_f_w1[j], hy_f_b1[j], hy_f_w2[j], hy_f_b2[j],
                                             hy_f_w3[j], hy_sin_freq[j])
                cos_kt, sin_kt, cos_tk, sin_tk = _dft_tables(grp.l)
                g_r, g_s = hyena_filter_spectrum(cos_kt, sin_kt, f_sum, f_dif, hy_f_bias[j])
                y_r, y_s = hyena_forward_dft(cos_kt, sin_kt, z, g_r, g_s)
                a = hyena_inverse_dft(cos_tk, sin_tk, y_r, y_s, x0).reshape(grp.t, d)
                w_out = hy_w_out[j].astype(BF16)
            x, h_mlp = matmul_resnorm(a, w_out, x, mod3, norm_mix_post[i], norm_ffn_pre[i], grp, tm)
            x = mlp_block(h_mlp, x, norm_ffn_post[i], mod3,
                          mlp_w1, mlp_w2, i, grp, tm, MLP_TILE_F)
            ys.append(x)
        xs = ys

    y_prompt = xs[0].reshape(bp, lp, d)
    y_sample = xs[1].reshape(bs, ls, d)
    return (y_prompt, y_sample, new_k, new_v, new_rf, new_rb) + tuple(new_ml)
```

```python
import functools
import math

import jax
import jax.numpy as jnp
import numpy as np
from jax import lax
from jax.experimental import pallas as pl
from jax.experimental.pallas import tpu as pltpu

F32 = jnp.float32
BF16 = jnp.bfloat16

EPS = 1e-6
CHUNK = 128
RET_CHUNK = 256
GRID_W = 64
ROPE_THETA = 10000.0

RET_HEADS = 4
ATT_HEADS = 8
ATT_KV = 2
ATT_HD = 128
ML_HEADS = 4

HY_BANDS = 16
HY_SHORT = 3
HY_FAST_DECAY = 0.3
HY_SLOW_DECAY = 1.5
HY_TARGET = 1e-2

VMEM_LIMIT_BYTES = 56 * 1024 * 1024
LANES = 128
PROJ_TILES_N = (3072, 1536)
MLP_TILE_F = 1024


def _params(*sem):
    return pltpu.CompilerParams(dimension_semantics=sem, vmem_limit_bytes=VMEM_LIMIT_BYTES)


def _dot(a, b):
    return jnp.dot(a, b, preferred_element_type=F32)


def _dot_nt(a, b):
    return lax.dot_general(a, b, (((1,), (1,)), ((), ())), preferred_element_type=F32)


def _dot_tn(a, b):
    return lax.dot_general(a, b, (((0,), (0,)), ((), ())), preferred_element_type=F32)


def _rms(x, g):
    return x * lax.rsqrt(jnp.mean(x * x, axis=-1, keepdims=True) + EPS) * g


def _log_sigmoid(x):
    return jnp.minimum(x, 0.0) - jnp.log1p(jnp.exp(-jnp.abs(x)))


def _chunk_offset(ch, c):
    return ch * c if isinstance(ch, int) else pl.multiple_of(ch * c, c)


def _pipelined_scan(n_chunks, weigh_both, advance_both, fin):
    if n_chunks == 1:
        weigh_both(0)
        advance_both(0)
        fin(0)
        return
    assert n_chunks == 2 or n_chunks % 4 == 0

    def advance_pair(i):
        advance_both(i)
        advance_both(i + 1)

    def fin_pair(i):
        chunks = (i, n_chunks - 1 - i, i + 1, n_chunks - 2 - i)
        for ch in (dict.fromkeys(chunks) if isinstance(i, int) else chunks):
            fin(ch)

    def body(p, carry):
        advance_pair(2 * p)
        weigh_both(2 * p + 2)
        weigh_both(2 * p + 3)
        return carry

    def body_fin(p, carry):
        advance_pair(2 * p)
        fin_pair(2 * p)
        weigh_both(2 * p + 2)
        weigh_both(2 * p + 3)
        return carry

    pairs = n_chunks // 2
    weigh_both(0)
    weigh_both(1)
    if pairs > 1:
        lax.fori_loop(0, pairs // 2, body, 0)
        lax.fori_loop(pairs // 2, pairs - 1, body_fin, 0)
    advance_pair(n_chunks - 2)
    fin_pair(n_chunks - 2)


class Group:
    def __init__(self, b, l, row0, per_batch):
        self.b, self.l, self.row0, self.per_batch = b, l, row0, per_batch
        self.t = b * l

    def mod_spec(self, chunk, tm, d):
        row0, per_batch, l = self.row0, self.per_batch, self.l
        if per_batch:
            return pl.BlockSpec((1, 1, d), lambda i, *_: (row0 + (i * tm) // l, 0, chunk))
        return pl.BlockSpec((1, 1, d), lambda i, *_: (row0, 0, chunk))


def _adaln_kernel(c_ref, w_ref, b_ref, o_ref):
    s = jax.nn.silu(c_ref[...])
    o_ref[0] = _dot(s.astype(BF16), w_ref[0].astype(BF16)) + b_ref[0]


def adaln_all(cond, mod_w, mod_b):
    depth, d, n = mod_w.shape
    rows = cond.shape[0]
    tn = 768
    return pl.pallas_call(
        _adaln_kernel,
        grid=(depth, n // tn),
        in_specs=[
            pl.BlockSpec((rows, d), lambda l, j: (0, 0)),
            pl.BlockSpec((1, d, tn), lambda l, j: (l, 0, j)),
            pl.BlockSpec((1, 1, tn), lambda l, j: (l, 0, j)),
        ],
        out_specs=pl.BlockSpec((1, rows, tn), lambda l, j: (l, 0, j)),
        out_shape=jax.ShapeDtypeStruct((depth, rows, n), F32),
        compiler_params=_params("parallel", "parallel"),
        name="adaln",
    )(cond, mod_w, mod_b.reshape(depth, 1, n))


def _norm_mm_kernel(x_ref, g_ref, sh_ref, sc_ref, w_ref, *rest, has_bias, has_aux):
    rest = list(rest)
    b_ref = rest.pop(0) if has_bias else None
    wa_ref, ba_ref = (rest.pop(0), rest.pop(0)) if has_aux else (None, None)
    o_ref = rest.pop(0)
    oa_ref = rest.pop(0) if has_aux else None
    (h_scr,) = rest

    @pl.when(pl.program_id(1) == 0)
    def _():
        y = _rms(x_ref[...], g_ref[...])
        h = (y * (1.0 + sc_ref[0]) + sh_ref[0]).astype(BF16)
        h_scr[...] = h
        if has_aux:
            oa_ref[...] = _dot(h, wa_ref[...]) + ba_ref[...]

    acc = _dot(h_scr[...], w_ref[...])
    if has_bias:
        acc = acc + b_ref[...]
    o_ref[...] = acc.astype(o_ref.dtype)


def norm_matmul(x, gain, mod3, sh_idx, sc_idx, w, bias, out_dtype, grp, tm, aux=None):
    t, d = x.shape
    n = w.shape[1]
    tn = next(c for c in PROJ_TILES_N if n % c == 0)
    in_specs = [
        pl.BlockSpec((tm, d), lambda i, j: (i, 0)),
        pl.BlockSpec((1, d), lambda i, j: (0, 0)),
        grp.mod_spec(sh_idx, tm, d),
        grp.mod_spec(sc_idx, tm, d),
        pl.BlockSpec((d, tn), lambda i, j: (0, j)),
    ]
    args = [x, gain.reshape(1, d), mod3, mod3, w]
    if bias is not None:
        in_specs.append(pl.BlockSpec((1, tn), lambda i, j: (0, j)))
        args.append(bias.reshape(1, n))
    out_specs = [pl.BlockSpec((tm, tn), lambda i, j: (i, j))]
    out_shape = [jax.ShapeDtypeStruct((t, n), out_dtype)]
    if aux is not None:
        w_aux, b_aux = aux
        na = w_aux.shape[1]
        in_specs += [pl.BlockSpec((d, na), lambda i, j: (0, 0)), pl.BlockSpec((1, na), lambda i, j: (0, 0))]
        args += [w_aux, b_aux.reshape(1, na)]
        out_specs.append(pl.BlockSpec((tm, na), lambda i, j: (i, 0)))
        out_shape.append(jax.ShapeDtypeStruct((t, na), F32))
    res = pl.pallas_call(
        functools.partial(_norm_mm_kernel, has_bias=bias is not None, has_aux=aux is not None),
        grid=(t // tm, n // tn),
        in_specs=in_specs,
        out_specs=out_specs,
        out_shape=out_shape,
        scratch_shapes=[pltpu.VMEM((tm, d), BF16)],
        compiler_params=_params("parallel", "arbitrary"),
        name="norm_matmul",
    )(*args)
    return res if aux is not None else res[0]


def _mm_res_kernel(a_ref, w_ref, x_ref, gate_ref, pg_ref, ng_ref, sh_ref, sc_ref, o_ref, h_ref):
    y = _dot(a_ref[...], w_ref[...])
    x1 = x_ref[...] + gate_ref[0] * _rms(y, pg_ref[...])
    o_ref[...] = x1
    h_ref[...] = (_rms(x1, ng_ref[...]) * (1.0 + sc_ref[0]) + sh_ref[0]).astype(h_ref.dtype)


def matmul_resnorm(a, w, x, mod3, post_gain, next_gain, grp, tm):
    t, k = a.shape
    d = w.shape[1]
    row = pl.BlockSpec((tm, d), lambda i: (i, 0))
    vec = pl.BlockSpec((1, d), lambda i: (0, 0))
    return pl.pallas_call(
        _mm_res_kernel,
        grid=(t // tm,),
        in_specs=[
            pl.BlockSpec((tm, k), lambda i: (i, 0)),
            pl.BlockSpec((k, d), lambda i: (0, 0)),
            row,
            grp.mod_spec(2, tm, d),
            vec,
            vec,
            grp.mod_spec(3, tm, d),
            grp.mod_spec(4, tm, d),
        ],
        out_specs=[row, row],
        out_shape=[jax.ShapeDtypeStruct((t, d), F32), jax.ShapeDtypeStruct((t, d), BF16)],
        compiler_params=_params("parallel"),
        name="out_proj",
    )(a, w, x, mod3, post_gain.reshape(1, d), next_gain.reshape(1, d), mod3, mod3)


def _mlp_kernel(h_ref, x_ref, w1_ref, w2_ref, gate_ref, pg_ref, o_ref, acc_scr):
    j = pl.program_id(1)

    @pl.when(j == 0)
    def _():
        acc_scr[...] = jnp.zeros_like(acc_scr)

    u = _dot(h_ref[...], w1_ref[...].astype(BF16))
    u = jnp.square(jnp.maximum(u, 0.0)).astype(BF16)
    acc_scr[...] += _dot(u, w2_ref[...].astype(BF16))

    @pl.when(j == pl.num_programs(1) - 1)
    def _():
        o_ref[...] = x_ref[...] + gate_ref[0] * _rms(acc_scr[...], pg_ref[...])


def mlp_block(h, x, post_gain, mod3, w1, w2, layer, grp, tm, tf):
    t, d = x.shape
    f = w1.shape[2]
    return pl.pallas_call(
        _mlp_kernel,
        grid=(t // tm, f // tf),
        in_specs=[
            pl.BlockSpec((tm, d), lambda i, j: (i, 0)),
            pl.BlockSpec((tm, d), lambda i, j: (i, 0)),
            pl.BlockSpec((None, d, tf), lambda i, j: (layer, 0, j)),
            pl.BlockSpec((None, tf, d), lambda i, j: (layer, j, 0)),
            grp.mod_spec(5, tm, d),
            pl.BlockSpec((1, d), lambda i, j: (0, 0)),
        ],
        out_specs=pl.BlockSpec((tm, d), lambda i, j: (i, 0)),
        out_shape=jax.ShapeDtypeStruct((t, d), F32),
        scratch_shapes=[pltpu.VMEM((tm, d), F32)],
        compiler_params=_params("parallel", "arbitrary"),
        name="mlp",
    )(h, x, w1, w2, mod3, post_gain.reshape(1, d))


def _ret_kernel(dec_ref, q_ref, k_ref, v_ref, g_ref, gn_ref, *rest, n_chunks, has_state, emit_state):
    rest = list(rest)
    s0f_ref = s0b_ref = sf_ref = sb_ref = None
    if has_state:
        s0f_ref, s0b_ref = rest[:2]
        rest = rest[2:]
    o_ref = rest.pop(0)
    if emit_state:
        sf_ref, sb_ref = rest[:2]
        rest = rest[2:]
    st_f, st_b, of_scr, ob_scr, sw_f, sw_b, kw_f, kw_b = rest

    c = RET_CHUNK
    dk = q_ref.shape[-1]
    h = pl.program_id(1)
    lg_f = _log_sigmoid(jnp.full((1, 1), dec_ref[0, h], F32))
    lg_b = _log_sigmoid(jnp.full((1, 1), dec_ref[1, h], F32))
    ri = lax.broadcasted_iota(jnp.int32, (c, c), 0)
    ci = lax.broadcasted_iota(jnp.int32, (c, c), 1)
    rel = (ri - ci).astype(F32)
    intra_f = jnp.where(rel >= 0, jnp.exp(lg_f * jnp.maximum(rel, 0.0)), 0.0)
    intra_b = jnp.where(rel <= 0, jnp.exp(lg_b * jnp.maximum(-rel, 0.0)), 0.0)
    idx = lax.broadcasted_iota(jnp.int32, (c, 1), 0).astype(F32)
    qdec_f = jnp.exp(lg_f * (idx + 1.0))
    kdec_f = jnp.exp(lg_f * (c - 1.0 - idx))
    qdec_b = jnp.exp(lg_b * (c - idx))
    kdec_b = jnp.exp(lg_b * idx)
    cdec_f = jnp.exp(lg_f * c)
    cdec_b = jnp.exp(lg_b * c)
    q_scale = dk ** -0.5

    if has_state:
        st_f[...] = s0f_ref[0, 0, 0]
        st_b[...] = s0b_ref[0, 0, 0]
    else:
        st_f[...] = jnp.zeros_like(st_f)
        st_b[...] = jnp.zeros_like(st_b)

    def weigh(ch, intra, kdec, sw, kw_s):
        rows = pl.ds(_chunk_offset(ch, c), c)
        kc = k_ref[0, rows, :]
        sw[ch] = (_dot_nt(q_ref[0, rows, :], kc) * (intra * q_scale)).astype(BF16)
        kw_s[ch] = (kc.astype(F32) * kdec).T.astype(BF16)

    def advance(ch, st, out_scr, qdec, cdec, sw, kw_s):
        rows = pl.ds(_chunk_offset(ch, c), c)
        qc = q_ref[0, rows, :]
        vc = v_ref[0, rows, :]
        s_old = st[...]
        out_scr[rows, :] = _dot(sw[ch], vc) + _dot(qc, s_old.astype(BF16)) * (qdec * q_scale)
        st[...] = s_old * cdec + _dot(kw_s[ch], vc)

    def fin(ch):
        rows = pl.ds(_chunk_offset(ch, c), c)
        o = _rms(of_scr[rows, :] + ob_scr[rows, :], gn_ref[...])
        o_ref[0, rows, :] = (jax.nn.silu(g_ref[0, rows, :].astype(F32)) * o).astype(o_ref.dtype)

    def weigh_both(i):
        weigh(i, intra_f, kdec_f, sw_f, kw_f)
        weigh(n_chunks - 1 - i, intra_b, kdec_b, sw_b, kw_b)

    def advance_both(i):
        advance(i, st_f, of_scr, qdec_f, cdec_f, sw_f, kw_f)
        advance(n_chunks - 1 - i, st_b, ob_scr, qdec_b, cdec_b, sw_b, kw_b)

    _pipelined_scan(n_chunks, weigh_both, advance_both, fin)

    if emit_state:
        sf_ref[0, 0, 0] = st_f[...]
        sb_ref[0, 0, 0] = st_b[...]


def retention_core(proj, dec, gn, s0f, s0b, b, l, emit_state):
    hh = RET_HEADS
    width = proj.shape[-1]
    dk = width // (6 * hh)
    dv = 2 * dk
    has_state = s0f is not None
    n_chunks = l // RET_CHUNK
    in_specs = [
        pl.BlockSpec(memory_space=pltpu.SMEM),
        pl.BlockSpec((1, l, dk), lambda bi, h: (bi, 0, h)),
        pl.BlockSpec((1, l, dk), lambda bi, h: (bi, 0, hh + h)),
        pl.BlockSpec((1, l, dv), lambda bi, h: (bi, 0, hh + h)),
        pl.BlockSpec((1, l, dv), lambda bi, h: (bi, 0, 2 * hh + h)),
        pl.BlockSpec((1, dv), lambda bi, h: (0, h)),
    ]
    args = [dec, proj, proj, proj, proj, gn.reshape(1, hh * dv)]
    if has_state:
        st_spec = pl.BlockSpec((1, 1, 1, dk, dv), lambda bi, h: (bi, 0, h, 0, 0))
        in_specs += [st_spec, st_spec]
        args += [s0f, s0b]
    out_specs = [pl.BlockSpec((1, l, dv), lambda bi, h: (bi, 0, h))]
    out_shape = [jax.ShapeDtypeStruct((b, l, hh * dv), BF16)]
    if emit_state:
        so_spec = pl.BlockSpec((1, 1, 1, dk, dv), lambda bi, h: (bi, 0, h, 0, 0))
        out_specs += [so_spec, so_spec]
        out_shape += [jax.ShapeDtypeStruct((b, 1, hh, dk, dv), F32)] * 2
    return pl.pallas_call(
        functools.partial(_ret_kernel, n_chunks=n_chunks, has_state=has_state, emit_state=emit_state),
        grid=(b, hh),
        in_specs=in_specs,
        out_specs=out_specs,
        out_shape=out_shape,
        scratch_shapes=[pltpu.VMEM((dk, dv), F32), pltpu.VMEM((dk, dv), F32),
                        pltpu.VMEM((l, dv), F32), pltpu.VMEM((l, dv), F32)]
                       + [pltpu.VMEM((n_chunks, RET_CHUNK, RET_CHUNK), BF16)] * 2
                       + [pltpu.VMEM((n_chunks, dk, RET_CHUNK), BF16)] * 2,
        compiler_params=_params("parallel", "parallel"),
        name="retention",
    )(*args)


def _rope_rot(x, cos_t, sin_t):
    lane = lax.broadcasted_iota(jnp.int32, x.shape, x.ndim - 1)
    nxt = pltpu.roll(x, LANES - 1, axis=x.ndim - 1)
    prv = pltpu.roll(x, 1, axis=x.ndim - 1)
    swapped = jnp.where(jnp.bitwise_and(lane, 1) == 0, nxt, prv)
    return x * cos_t + swapped * sin_t


def _att_kernel(q_ref, k_ref, v_ref, qg_ref, kg_ref, *rest, rope, has_cache, emit_kv, groups):
    rest = list(rest)
    cosq_ref = sinq_ref = cosk_ref = sink_ref = ck_ref = cv_ref = kn_ref = vo_ref = None
    if rope:
        cosq_ref, sinq_ref, cosk_ref, sink_ref = rest[:4]
        rest = rest[4:]
    if has_cache:
        ck_ref, cv_ref = rest[:2]
        rest = rest[2:]
    o_ref = rest.pop(0)
    if emit_kv:
        kn_ref, vo_ref = rest[:2]
        rest = rest[2:]
    k_scr, v_scr = rest
    hd = ATT_HD
    exp2_scale = hd ** -0.5 * math.log2(math.e)

    @pl.when(pl.program_id(2) == 0)
    def _():
        kn = _rms(k_ref[0].astype(F32), kg_ref[...])
        if emit_kv:
            kn_ref[0] = kn
            vo_ref[0] = v_ref[0]
        if rope:
            kn = _rope_rot(kn, cosk_ref[...], sink_ref[...])
        k_scr[...] = kn.astype(BF16)
        v_scr[:, :hd] = v_ref[0].astype(BF16)
        v_scr[:, hd:] = jnp.ones((v_scr.shape[0], hd), BF16)

    if has_cache:
        ck = ck_ref[0].astype(BF16)
        cv = jnp.concatenate([cv_ref[0].astype(BF16), jnp.ones((cv_ref.shape[1], hd), BF16)], axis=1)

    def scores(g):
        qn = _rms(q_ref[0, :, g * hd:(g + 1) * hd].astype(F32), qg_ref[...])
        if rope:
            qn = _rope_rot(qn, cosq_ref[...], sinq_ref[...])
        qb = qn.astype(BF16)
        return _dot_nt(qb, k_scr[...]), (_dot_nt(qb, ck) if has_cache else None)

    nxt = scores(0)
    for g in range(groups):
        s1, s2 = nxt
        if g + 1 < groups:
            nxt = scores(g + 1)
        m = jnp.max(s1, axis=-1, keepdims=True)
        if has_cache:
            m = jnp.maximum(m, jnp.max(s2, axis=-1, keepdims=True))
        p1 = jnp.exp2((s1 - m) * exp2_scale)
        nd = _dot(p1.astype(BF16), v_scr[...])
        if has_cache:
            p2 = jnp.exp2((s2 - m) * exp2_scale)
            nd = nd + _dot(p2.astype(BF16), cv)
        o_ref[0, :, g * hd:(g + 1) * hd] = (nd[:, :hd] * (1.0 / nd[:, hd:])).astype(o_ref.dtype)


def attention_core(proj, q_gain, k_gain, rope_tabs, cache_k, cache_v, b, l, tq, emit_kv):
    hd, kv, heads = ATT_HD, ATT_KV, ATT_HEADS
    groups = heads // kv
    rope = rope_tabs is not None
    has_cache = cache_k is not None
    in_specs = [
        pl.BlockSpec((1, tq, groups * hd), lambda bi, kh, qi: (bi, qi, kh)),
        pl.BlockSpec((1, l, hd), lambda bi, kh, qi: (bi, 0, heads + kh)),
        pl.BlockSpec((1, l, hd), lambda bi, kh, qi: (bi, 0, heads + kv + kh)),
        pl.BlockSpec((1, hd), lambda bi, kh, qi: (0, 0)),
        pl.BlockSpec((1, hd), lambda bi, kh, qi: (0, 0)),
    ]
    args = [proj, proj, proj, q_gain.reshape(1, hd), k_gain.reshape(1, hd)]
    if rope:
        cos_t, sin_t = rope_tabs
        in_specs += [
            pl.BlockSpec((tq, hd), lambda bi, kh, qi: (qi, 0)),
            pl.BlockSpec((tq, hd), lambda bi, kh, qi: (qi, 0)),
            pl.BlockSpec((l, hd), lambda bi, kh, qi: (0, 0)),
            pl.BlockSpec((l, hd), lambda bi, kh, qi: (0, 0)),
        ]
        args += [cos_t, sin_t, cos_t, sin_t]
    if has_cache:
        past = cache_k.shape[1]
        c_spec = pl.BlockSpec((1, past, hd), lambda bi, kh, qi: (bi, 0, kh))
        in_specs += [c_spec, c_spec]
        args += [cache_k, cache_v]
    out_specs = [pl.BlockSpec((1, tq, groups * hd), lambda bi, kh, qi: (bi, qi, kh))]
    out_shape = [jax.ShapeDtypeStruct((b, l, heads * hd), BF16)]
    if emit_kv:
        kv_spec = pl.BlockSpec((1, l, hd), lambda bi, kh, qi: (bi, 0, kh))
        out_specs += [kv_spec, kv_spec]
        out_shape += [jax.ShapeDtypeStruct((b, l, kv * hd), F32)] * 2
    return pl.pallas_call(
        functools.partial(_att_kernel, rope=rope, has_cache=has_cache, emit_kv=emit_kv, groups=groups),
        grid=(b, kv, l // tq),
        in_specs=in_specs,
        out_specs=out_specs,
        out_shape=out_shape,
        scratch_shapes=[pltpu.VMEM((l, hd), BF16), pltpu.VMEM((l, 2 * hd), BF16)],
        compiler_params=_params("parallel", "parallel", "arbitrary"),
        name="attention",
    )(*args)


def _rope_tables(l):
    rows = l // GRID_W
    row = jnp.repeat(jnp.arange(rows, dtype=F32), GRID_W)
    col = jnp.tile(jnp.arange(GRID_W, dtype=F32), rows)
    half = ATT_HD // 2
    inv = ROPE_THETA ** (-jnp.arange(0, half, 2, dtype=F32) / half)
    ang = jnp.concatenate([row[:, None] * inv, col[:, None] * inv], axis=-1)
    cos_t = jnp.repeat(jnp.cos(ang), 2, axis=-1)
    sin_h = jnp.sin(ang)
    sin_t = jnp.stack([-sin_h, sin_h], axis=-1).reshape(l, ATT_HD)
    return cos_t, sin_t


def _ml_kernel(q_ref, k_ref, v_ref, og_ref, gt_ref, gn_ref, *rest, n_chunks, has_state, emit_state):
    rest = list(rest)
    c0f_ref = n0f_ref = m0f_ref = c0b_ref = n0b_ref = m0b_ref = None
    if has_state:
        c0f_ref, n0f_ref, m0f_ref, c0b_ref, n0b_ref, m0b_ref = rest[:6]
        rest = rest[6:]
    o_ref = rest.pop(0)
    outs = None
    if emit_state:
        outs = rest[:6]
        rest = rest[6:]
    cm_f, cm_b, nv_f, nv_b, hf_scr, hb_scr, fl_f, fl_b, br_f, br_b, mo_f, mo_b, mn_f, mn_b = rest[:14]
    sw_f, sw_b, kw_f, kw_b = rest[14:18]
    vt_f, vt_b = rest[18:23], rest[23:28]

    c = CHUNK
    dqk = q_ref.shape[-1]
    k_scale = dqk ** -0.5
    ri = lax.broadcasted_iota(jnp.int32, (c, c), 0)
    ci = lax.broadcasted_iota(jnp.int32, (c, c), 1)
    mask_f = ci <= ri
    mask_b = ci >= ri
    assert c == LANES and dqk == LANES
    dv_tiles = v_ref.shape[-1] // LANES

    def bf3(m):
        m = jnp.where(m, 1.0, 0.0).astype(BF16)
        return jnp.concatenate([m, m, m], axis=1), jnp.concatenate([m, m, m], axis=0)

    mf3, mf3_t = bf3(mask_f)
    mb3, mb3_t = bf3(mask_b)
    eye3, _ = bf3(ci == ri)

    def split3(x):
        hi = x.astype(BF16)
        r1 = x - hi.astype(F32)
        mid = r1.astype(BF16)
        lo = (r1 - mid.astype(F32)).astype(BF16)
        return jnp.concatenate([hi, mid, lo], axis=1)

    def wide(x):
        return jnp.concatenate([x] * dv_tiles, axis=1)

    if has_state:
        cm_f[...] = c0f_ref[0, 0, 0]
        cm_b[...] = c0b_ref[0, 0, 0]
        nv_f[...] = n0f_ref[0, 0]
        nv_b[...] = n0b_ref[0, 0]
        m0_f = jnp.broadcast_to(m0f_ref[0, 0], (1, LANES))
        m0_b = jnp.broadcast_to(m0b_ref[0, 0], (1, LANES))
    else:
        for r in (cm_f, cm_b, nv_f, nv_b):
            r[...] = jnp.zeros_like(r)
        m0_f = m0_b = jnp.zeros((1, LANES), F32)

    def gate_pass(row_i, row_f, m3_t, m0, fl, br, mo, mn, order):
        f_all = _log_sigmoid(gt_ref[0, 0, row_f])
        fl[...] = f_all
        b_all = _dot(split3(f_all), m3_t)
        br[...] = b_all
        b_end = jnp.sum(f_all, axis=-1, keepdims=True)
        w_max = jnp.max(b_end - b_all + gt_ref[0, 0, row_i], axis=-1, keepdims=True)
        m = m0
        for r in order:
            mo[r:r + 1, :] = m
            m = jnp.maximum(b_end[r:r + 1, :] + m, w_max[r:r + 1, :])
            mn[r:r + 1, :] = m
        return m

    m_last_f = gate_pass(0, 1, mb3_t, m0_f, fl_f, br_f, mo_f, mn_f, range(n_chunks))
    m_last_b = gate_pass(2, 3, mf3_t, m0_b, fl_b, br_b, mo_b, mn_b, reversed(range(n_chunks)))

    def weigh(ch, mask, m3, fl, br, mo, mn, row_i, sw, kw_s, wx_s, rs_s, em_s, cd_s, ks_s):
        rows = pl.ds(_chunk_offset(ch, c), c)
        one = pl.ds(ch, 1)
        qc = q_ref[0, rows, :]
        kc = k_ref[0, rows, :]
        i_row = gt_ref[0, 0, row_i, one, :]
        f_row = fl[one, :]
        b_row = br[one, :]
        m_old = mo[one, :]
        m_new = mn[one, :]
        b_q = _dot_nt(m3, jnp.broadcast_to(split3(f_row), (LANES, 3 * c)))
        i_q = _dot_nt(eye3, jnp.broadcast_to(split3(i_row), (LANES, 3 * c)))
        dlog = jnp.where(mask, b_q - b_row + i_row, -jnp.inf)
        inter = b_q + m_old
        m_q = jnp.maximum(inter, jnp.max(dlog, axis=-1, keepdims=True))
        s = _dot_nt(qc, kc) * (jnp.exp(dlog - m_q) * k_scale)
        sw[ch] = s.astype(BF16)
        rs_s[rows, :] = jnp.broadcast_to(jnp.sum(s, axis=-1, keepdims=True), (c, LANES))
        wx_s[rows, :] = jnp.exp(inter - m_q)
        em_s[rows, :] = jnp.exp(-m_q)
        b_end = jnp.sum(f_row, axis=-1, keepdims=True)
        wlog = b_end - b_q + i_q
        cd_s[one, :] = jnp.exp(b_end + m_old - m_new)
        kw = kc.astype(F32) * (jnp.exp(wlog - m_new) * k_scale)
        kw_s[ch] = kw.T.astype(BF16)
        ks_s[one, :] = jnp.sum(kw, axis=0, keepdims=True)

    def advance(ch, cm, nv, out_scr, sw, kw_s, wx_s, rs_s, em_s, cd_s, ks_s):
        rows = pl.ds(_chunk_offset(ch, c), c)
        one = pl.ds(ch, 1)
        qc = q_ref[0, rows, :]
        vc = v_ref[0, rows, :]
        c_old = cm[...]
        n_old = nv[...]
        w_x = wx_s[rows, :]
        num = _dot(sw[ch], vc) + _dot(qc, c_old.astype(BF16)) * wide(w_x)
        qn = _dot_nt(qc, jnp.broadcast_to(n_old, (LANES, dqk)).astype(BF16))
        den = jnp.maximum(jnp.abs(rs_s[rows, :] + qn * w_x), em_s[rows, :])
        out_scr[rows, :] = num * wide(1.0 / den)
        carry_dec = cd_s[one, :]
        cm[...] = c_old * wide(carry_dec) + _dot(kw_s[ch], vc)
        nv[...] = n_old * carry_dec + ks_s[one, :]

    def fin(ch):
        rows = pl.ds(_chunk_offset(ch, c), c)
        hn = _rms(hf_scr[rows, :] + hb_scr[rows, :], gn_ref[...])
        o_ref[0, rows, :] = (jax.nn.sigmoid(og_ref[0, rows, :].astype(F32)) * hn).astype(o_ref.dtype)

    tmp_f = (sw_f, kw_f) + tuple(vt_f)
    tmp_b = (sw_b, kw_b) + tuple(vt_b)

    def weigh_both(i):
        weigh(i, mask_f, mf3, fl_f, br_f, mo_f, mn_f, 0, *tmp_f)
        weigh(n_chunks - 1 - i, mask_b, mb3, fl_b, br_b, mo_b, mn_b, 2, *tmp_b)

    def advance_both(i):
        advance(i, cm_f, nv_f, hf_scr, *tmp_f)
        advance(n_chunks - 1 - i, cm_b, nv_b, hb_scr, *tmp_b)

    _pipelined_scan(n_chunks, weigh_both, advance_both, fin)

    if emit_state:
        for dst, src in zip(outs[0::3], (cm_f, cm_b)):
            dst[0, 0, 0] = src[...]
        for dst, src in zip(outs[1::3], (nv_f, nv_b)):
            dst[0, 0] = src[...]
        for dst, m_last in zip(outs[2::3], (m_last_f, m_last_b)):
            dst[0, 0] = m_last[:, 0:1]


def mlstm_core(proj, gates_t, gn, state, b, l, emit_state):
    hh = ML_HEADS
    width = proj.shape[-1]
    dqk = width // (6 * hh)
    dv = 2 * dqk
    has_state = state is not None
    n_chunks = l // CHUNK
    chunk_rows = gates_t.shape[3]
    in_specs = [
        pl.BlockSpec((1, l, dqk), lambda bi, h: (bi, 0, h)),
        pl.BlockSpec((1, l, dqk), lambda bi, h: (bi, 0, hh + h)),
        pl.BlockSpec((1, l, dv), lambda bi, h: (bi, 0, hh + h)),
        pl.BlockSpec((1, l, dv), lambda bi, h: (bi, 0, 2 * hh + h)),
        pl.BlockSpec((1, 1, 4, chunk_rows, CHUNK), lambda bi, h: (bi, h, 0, 0, 0)),
        pl.BlockSpec((1, dv), lambda bi, h: (0, h)),
    ]
    args = [proj, proj, proj, proj, gates_t, gn.reshape(1, hh * dv)]
    c_spec_in = pl.BlockSpec((1, 1, 1, dqk, dv), lambda bi, h: (bi, 0, h, 0, 0))
    n_spec = pl.BlockSpec((1, 1, 1, dqk), lambda bi, h: (bi, h, 0, 0))
    m_spec = pl.BlockSpec((1, 1, 1, 1), lambda bi, h: (bi, h, 0, 0))
    if has_state:
        cf, nf, mf, cb, nb, mb = state
        in_specs += [c_spec_in, n_spec, m_spec, c_spec_in, n_spec, m_spec]
        args += [cf, nf.reshape(b, hh, 1, dqk), mf.reshape(b, hh, 1, 1),
                 cb, nb.reshape(b, hh, 1, dqk), mb.reshape(b, hh, 1, 1)]
    out_specs = [pl.BlockSpec((1, l, dv), lambda bi, h: (bi, 0, h))]
    out_shape = [jax.ShapeDtypeStruct((b, l, hh * dv), BF16)]
    if emit_state:
        out_specs += [c_spec_in, n_spec, m_spec] * 2
        out_shape += [jax.ShapeDtypeStruct((b, 1, hh, dqk, dv), F32),
                      jax.ShapeDtypeStruct((b, hh, 1, dqk), F32),
                      jax.ShapeDtypeStruct((b, hh, 1, 1), F32)] * 2
    return pl.pallas_call(
        functools.partial(_ml_kernel, n_chunks=n_chunks, has_state=has_state, emit_state=emit_state),
        grid=(b, hh),
        in_specs=in_specs,
        out_specs=out_specs,
        out_shape=out_shape,
        scratch_shapes=[pltpu.VMEM((dqk, dv), F32), pltpu.VMEM((dqk, dv), F32),
                        pltpu.VMEM((1, dqk), F32), pltpu.VMEM((1, dqk), F32),
                        pltpu.VMEM((l, dv), F32), pltpu.VMEM((l, dv), F32)]
                       + [pltpu.VMEM((chunk_rows, CHUNK), F32)] * 4 + [pltpu.VMEM((chunk_rows, LANES), F32)] * 4
                       + [pltpu.VMEM((n_chunks, CHUNK, CHUNK), BF16)] * 2 + [pltpu.VMEM((n_chunks, dqk, CHUNK), BF16)] * 2
                       + [pltpu.VMEM((l, LANES), F32)] * 3 + [pltpu.VMEM((chunk_rows, LANES), F32)] * 2
                       + [pltpu.VMEM((l, LANES), F32)] * 3 + [pltpu.VMEM((chunk_rows, LANES), F32)] * 2,
        compiler_params=_params("parallel", "parallel"),
        name="mlstm",
    )(*args)


def _hy_filter_kernel(feat_ref, w1_ref, b1_ref, w2_ref, b2_ref, w3_ref, fr_ref, win_ref, sum_ref, dif_ref):
    d = win_ref.shape[-1]
    z = jnp.sin(fr_ref[0:1, :] * (_dot(feat_ref[...].astype(BF16), w1_ref[...].astype(BF16)) + b1_ref[...]))
    z = jnp.sin(fr_ref[1:2, :] * (_dot(z.astype(BF16), w2_ref[...].astype(BF16)) + b2_ref[...]))
    filt = _dot(z.astype(BF16), w3_ref[...].astype(BF16))
    win = win_ref[...]
    ff = filt[:, :d] * win
    fb = filt[:, d:] * win
    sum_ref[...] = (ff + fb).astype(sum_ref.dtype)
    dif_ref[...] = (ff - fb).astype(dif_ref.dtype)


def hyena_filters(l, d, w1, b1, w2, b2, w3, freq):
    t = jnp.linspace(0.0, 1.0, l, dtype=F32)[:, None]
    pos = jnp.arange(l, dtype=F32)[:, None]
    bands = jnp.linspace(1e-4, HY_BANDS - 1, HY_BANDS, dtype=F32)
    ang = 2.0 * math.pi * pos * bands / l
    feats = jnp.concatenate([t, jnp.cos(ang), -jnp.sin(ang)], axis=-1)
    emb = feats.shape[1]
    feats = jnp.pad(feats, ((0, 0), (0, LANES - emb)))
    w1p = jnp.pad(w1, ((0, LANES - emb), (0, 0)))
    deltas = jnp.abs(jnp.linspace(math.log(HY_TARGET) / HY_SLOW_DECAY,
                                  math.log(HY_TARGET) / HY_FAST_DECAY, d, dtype=F32))
    window = jnp.exp(-t * deltas)
    hid = w2.shape[0]
    tl = min(l, 512)
    full = lambda shp: pl.BlockSpec(shp, lambda i: (0,) * len(shp))
    return pl.pallas_call(
        _hy_filter_kernel,
        grid=(l // tl,),
        in_specs=[
            pl.BlockSpec((tl, LANES), lambda i: (i, 0)),
            full((LANES, hid)), full((1, hid)), full((hid, hid)), full((1, hid)), full((hid, 2 * d)),
            full((2, hid)),
            pl.BlockSpec((tl, d), lambda i: (i, 0)),
        ],
        out_specs=[pl.BlockSpec((tl, d), lambda i: (i, 0))] * 2,
        out_shape=[jax.ShapeDtypeStruct((l, d), BF16)] * 2,
        compiler_params=_params("parallel"),
        name="hyena_filter",
    )(feats, w1p, b1.reshape(1, hid), w2, b2.reshape(1, hid), w3, freq, window)


def _dft_tables(l):
    n = 2 * l
    blk = 64
    assert l % blk == 0

    def trig(step, count):
        k = lax.broadcasted_iota(jnp.int32, (l, count), 0)
        t = lax.broadcasted_iota(jnp.int32, (l, count), 1)
        ang = (((2 * k + 1) * step * t) % (2 * n)).astype(F32) * (math.pi / n)
        return jnp.cos(ang), jnp.sin(ang)

    c1, s1 = trig(blk, l // blk)
    c0, s0 = trig(1, blk)
    cos_kt = (c1[:, :, None] * c0[:, None, :] - s1[:, :, None] * s0[:, None, :]).reshape(l, l).astype(BF16)
    sin_kt = (s1[:, :, None] * c0[:, None, :] + c1[:, :, None] * s0[:, None, :]).reshape(l, l).astype(BF16)
    c1t, s1t, c0t, s0t = c1.T, s1.T, c0.T, s0.T
    cos_tk = (c1t[:, None, :] * c0t[None, :, :] - s1t[:, None, :] * s0t[None, :, :]).reshape(l, l).astype(BF16)
    sin_tk = (s1t[:, None, :] * c0t[None, :, :] + c1t[:, None, :] * s0t[None, :, :]).reshape(l, l).astype(BF16)
    return cos_kt, sin_kt, cos_tk, sin_tk


def _hy_spec_kernel(c_ref, s_ref, fs_ref, fd_ref, bias_ref, gr_ref, gs_ref):
    gr_ref[...] = _dot(c_ref[...], fs_ref[...]) + bias_ref[...]
    gs_ref[...] = _dot(s_ref[...], fd_ref[...])


def hyena_filter_spectrum(cos_kt, sin_kt, f_sum, f_dif, f_bias):
    l, d = f_sum.shape
    tm = min(l, 512)
    tn = min(d, 512)
    return pl.pallas_call(
        _hy_spec_kernel,
        grid=(l // tm, d // tn),
        in_specs=[
            pl.BlockSpec((tm, l), lambda i, j: (i, 0)),
            pl.BlockSpec((tm, l), lambda i, j: (i, 0)),
            pl.BlockSpec((l, tn), lambda i, j: (0, j)),
            pl.BlockSpec((l, tn), lambda i, j: (0, j)),
            pl.BlockSpec((1, tn), lambda i, j: (0, j)),
        ],
        out_specs=[pl.BlockSpec((tm, tn), lambda i, j: (i, j))] * 2,
        out_shape=[jax.ShapeDtypeStruct((l, d), F32)] * 2,
        compiler_params=_params("parallel", "parallel"),
        name="hyena_filter_spectrum",
    )(cos_kt, sin_kt, f_sum, f_dif, f_bias.reshape(1, d))


def _hy_conv_kernel(p0_ref, p1_ref, pv_ref, w0_ref, w1_ref, wv_ref, b0_ref, b1_ref, bv_ref, z_ref, x0_ref):
    l = p0_ref.shape[1]
    t = lax.broadcasted_iota(jnp.int32, (l, 1), 0)

    def conv(p_ref, w_ref, b_ref):
        p = p_ref[0].astype(F32)
        prev = jnp.where(t == 0, 0.0, pltpu.roll(p, 1, axis=0))
        nxt = jnp.where(t == l - 1, 0.0, pltpu.roll(p, l - 1, axis=0))
        return b_ref[...] + prev * w_ref[0:1, :] + p * w_ref[1:2, :] + nxt * w_ref[2:3, :]

    x0_ref[0] = conv(p0_ref, w0_ref, b0_ref).astype(x0_ref.dtype)
    z_ref[0] = (conv(pv_ref, wv_ref, bv_ref) * conv(p1_ref, w1_ref, b1_ref)).astype(z_ref.dtype)


def hyena_short_conv(proj, conv_w, conv_b, b, l, d):
    tc = min(d, 512)
    nc = d // tc
    p_spec = lambda off: pl.BlockSpec((1, l, tc), lambda bi, j: (bi, 0, off * nc + j))
    w_spec = lambda off: pl.BlockSpec((HY_SHORT, tc), lambda bi, j: (0, off * nc + j))
    b_spec = lambda off: pl.BlockSpec((1, tc), lambda bi, j: (0, off * nc + j))
    cb = conv_b.reshape(1, 3 * d)
    return pl.pallas_call(
        _hy_conv_kernel,
        grid=(b, nc),
        in_specs=[p_spec(0), p_spec(1), p_spec(2), w_spec(0), w_spec(1), w_spec(2),
                  b_spec(0), b_spec(1), b_spec(2)],
        out_specs=[pl.BlockSpec((1, l, tc), lambda bi, j: (bi, 0, j))] * 2,
        out_shape=[jax.ShapeDtypeStruct((b, l, d), BF16), jax.ShapeDtypeStruct((b, l, d), BF16)],
        compiler_params=_params("parallel", "parallel"),
        name="hyena_short_conv",
    )(proj, proj, proj, conv_w, conv_w, conv_w, cb, cb, cb)


def _hy_fwd_kernel(c_ref, s_ref, z_ref, gr_ref, gs_ref, yr_ref, ys_ref):
    z = z_ref[0]
    zr = _dot(c_ref[...], z)
    zs = _dot(s_ref[...], z)
    gr = gr_ref[...]
    gs = gs_ref[...]
    yr_ref[0] = (zr * gr - zs * gs).astype(yr_ref.dtype)
    ys_ref[0] = (zr * gs + zs * gr).astype(ys_ref.dtype)


def hyena_forward_dft(cos_kt, sin_kt, z, g_r, g_s):
    b, l, d = z.shape
    tm = min(l, 512)
    tn = min(d, 512)
    return pl.pallas_call(
        _hy_fwd_kernel,
        grid=(l // tm, b, d // tn),
        in_specs=[
            pl.BlockSpec((tm, l), lambda i, bi, j: (i, 0)),
            pl.BlockSpec((tm, l), lambda i, bi, j: (i, 0)),
            pl.BlockSpec((1, l, tn), lambda i, bi, j: (bi, 0, j)),
            pl.BlockSpec((tm, tn), lambda i, bi, j: (i, j)),
            pl.BlockSpec((tm, tn), lambda i, bi, j: (i, j)),
        ],
        out_specs=[pl.BlockSpec((1, tm, tn), lambda i, bi, j: (bi, i, j))] * 2,
        out_shape=[jax.ShapeDtypeStruct((b, l, d), BF16)] * 2,
        compiler_params=_params("parallel", "parallel", "parallel"),
        name="hyena_forward_dft",
    )(cos_kt, sin_kt, z, g_r, g_s)


def _hy_inv_kernel(ct_ref, st_ref, yr_ref, ys_ref, x0_ref, o_ref, *, inv_scale):
    y = _dot(ct_ref[...], yr_ref[0]) + _dot(st_ref[...], ys_ref[0])
    o_ref[0] = (y * inv_scale * x0_ref[0]).astype(o_ref.dtype)


def hyena_inverse_dft(cos_tk, sin_tk, y_r, y_s, x0):
    b, l, d = y_r.shape
    tm = min(l, 512)
    tn = min(d, 512)
    return pl.pallas_call(
        functools.partial(_hy_inv_kernel, inv_scale=1.0 / l),
        grid=(l // tm, b, d // tn),
        in_specs=[
            pl.BlockSpec((tm, l), lambda i, bi, j: (i, 0)),
            pl.BlockSpec((tm, l), lambda i, bi, j: (i, 0)),
            pl.BlockSpec((1, l, tn), lambda i, bi, j: (bi, 0, j)),
            pl.BlockSpec((1, l, tn), lambda i, bi, j: (bi, 0, j)),
            pl.BlockSpec((1, tm, tn), lambda i, bi, j: (bi, i, j)),
        ],
        out_specs=pl.BlockSpec((1, tm, tn), lambda i, bi, j: (bi, i, j)),
        out_shape=jax.ShapeDtypeStruct((b, l, d), BF16),
        compiler_params=_params("parallel", "parallel", "parallel"),
        name="hyena_inverse_dft",
    )(cos_tk, sin_tk, y_r, y_s, x0)


def _tile_rows(grp):
    span = grp.l if grp.per_batch else grp.t
    return next(tm for tm in (1024, 512, 256, 128) if span % tm == 0)


def kernel(x_prompt, x_sample, cache_k, cache_v, state_ret_fwd, state_ret_bwd, state_ml_C_fwd, state_ml_n_fwd, state_ml_m_fwd, state_ml_C_bwd, state_ml_n_bwd, state_ml_m_bwd, c, c_ctx, mod_w, mod_b, norm_mix_pre, norm_mix_post, norm_ffn_pre, norm_ffn_post, mlp_w1, mlp_w2, ret_w_in, ret_decay_fwd, ret_decay_bwd, ret_gn, ret_w_out, att_w_in, att_q_gain, att_k_gain, att_w_out, ml_w_in, ml_gate_b, ml_gn, ml_w_out, hy_w_in, hy_b_in, hy_conv_w, hy_conv_b, hy_f_w1, hy_f_b1, hy_f_w2, hy_f_b2, hy_f_w3, hy_sin_freq, hy_f_bias, hy_w_out):
    bp, lp, d = x_prompt.shape
    bs, ls, _ = x_sample.shape
    depth = mod_w.shape[0]
    n_mixers = 4
    mod_rows = 16
    assert 1 + bs <= mod_rows

    grp_p = Group(bp, lp, 0, False)
    grp_s = Group(bs, ls, 1, True)
    groups = (grp_p, grp_s)

    cond = jnp.concatenate([c_ctx[None, :], c, jnp.zeros((mod_rows - 1 - bs, d), F32)], axis=0)
    mod_all = adaln_all(cond, mod_w, mod_b)

    xs = [x_prompt.reshape(grp_p.t, d), x_sample.reshape(grp_s.t, d)]
    new_k = new_v = new_rf = new_rb = None
    new_ml = None

    for i in range(depth):
        mixer = i % n_mixers
        j = i // n_mixers
        mod3 = mod_all[i].reshape(mod_rows, 1, 6 * d)
        ys = []
        for gi, grp in enumerate(groups):
            x = xs[gi]
            tm = _tile_rows(grp)
            is_prompt = gi == 0
            if mixer == 0:
                w_in = ret_w_in[j].astype(BF16)
                proj = norm_matmul(x, norm_mix_pre[i], mod3, 0, 1, w_in, None, BF16, grp, tm)
                dec = jnp.stack([ret_decay_fwd[j], ret_decay_bwd[j]]).astype(F32)
                s0f = None if is_prompt else state_ret_fwd
                s0b = None if is_prompt else state_ret_bwd
                assert is_prompt or state_ret_fwd.shape[1] == 1
                res = retention_core(proj.reshape(grp.b, grp.l, -1), dec, ret_gn[j], s0f, s0b,
                                     grp.b, grp.l, emit_state=is_prompt)
                if is_prompt:
                    new_rf, new_rb = res[1], res[2]
                a = res[0].reshape(grp.t, -1)
                w_out = ret_w_out[j].astype(BF16)
            elif mixer == 1:
                w_in = att_w_in[j].astype(BF16)
                proj = norm_matmul(x, norm_mix_pre[i], mod3, 0, 1, w_in, None, F32 if is_prompt else BF16, grp, tm)
                proj = proj.reshape(grp.b, grp.l, -1)
                if is_prompt:
                    res = attention_core(proj, att_q_gain[j], att_k_gain[j], None, None, None,
                                         grp.b, grp.l, min(grp.l, 256), emit_kv=True)
                    new_k = res[1].reshape(grp.b, 1, grp.l, ATT_KV, ATT_HD)
                    new_v = res[2].reshape(grp.b, 1, grp.l, ATT_KV, ATT_HD)
                else:
                    assert cache_k.shape[1] == 1
                    ck = cache_k.reshape(grp.b, cache_k.shape[2], ATT_KV * ATT_HD)
                    cv = cache_v.reshape(grp.b, cache_v.shape[2], ATT_KV * ATT_HD)
                    res = attention_core(proj, att_q_gain[j], att_k_gain[j], _rope_tables(grp.l), ck, cv,
                                         grp.b, grp.l, min(grp.l, 256), emit_kv=False)
                a = res[0].reshape(grp.t, -1)
                w_out = att_w_out[j].astype(BF16)
            elif mixer == 2:
                hh = ML_HEADS
                n_main = ml_w_in.shape[2] - 4 * hh
                w_main = ml_w_in[j][:, :n_main].astype(BF16)
                w_gate = jnp.pad(ml_w_in[j][:, n_main:], ((0, 0), (0, LANES - 4 * hh))).astype(BF16)
                b_gate = jnp.pad(ml_gate_b[j], (0, LANES - 4 * hh))
                proj, gates = norm_matmul(x, norm_mix_pre[i], mod3, 0, 1, w_main, None, BF16, grp, tm,
                                          aux=(w_gate, b_gate))
                gates = gates[:, :4 * hh].reshape(grp.b, grp.l, 4, hh)
                n_chunks = grp.l // CHUNK
                gates_t = gates.transpose(0, 3, 2, 1).reshape(grp.b, hh, 4, n_chunks, CHUNK)
                gates_t = jnp.pad(gates_t, ((0, 0), (0, 0), (0, 0), (0, -n_chunks % 16), (0, 0)))
                state = None
                if not is_prompt:
                    assert state_ml_C_fwd.shape[1] == 1
                    state = (state_ml_C_fwd, state_ml_n_fwd, state_ml_m_fwd,
                             state_ml_C_bwd, state_ml_n_bwd, state_ml_m_bwd)
                res = mlstm_core(proj.reshape(grp.b, grp.l, -1), gates_t, ml_gn[j], state,
                                 grp.b, grp.l, emit_state=is_prompt)
                if is_prompt:
                    dqk = res[2].shape[-1]
                    new_ml = (res[1], res[2].reshape(grp.b, 1, hh, dqk), res[3].reshape(grp.b, 1, hh),
                              res[4], res[5].reshape(grp.b, 1, hh, dqk), res[6].reshape(grp.b, 1, hh))
                a = res[0].reshape(grp.t, -1)
                w_out = ml_w_out[j].astype(BF16)
            else:
                w_in = hy_w_in[j].astype(BF16)
                proj = norm_matmul(x, norm_mix_pre[i], mod3, 0, 1, w_in, hy_b_in[j], BF16, grp, tm)
                z, x0 = hyena_short_conv(proj.reshape(grp.b, grp.l, 3 * d), hy_conv_w[j], hy_conv_b[j],
                                         grp.b, grp.l, d)
                f_sum, f_dif = hyena_filters(grp.l, d, hy_f_w1[j], hy_f_b1[j], hy_f_w2[j], hy_f_b2[j],
                                             hy_f_w3[j], hy_sin_freq[j])
                cos_kt, sin_kt, cos_tk, sin_tk = _dft_tables(grp.l)
                g_r, g_s = hyena_filter_spectrum(cos_kt, sin_kt, f_sum, f_dif, hy_f_bias[j])
                y_r, y_s = hyena_forward_dft(cos_kt, sin_kt, z, g_r, g_s)
                a = hyena_inverse_dft(cos_tk, sin_tk, y_r, y_s, x0).reshape(grp.t, d)
                w_out = hy_w_out[j].astype(BF16)
            x, h_mlp = matmul_resnorm(a, w_out, x, mod3, norm_mix_post[i], norm_ffn_pre[i], grp, tm)
            x = mlp_block(h_mlp, x, norm_ffn_post[i], mod3,
                          mlp_w1, mlp_w2, i, grp, tm, MLP_TILE_F)
            ys.append(x)
        xs = ys

    y_prompt = xs[0].reshape(bp, lp, d)
    y_sample = xs[1].reshape(bs, ls, d)
    return (y_prompt, y_sample, new_k, new_v, new_rf, new_rb) + tuple(new_ml)
```

```python
import functools
import math

import jax
import jax.numpy as jnp
import numpy as np
from jax import lax
from jax.experimental import pallas as pl
from jax.experimental.pallas import tpu as pltpu

F32 = jnp.float32
BF16 = jnp.bfloat16

EPS = 1e-6
CHUNK = 128
RET_CHUNK = 256
GRID_W = 64
ROPE_THETA = 10000.0

RET_HEADS = 4
ATT_HEADS = 8
ATT_KV = 2
ATT_HD = 128
ML_HEADS = 4

HY_BANDS = 16
HY_SHORT = 3
HY_FAST_DECAY = 0.3
HY_SLOW_DECAY = 1.5
HY_TARGET = 1e-2

VMEM_LIMIT_BYTES = 56 * 1024 * 1024
LANES = 128
PROJ_TILES_N = (3072, 1536)
MLP_TILE_F = 1024


def _params(*sem):
    return pltpu.CompilerParams(dimension_semantics=sem, vmem_limit_bytes=VMEM_LIMIT_BYTES)


def _dot(a, b):
    return jnp.dot(a, b, preferred_element_type=F32)


def _dot_nt(a, b):
    return lax.dot_general(a, b, (((1,), (1,)), ((), ())), preferred_element_type=F32)


def _dot_tn(a, b):
    return lax.dot_general(a, b, (((0,), (0,)), ((), ())), preferred_element_type=F32)


def _rms(x, g):
    return x * lax.rsqrt(jnp.mean(x * x, axis=-1, keepdims=True) + EPS) * g


def _log_sigmoid(x):
    return jnp.minimum(x, 0.0) - jnp.log1p(jnp.exp(-jnp.abs(x)))


def _chunk_offset(ch, c):
    return ch * c if isinstance(ch, int) else pl.multiple_of(ch * c, c)


def _pipelined_scan(n_chunks, weigh_both, advance_both, fin):
    if n_chunks == 1:
        weigh_both(0)
        advance_both(0)
        fin(0)
        return
    assert n_chunks == 2 or n_chunks % 4 == 0

    def advance_pair(i):
        advance_both(i)
        advance_both(i + 1)

    def fin_pair(i):
        chunks = (i, n_chunks - 1 - i, i + 1, n_chunks - 2 - i)
        for ch in (dict.fromkeys(chunks) if isinstance(i, int) else chunks):
            fin(ch)

    def body(p, carry):
        advance_pair(2 * p)
        weigh_both(2 * p + 2)
        weigh_both(2 * p + 3)
        return carry

    def body_fin(p, carry):
        advance_pair(2 * p)
        fin_pair(2 * p)
        weigh_both(2 * p + 2)
        weigh_both(2 * p + 3)
        return carry

    pairs = n_chunks // 2
    weigh_both(0)
    weigh_both(1)
    if pairs > 1:
        lax.fori_loop(0, pairs // 2, body, 0)
        lax.fori_loop(pairs // 2, pairs - 1, body_fin, 0)
    advance_pair(n_chunks - 2)
    fin_pair(n_chunks - 2)


class Group:
    def __init__(self, b, l, row0, per_batch):
        self.b, self.l, self.row0, self.per_batch = b, l, row0, per_batch
        self.t = b * l

    def mod_spec(self, chunk, tm, d):
        row0, per_batch, l = self.row0, self.per_batch, self.l
        if per_batch:
            return pl.BlockSpec((1, 1, d), lambda i, *_: (row0 + (i * tm) // l, 0, chunk))
        return pl.BlockSpec((1, 1, d), lambda i, *_: (row0, 0, chunk))


def _adaln_kernel(c_ref, w_ref, b_ref, o_ref):
    s = jax.nn.silu(c_ref[...])
    o_ref[0] = _dot(s.astype(BF16), w_ref[0].astype(BF16)) + b_ref[0]


def adaln_all(cond, mod_w, mod_b):
    depth, d, n = mod_w.shape
    rows = cond.shape[0]
    tn = 768
    return pl.pallas_call(
        _adaln_kernel,
        grid=(depth, n // tn),
        in_specs=[
            pl.BlockSpec((rows, d), lambda l, j: (0, 0)),
            pl.BlockSpec((1, d, tn), lambda l, j: (l, 0, j)),
            pl.BlockSpec((1, 1, tn), lambda l, j: (l, 0, j)),
        ],
        out_specs=pl.BlockSpec((1, rows, tn), lambda l, j: (l, 0, j)),
        out_shape=jax.ShapeDtypeStruct((depth, rows, n), F32),
        compiler_params=_params("parallel", "parallel"),
        name="adaln",
    )(cond, mod_w, mod_b.reshape(depth, 1, n))


def _norm_mm_kernel(x_ref, g_ref, sh_ref, sc_ref, w_ref, *rest, has_bias, has_aux):
    rest = list(rest)
    b_ref = rest.pop(0) if has_bias else None
    wa_ref, ba_ref = (rest.pop(0), rest.pop(0)) if has_aux else (None, None)
    o_ref = rest.pop(0)
    oa_ref = rest.pop(0) if has_aux else None
    (h_scr,) = rest

    @pl.when(pl.program_id(1) == 0)
    def _():
        y = _rms(x_ref[...], g_ref[...])
        h = (y * (1.0 + sc_ref[0]) + sh_ref[0]).astype(BF16)
        h_scr[...] = h
        if has_aux:
            oa_ref[...] = _dot(h, wa_ref[...]) + ba_ref[...]

    acc = _dot(h_scr[...], w_ref[...])
    if has_bias:
        acc = acc + b_ref[...]
    o_ref[...] = acc.astype(o_ref.dtype)


def norm_matmul(x, gain, mod3, sh_idx, sc_idx, w, bias, out_dtype, grp, tm, aux=None):
    t, d = x.shape
    n = w.shape[1]
    tn = next(c for c in PROJ_TILES_N if n % c == 0)
    in_specs = [
        pl.BlockSpec((tm, d), lambda i, j: (i, 0)),
        pl.BlockSpec((1, d), lambda i, j: (0, 0)),
        grp.mod_spec(sh_idx, tm, d),
        grp.mod_spec(sc_idx, tm, d),
        pl.BlockSpec((d, tn), lambda i, j: (0, j)),
    ]
    args = [x, gain.reshape(1, d), mod3, mod3, w]
    if bias is not None:
        in_specs.append(pl.BlockSpec((1, tn), lambda i, j: (0, j)))
        args.append(bias.reshape(1, n))
    out_specs = [pl.BlockSpec((tm, tn), lambda i, j: (i, j))]
    out_shape = [jax.ShapeDtypeStruct((t, n), out_dtype)]
    if aux is not None:
        w_aux, b_aux = aux
        na = w_aux.shape[1]
        in_specs += [pl.BlockSpec((d, na), lambda i, j: (0, 0)), pl.BlockSpec((1, na), lambda i, j: (0, 0))]
        args += [w_aux, b_aux.reshape(1, na)]
        out_specs.append(pl.BlockSpec((tm, na), lambda i, j: (i, 0)))
        out_shape.append(jax.ShapeDtypeStruct((t, na), F32))
    res = pl.pallas_call(
        functools.partial(_norm_mm_kernel, has_bias=bias is not None, has_aux=aux is not None),
        grid=(t // tm, n // tn),
        in_specs=in_specs,
        out_specs=out_specs,
        out_shape=out_shape,
        scratch_shapes=[pltpu.VMEM((tm, d), BF16)],
        compiler_params=_params("parallel", "arbitrary"),
        name="norm_matmul",
    )(*args)
    return res if aux is not None else res[0]


def _mm_res_kernel(a_ref, w_ref, x_ref, gate_ref, pg_ref, ng_ref, sh_ref, sc_ref, o_ref, h_ref):
    y = _dot(a_ref[...], w_ref[...])
    x1 = x_ref[...] + gate_ref[0] * _rms(y, pg_ref[...])
    o_ref[...] = x1
    h_ref[...] = (_rms(x1, ng_ref[...]) * (1.0 + sc_ref[0]) + sh_ref[0]).astype(h_ref.dtype)


def matmul_resnorm(a, w, x, mod3, post_gain, next_gain, grp, tm):
    t, k = a.shape
    d = w.shape[1]
    row = pl.BlockSpec((tm, d), lambda i: (i, 0))
    vec = pl.BlockSpec((1, d), lambda i: (0, 0))
    return pl.pallas_call(
        _mm_res_kernel,
        grid=(t // tm,),
        in_specs=[
            pl.BlockSpec((tm, k), lambda i: (i, 0)),
            pl.BlockSpec((k, d), lambda i: (0, 0)),
            row,
            grp.mod_spec(2, tm, d),
            vec,
            vec,
            grp.mod_spec(3, tm, d),
            grp.mod_spec(4, tm, d),
        ],
        out_specs=[row, row],
        out_shape=[jax.ShapeDtypeStruct((t, d), F32), jax.ShapeDtypeStruct((t, d), BF16)],
        compiler_params=_params("parallel"),
        name="out_proj",
    )(a, w, x, mod3, post_gain.reshape(1, d), next_gain.reshape(1, d), mod3, mod3)


def _mlp_kernel(h_ref, x_ref, w1_ref, w2_ref, gate_ref, pg_ref, o_ref, acc_scr):
    j = pl.program_id(1)

    @pl.when(j == 0)
    def _():
        acc_scr[...] = jnp.zeros_like(acc_scr)

    u = _dot(h_ref[...], w1_ref[...].astype(BF16))
    u = jnp.square(jnp.maximum(u, 0.0)).astype(BF16)
    acc_scr[...] += _dot(u, w2_ref[...].astype(BF16))

    @pl.when(j == pl.num_programs(1) - 1)
    def _():
        o_ref[...] = x_ref[...] + gate_ref[0] * _rms(acc_scr[...], pg_ref[...])


def mlp_block(h, x, post_gain, mod3, w1, w2, layer, grp, tm, tf):
    t, d = x.shape
    f = w1.shape[2]
    return pl.pallas_call(
        _mlp_kernel,
        grid=(t // tm, f // tf),
        in_specs=[
            pl.BlockSpec((tm, d), lambda i, j: (i, 0)),
            pl.BlockSpec((tm, d), lambda i, j: (i, 0)),
            pl.BlockSpec((None, d, tf), lambda i, j: (layer, 0, j)),
            pl.BlockSpec((None, tf, d), lambda i, j: (layer, j, 0)),
            grp.mod_spec(5, tm, d),
            pl.BlockSpec((1, d), lambda i, j: (0, 0)),
        ],
        out_specs=pl.BlockSpec((tm, d), lambda i, j: (i, 0)),
        out_shape=jax.ShapeDtypeStruct((t, d), F32),
        scratch_shapes=[pltpu.VMEM((tm, d), F32)],
        compiler_params=_params("parallel", "arbitrary"),
        name="mlp",
    )(h, x, w1, w2, mod3, post_gain.reshape(1, d))


def _ret_kernel(dec_ref, q_ref, k_ref, v_ref, g_ref, gn_ref, *rest, n_chunks, has_state, emit_state):
    rest = list(rest)
    s0f_ref = s0b_ref = sf_ref = sb_ref = None
    if has_state:
        s0f_ref, s0b_ref = rest[:2]
        rest = rest[2:]
    o_ref = rest.pop(0)
    if emit_state:
        sf_ref, sb_ref = rest[:2]
        rest = rest[2:]
    st_f, st_b, of_scr, ob_scr, sw_f, sw_b, kw_f, kw_b = rest

    c = RET_CHUNK
    dk = q_ref.shape[-1]
    h = pl.program_id(1)
    lg_f = _log_sigmoid(jnp.full((1, 1), dec_ref[0, h], F32))
    lg_b = _log_sigmoid(jnp.full((1, 1), dec_ref[1, h], F32))
    ri = lax.broadcasted_iota(jnp.int32, (c, c), 0)
    ci = lax.broadcasted_iota(jnp.int32, (c, c), 1)
    rel = (ri - ci).astype(F32)
    intra_f = jnp.where(rel >= 0, jnp.exp(lg_f * jnp.maximum(rel, 0.0)), 0.0)
    intra_b = jnp.where(rel <= 0, jnp.exp(lg_b * jnp.maximum(-rel, 0.0)), 0.0)
    idx = lax.broadcasted_iota(jnp.int32, (c, 1), 0).astype(F32)
    qdec_f = jnp.exp(lg_f * (idx + 1.0))
    kdec_f = jnp.exp(lg_f * (c - 1.0 - idx))
    qdec_b = jnp.exp(lg_b * (c - idx))
    kdec_b = jnp.exp(lg_b * idx)
    cdec_f = jnp.exp(lg_f * c)
    cdec_b = jnp.exp(lg_b * c)
    q_scale = dk ** -0.5

    if has_state:
        st_f[...] = s0f_ref[0, 0, 0]
        st_b[...] = s0b_ref[0, 0, 0]
    else:
        st_f[...] = jnp.zeros_like(st_f)
        st_b[...] = jnp.zeros_like(st_b)

    def weigh(ch, intra, kdec, sw, kw_s):
        rows = pl.ds(_chunk_offset(ch, c), c)
        kc = k_ref[0, rows, :]
        sw[ch] = (_dot_nt(q_ref[0, rows, :], kc) * (intra * q_scale)).astype(BF16)
        kw_s[ch] = (kc.astype(F32) * kdec).T.astype(BF16)

    def advance(ch, st, out_scr, qdec, cdec, sw, kw_s):
        rows = pl.ds(_chunk_offset(ch, c), c)
        qc = q_ref[0, rows, :]
        vc = v_ref[0, rows, :]
        s_old = st[...]
        out_scr[rows, :] = _dot(sw[ch], vc) + _dot(qc, s_old.astype(BF16)) * (qdec * q_scale)
        st[...] = s_old * cdec + _dot(kw_s[ch], vc)

    def fin(ch):
        rows = pl.ds(_chunk_offset(ch, c), c)
        o = _rms(of_scr[rows, :] + ob_scr[rows, :], gn_ref[...])
        o_ref[0, rows, :] = (jax.nn.silu(g_ref[0, rows, :].astype(F32)) * o).astype(o_ref.dtype)

    def weigh_both(i):
        weigh(i, intra_f, kdec_f, sw_f, kw_f)
        weigh(n_chunks - 1 - i, intra_b, kdec_b, sw_b, kw_b)

    def advance_both(i):
        advance(i, st_f, of_scr, qdec_f, cdec_f, sw_f, kw_f)
        advance(n_chunks - 1 - i, st_b, ob_scr, qdec_b, cdec_b, sw_b, kw_b)

    _pipelined_scan(n_chunks, weigh_both, advance_both, fin)

    if emit_state:
        sf_ref[0, 0, 0] = st_f[...]
        sb_ref[0, 0, 0] = st_b[...]


def retention_core(proj, dec, gn, s0f, s0b, b, l, emit_state):
    hh = RET_HEADS
    width = proj.shape[-1]
    dk = width // (6 * hh)
    dv = 2 * dk
    has_state = s0f is not None
    n_chunks = l // RET_CHUNK
    in_specs = [
        pl.BlockSpec(memory_space=pltpu.SMEM),
        pl.BlockSpec((1, l, dk), lambda bi, h: (bi, 0, h)),
        pl.BlockSpec((1, l, dk), lambda bi, h: (bi, 0, hh + h)),
        pl.BlockSpec((1, l, dv), lambda bi, h: (bi, 0, hh + h)),
        pl.BlockSpec((1, l, dv), lambda bi, h: (bi, 0, 2 * hh + h)),
        pl.BlockSpec((1, dv), lambda bi, h: (0, h)),
    ]
    args = [dec, proj, proj, proj, proj, gn.reshape(1, hh * dv)]
    if has_state:
        st_spec = pl.BlockSpec((1, 1, 1, dk, dv), lambda bi, h: (bi, 0, h, 0, 0))
        in_specs += [st_spec, st_spec]
        args += [s0f, s0b]
    out_specs = [pl.BlockSpec((1, l, dv), lambda bi, h: (bi, 0, h))]
    out_shape = [jax.ShapeDtypeStruct((b, l, hh * dv), BF16)]
    if emit_state:
        so_spec = pl.BlockSpec((1, 1, 1, dk, dv), lambda bi, h: (bi, 0, h, 0, 0))
        out_specs += [so_spec, so_spec]
        out_shape += [jax.ShapeDtypeStruct((b, 1, hh, dk, dv), F32)] * 2
    return pl.pallas_call(
        functools.partial(_ret_kernel, n_chunks=n_chunks, has_state=has_state, emit_state=emit_state),
        grid=(b, hh),
        in_specs=in_specs,
        out_specs=out_specs,
        out_shape=out_shape,
        scratch_shapes=[pltpu.VMEM((dk, dv), F32), pltpu.VMEM((dk, dv), F32),
                        pltpu.VMEM((l, dv), F32), pltpu.VMEM((l, dv), F32)]
                       + [pltpu.VMEM((n_chunks, RET_CHUNK, RET_CHUNK), BF16)] * 2
                       + [pltpu.VMEM((n_chunks, dk, RET_CHUNK), BF16)] * 2,
        compiler_params=_params("parallel", "parallel"),
        name="retention",
    )(*args)


def _rope_rot(x, cos_t, sin_t):
    lane = lax.broadcasted_iota(jnp.int32, x.shape, x.ndim - 1)
    nxt = pltpu.roll(x, LANES - 1, axis=x.ndim - 1)
    prv = pltpu.roll(x, 1, axis=x.ndim - 1)
    swapped = jnp.where(jnp.bitwise_and(lane, 1) == 0, nxt, prv)
    return x * cos_t + swapped * sin_t


def _att_kernel(q_ref, k_ref, v_ref, qg_ref, kg_ref, *rest, rope, has_cache, emit_kv, kv_heads, groups):
    rest = list(rest)
    cosq_ref = sinq_ref = cosk_ref = sink_ref = ck_ref = cv_ref = kn_ref = vo_ref = None
    if rope:
        cosq_ref, sinq_ref, cosk_ref, sink_ref = rest[:4]
        rest = rest[4:]
    if has_cache:
        ck_ref, cv_ref = rest[:2]
        rest = rest[2:]
    w_ref, x_ref, gate_ref, pg_ref, ng_ref, sh_ref, sc_ref, x1_ref, h_ref = rest[:9]
    rest = rest[9:]
    if emit_kv:
        kn_ref, vo_ref = rest[:2]
        rest = rest[2:]
    k_scr, v_scr, o_scr = rest
    hd = ATT_HD
    exp2_scale = hd ** -0.5 * math.log2(math.e)

    @pl.when(pl.program_id(1) == 0)
    def _():
        for kh in range(kv_heads):
            cols = slice(kh * hd, (kh + 1) * hd)
            kn = _rms(k_ref[0, :, cols].astype(F32), kg_ref[...])
            if emit_kv:
                kn_ref[0, :, cols] = kn
            if rope:
                kn = _rope_rot(kn, cosk_ref[...], sink_ref[...])
            k_scr[kh] = kn.astype(BF16)
            v_scr[kh, :, :hd] = v_ref[0, :, cols].astype(BF16)
            v_scr[kh, :, hd:] = jnp.ones((v_scr.shape[1], hd), BF16)
        if emit_kv:
            vo_ref[0] = v_ref[0]

    if has_cache:
        ck = [ck_ref[0, :, kh * hd:(kh + 1) * hd].astype(BF16) for kh in range(kv_heads)]
        cv = [jnp.concatenate([cv_ref[0, :, kh * hd:(kh + 1) * hd].astype(BF16),
                               jnp.ones((cv_ref.shape[1], hd), BF16)], axis=1) for kh in range(kv_heads)]

    def scores(head):
        kh = head // groups
        qn = _rms(q_ref[0, :, head * hd:(head + 1) * hd].astype(F32), qg_ref[...])
        if rope:
            qn = _rope_rot(qn, cosq_ref[...], sinq_ref[...])
        qb = qn.astype(BF16)
        return _dot_nt(qb, k_scr[kh]), (_dot_nt(qb, ck[kh]) if has_cache else None)

    heads = kv_heads * groups
    nxt = scores(0)
    for head in range(heads):
        kh = head // groups
        s1, s2 = nxt
        if head + 1 < heads:
            nxt = scores(head + 1)
        m = jnp.max(s1, axis=-1, keepdims=True)
        if has_cache:
            m = jnp.maximum(m, jnp.max(s2, axis=-1, keepdims=True))
        p1 = jnp.exp2((s1 - m) * exp2_scale)
        nd = _dot(p1.astype(BF16), v_scr[kh])
        if has_cache:
            p2 = jnp.exp2((s2 - m) * exp2_scale)
            nd = nd + _dot(p2.astype(BF16), cv[kh])
        o_scr[:, head * hd:(head + 1) * hd] = (nd[:, :hd] * (1.0 / nd[:, hd:])).astype(o_scr.dtype)

    y = _dot(o_scr[...], w_ref[...])
    x1 = x_ref[0] + gate_ref[0] * _rms(y, pg_ref[...])
    x1_ref[0] = x1
    h_ref[0] = (_rms(x1, ng_ref[...]) * (1.0 + sc_ref[0]) + sh_ref[0]).astype(h_ref.dtype)


def attention_block(proj, x, q_gain, k_gain, rope_tabs, cache_k, cache_v, w_out, mod3, post_gain, next_gain,
                    grp, tq, emit_kv):
    hd, kv, heads = ATT_HD, ATT_KV, ATT_HEADS
    b, l, d = x.shape
    rope = rope_tabs is not None
    has_cache = cache_k is not None
    kv_blk = heads // kv
    assert heads % kv == 0
    row0, per_batch = grp.row0, grp.per_batch

    def mod_spec(chunk):
        return pl.BlockSpec((1, 1, d), lambda bi, qi: (row0 + bi if per_batch else row0, 0, chunk))

    vec = lambda n: pl.BlockSpec((1, n), lambda bi, qi: (0, 0))
    in_specs = [
        pl.BlockSpec((1, tq, heads * hd), lambda bi, qi: (bi, qi, 0)),
        pl.BlockSpec((1, l, kv * hd), lambda bi, qi: (bi, 0, kv_blk)),
        pl.BlockSpec((1, l, kv * hd), lambda bi, qi: (bi, 0, kv_blk + 1)),
        vec(hd),
        vec(hd),
    ]
    args = [proj, proj, proj, q_gain.reshape(1, hd), k_gain.reshape(1, hd)]
    if rope:
        cos_t, sin_t = rope_tabs
        in_specs += [
            pl.BlockSpec((tq, hd), lambda bi, qi: (qi, 0)),
            pl.BlockSpec((tq, hd), lambda bi, qi: (qi, 0)),
            pl.BlockSpec((l, hd), lambda bi, qi: (0, 0)),
            pl.BlockSpec((l, hd), lambda bi, qi: (0, 0)),
        ]
        args += [cos_t, sin_t, cos_t, sin_t]
    if has_cache:
        past = cache_k.shape[1]
        c_spec = pl.BlockSpec((1, past, kv * hd), lambda bi, qi: (bi, 0, 0))
        in_specs += [c_spec, c_spec]
        args += [cache_k, cache_v]
    row = pl.BlockSpec((1, tq, d), lambda bi, qi: (bi, qi, 0))
    in_specs += [pl.BlockSpec((heads * hd, d), lambda bi, qi: (0, 0)), row,
                 mod_spec(2), vec(d), vec(d), mod_spec(3), mod_spec(4)]
    args += [w_out, x, mod3, post_gain.reshape(1, d), next_gain.reshape(1, d), mod3, mod3]
    out_specs = [row, row]
    out_shape = [jax.ShapeDtypeStruct((b, l, d), F32), jax.ShapeDtypeStruct((b, l, d), BF16)]
    if emit_kv:
        kv_spec = pl.BlockSpec((1, l, kv * hd), lambda bi, qi: (bi, 0, 0))
        out_specs += [kv_spec, kv_spec]
        out_shape += [jax.ShapeDtypeStruct((b, l, kv * hd), F32)] * 2
    return pl.pallas_call(
        functools.partial(_att_kernel, rope=rope, has_cache=has_cache, emit_kv=emit_kv,
                          kv_heads=kv, groups=heads // kv),
        grid=(b, l // tq),
        in_specs=in_specs,
        out_specs=out_specs,
        out_shape=out_shape,
        scratch_shapes=[pltpu.VMEM((kv, l, hd), BF16), pltpu.VMEM((kv, l, 2 * hd), BF16),
                        pltpu.VMEM((tq, heads * hd), BF16)],
        compiler_params=_params("parallel", "arbitrary"),
        name="attention",
    )(*args)


def _rope_tables(l):
    rows = l // GRID_W
    row = jnp.repeat(jnp.arange(rows, dtype=F32), GRID_W)
    col = jnp.tile(jnp.arange(GRID_W, dtype=F32), rows)
    half = ATT_HD // 2
    inv = ROPE_THETA ** (-jnp.arange(0, half, 2, dtype=F32) / half)
    ang = jnp.concatenate([row[:, None] * inv, col[:, None] * inv], axis=-1)
    cos_t = jnp.repeat(jnp.cos(ang), 2, axis=-1)
    sin_h = jnp.sin(ang)
    sin_t = jnp.stack([-sin_h, sin_h], axis=-1).reshape(l, ATT_HD)
    return cos_t, sin_t


def _ml_kernel(q_ref, k_ref, v_ref, og_ref, gt_ref, gn_ref, *rest, n_chunks, has_state, emit_state):
    rest = list(rest)
    c0f_ref = n0f_ref = m0f_ref = c0b_ref = n0b_ref = m0b_ref = None
    if has_state:
        c0f_ref, n0f_ref, m0f_ref, c0b_ref, n0b_ref, m0b_ref = rest[:6]
        rest = rest[6:]
    o_ref = rest.pop(0)
    outs = None
    if emit_state:
        outs = rest[:6]
        rest = rest[6:]
    cm_f, cm_b, nv_f, nv_b, hf_scr, hb_scr, fl_f, fl_b, br_f, br_b, mo_f, mo_b, mn_f, mn_b = rest[:14]
    sw_f, sw_b, kw_f, kw_b = rest[14:18]
    vt_f, vt_b = rest[18:23], rest[23:28]

    c = CHUNK
    dqk = q_ref.shape[-1]
    k_scale = dqk ** -0.5
    ri = lax.broadcasted_iota(jnp.int32, (c, c), 0)
    ci = lax.broadcasted_iota(jnp.int32, (c, c), 1)
    mask_f = ci <= ri
    mask_b = ci >= ri
    assert c == LANES and dqk == LANES
    dv_tiles = v_ref.shape[-1] // LANES

    def bf3(m):
        m = jnp.where(m, 1.0, 0.0).astype(BF16)
        return jnp.concatenate([m, m, m], axis=1), jnp.concatenate([m, m, m], axis=0)

    mf3, mf3_t = bf3(mask_f)
    mb3, mb3_t = bf3(mask_b)
    eye3, _ = bf3(ci == ri)

    def split3(x):
        hi = x.astype(BF16)
        r1 = x - hi.astype(F32)
        mid = r1.astype(BF16)
        lo = (r1 - mid.astype(F32)).astype(BF16)
        return jnp.concatenate([hi, mid, lo], axis=1)

    def wide(x):
        return jnp.concatenate([x] * dv_tiles, axis=1)

    if has_state:
        cm_f[...] = c0f_ref[0, 0, 0]
        cm_b[...] = c0b_ref[0, 0, 0]
        nv_f[...] = n0f_ref[0, 0]
        nv_b[...] = n0b_ref[0, 0]
        m0_f = jnp.broadcast_to(m0f_ref[0, 0], (1, LANES))
        m0_b = jnp.broadcast_to(m0b_ref[0, 0], (1, LANES))
    else:
        for r in (cm_f, cm_b, nv_f, nv_b):
            r[...] = jnp.zeros_like(r)
        m0_f = m0_b = jnp.zeros((1, LANES), F32)

    def gate_pass(row_i, row_f, m3_t, m0, fl, br, mo, mn, order):
        f_all = _log_sigmoid(gt_ref[0, 0, row_f])
        fl[...] = f_all
        b_all = _dot(split3(f_all), m3_t)
        br[...] = b_all
        b_end = jnp.sum(f_all, axis=-1, keepdims=True)
        w_max = jnp.max(b_end - b_all + gt_ref[0, 0, row_i], axis=-1, keepdims=True)
        m = m0
        for r in order:
            mo[r:r + 1, :] = m
            m = jnp.maximum(b_end[r:r + 1, :] + m, w_max[r:r + 1, :])
            mn[r:r + 1, :] = m
        return m

    m_last_f = gate_pass(0, 1, mb3_t, m0_f, fl_f, br_f, mo_f, mn_f, range(n_chunks))
    m_last_b = gate_pass(2, 3, mf3_t, m0_b, fl_b, br_b, mo_b, mn_b, reversed(range(n_chunks)))

    def weigh(ch, mask, m3, fl, br, mo, mn, row_i, sw, kw_s, wx_s, rs_s, em_s, cd_s, ks_s):
        rows = pl.ds(_chunk_offset(ch, c), c)
        one = pl.ds(ch, 1)
        qc = q_ref[0, rows, :]
        kc = k_ref[0, rows, :]
        i_row = gt_ref[0, 0, row_i, one, :]
        f_row = fl[one, :]
        b_row = br[one, :]
        m_old = mo[one, :]
        m_new = mn[one, :]
        b_q = _dot_nt(m3, jnp.broadcast_to(split3(f_row), (LANES, 3 * c)))
        i_q = _dot_nt(eye3, jnp.broadcast_to(split3(i_row), (LANES, 3 * c)))
        dlog = jnp.where(mask, b_q - b_row + i_row, -jnp.inf)
        inter = b_q + m_old
        m_q = jnp.maximum(inter, jnp.max(dlog, axis=-1, keepdims=True))
        s = _dot_nt(qc, kc) * (jnp.exp(dlog - m_q) * k_scale)
        sw[ch] = s.astype(BF16)
        rs_s[rows, :] = jnp.broadcast_to(jnp.sum(s, axis=-1, keepdims=True), (c, LANES))
        wx_s[rows, :] = jnp.exp(inter - m_q)
        em_s[rows, :] = jnp.exp(-m_q)
        b_end = jnp.sum(f_row, axis=-1, keepdims=True)
        wlog = b_end - b_q + i_q
        cd_s[one, :] = jnp.exp(b_end + m_old - m_new)
        kw = kc.astype(F32) * (jnp.exp(wlog - m_new) * k_scale)
        kw_s[ch] = kw.T.astype(BF16)
        ks_s[one, :] = jnp.sum(kw, axis=0, keepdims=True)

    def advance(ch, cm, nv, out_scr, sw, kw_s, wx_s, rs_s, em_s, cd_s, ks_s):
        rows = pl.ds(_chunk_offset(ch, c), c)
        one = pl.ds(ch, 1)
        qc = q_ref[0, rows, :]
        vc = v_ref[0, rows, :]
        c_old = cm[...]
        n_old = nv[...]
        w_x = wx_s[rows, :]
        num = _dot(sw[ch], vc) + _dot(qc, c_old.astype(BF16)) * wide(w_x)
        qn = _dot_nt(qc, jnp.broadcast_to(n_old, (LANES, dqk)).astype(BF16))
        den = jnp.maximum(jnp.abs(rs_s[rows, :] + qn * w_x), em_s[rows, :])
        out_scr[rows, :] = num * wide(1.0 / den)
        carry_dec = cd_s[one, :]
        cm[...] = c_old * wide(carry_dec) + _dot(kw_s[ch], vc)
        nv[...] = n_old * carry_dec + ks_s[one, :]

    def fin(ch):
        rows = pl.ds(_chunk_offset(ch, c), c)
        hn = _rms(hf_scr[rows, :] + hb_scr[rows, :], gn_ref[...])
        o_ref[0, rows, :] = (jax.nn.sigmoid(og_ref[0, rows, :].astype(F32)) * hn).astype(o_ref.dtype)

    tmp_f = (sw_f, kw_f) + tuple(vt_f)
    tmp_b = (sw_b, kw_b) + tuple(vt_b)

    def weigh_both(i):
        weigh(i, mask_f, mf3, fl_f, br_f, mo_f, mn_f, 0, *tmp_f)
        weigh(n_chunks - 1 - i, mask_b, mb3, fl_b, br_b, mo_b, mn_b, 2, *tmp_b)

    def advance_both(i):
        advance(i, cm_f, nv_f, hf_scr, *tmp_f)
        advance(n_chunks - 1 - i, cm_b, nv_b, hb_scr, *tmp_b)

    _pipelined_scan(n_chunks, weigh_both, advance_both, fin)

    if emit_state:
        for dst, src in zip(outs[0::3], (cm_f, cm_b)):
            dst[0, 0, 0] = src[...]
        for dst, src in zip(outs[1::3], (nv_f, nv_b)):
            dst[0, 0] = src[...]
        for dst, m_last in zip(outs[2::3], (m_last_f, m_last_b)):
            dst[0, 0] = m_last[:, 0:1]


def mlstm_core(proj, gates_t, gn, state, b, l, emit_state):
    hh = ML_HEADS
    width = proj.shape[-1]
    dqk = width // (6 * hh)
    dv = 2 * dqk
    has_state = state is not None
    n_chunks = l // CHUNK
    chunk_rows = gates_t.shape[3]
    in_specs = [
        pl.BlockSpec((1, l, dqk), lambda bi, h: (bi, 0, h)),
        pl.BlockSpec((1, l, dqk), lambda bi, h: (bi, 0, hh + h)),
        pl.BlockSpec((1, l, dv), lambda bi, h: (bi, 0, hh + h)),
        pl.BlockSpec((1, l, dv), lambda bi, h: (bi, 0, 2 * hh + h)),
        pl.BlockSpec((1, 1, 4, chunk_rows, CHUNK), lambda bi, h: (bi, h, 0, 0, 0)),
        pl.BlockSpec((1, dv), lambda bi, h: (0, h)),
    ]
    args = [proj, proj, proj, proj, gates_t, gn.reshape(1, hh * dv)]
    c_spec_in = pl.BlockSpec((1, 1, 1, dqk, dv), lambda bi, h: (bi, 0, h, 0, 0))
    n_spec = pl.BlockSpec((1, 1, 1, dqk), lambda bi, h: (bi, h, 0, 0))
    m_spec = pl.BlockSpec((1, 1, 1, 1), lambda bi, h: (bi, h, 0, 0))
    if has_state:
        cf, nf, mf, cb, nb, mb = state
        in_specs += [c_spec_in, n_spec, m_spec, c_spec_in, n_spec, m_spec]
        args += [cf, nf.reshape(b, hh, 1, dqk), mf.reshape(b, hh, 1, 1),
                 cb, nb.reshape(b, hh, 1, dqk), mb.reshape(b, hh, 1, 1)]
    out_specs = [pl.BlockSpec((1, l, dv), lambda bi, h: (bi, 0, h))]
    out_shape = [jax.ShapeDtypeStruct((b, l, hh * dv), BF16)]
    if emit_state:
        out_specs += [c_spec_in, n_spec, m_spec] * 2
        out_shape += [jax.ShapeDtypeStruct((b, 1, hh, dqk, dv), F32),
                      jax.ShapeDtypeStruct((b, hh, 1, dqk), F32),
                      jax.ShapeDtypeStruct((b, hh, 1, 1), F32)] * 2
    return pl.pallas_call(
        functools.partial(_ml_kernel, n_chunks=n_chunks, has_state=has_state, emit_state=emit_state),
        grid=(b, hh),
        in_specs=in_specs,
        out_specs=out_specs,
        out_shape=out_shape,
        scratch_shapes=[pltpu.VMEM((dqk, dv), F32), pltpu.VMEM((dqk, dv), F32),
                        pltpu.VMEM((1, dqk), F32), pltpu.VMEM((1, dqk), F32),
                        pltpu.VMEM((l, dv), F32), pltpu.VMEM((l, dv), F32)]
                       + [pltpu.VMEM((chunk_rows, CHUNK), F32)] * 4 + [pltpu.VMEM((chunk_rows, LANES), F32)] * 4
                       + [pltpu.VMEM((n_chunks, CHUNK, CHUNK), BF16)] * 2 + [pltpu.VMEM((n_chunks, dqk, CHUNK), BF16)] * 2
                       + [pltpu.VMEM((l, LANES), F32)] * 3 + [pltpu.VMEM((chunk_rows, LANES), F32)] * 2
                       + [pltpu.VMEM((l, LANES), F32)] * 3 + [pltpu.VMEM((chunk_rows, LANES), F32)] * 2,
        compiler_params=_params("parallel", "parallel"),
        name="mlstm",
    )(*args)


def _hy_filter_kernel(feat_ref, w1_ref, b1_ref, w2_ref, b2_ref, w3_ref, fr_ref, win_ref, sum_ref, dif_ref):
    d = win_ref.shape[-1]
    z = jnp.sin(fr_ref[0:1, :] * (_dot(feat_ref[...].astype(BF16), w1_ref[...].astype(BF16)) + b1_ref[...]))
    z = jnp.sin(fr_ref[1:2, :] * (_dot(z.astype(BF16), w2_ref[...].astype(BF16)) + b2_ref[...]))
    filt = _dot(z.astype(BF16), w3_ref[...].astype(BF16))
    win = win_ref[...]
    ff = filt[:, :d] * win
    fb = filt[:, d:] * win
    sum_ref[...] = (ff + fb).astype(sum_ref.dtype)
    dif_ref[...] = (ff - fb).astype(dif_ref.dtype)


def hyena_filters(l, d, w1, b1, w2, b2, w3, freq):
    t = jnp.linspace(0.0, 1.0, l, dtype=F32)[:, None]
    pos = jnp.arange(l, dtype=F32)[:, None]
    bands = jnp.linspace(1e-4, HY_BANDS - 1, HY_BANDS, dtype=F32)
    ang = 2.0 * math.pi * pos * bands / l
    feats = jnp.concatenate([t, jnp.cos(ang), -jnp.sin(ang)], axis=-1)
    emb = feats.shape[1]
    feats = jnp.pad(feats, ((0, 0), (0, LANES - emb)))
    w1p = jnp.pad(w1, ((0, LANES - emb), (0, 0)))
    deltas = jnp.abs(jnp.linspace(math.log(HY_TARGET) / HY_SLOW_DECAY,
                                  math.log(HY_TARGET) / HY_FAST_DECAY, d, dtype=F32))
    window = jnp.exp(-t * deltas)
    hid = w2.shape[0]
    tl = min(l, 512)
    full = lambda shp: pl.BlockSpec(shp, lambda i: (0,) * len(shp))
    return pl.pallas_call(
        _hy_filter_kernel,
        grid=(l // tl,),
        in_specs=[
            pl.BlockSpec((tl, LANES), lambda i: (i, 0)),
            full((LANES, hid)), full((1, hid)), full((hid, hid)), full((1, hid)), full((hid, 2 * d)),
            full((2, hid)),
            pl.BlockSpec((tl, d), lambda i: (i, 0)),
        ],
        out_specs=[pl.BlockSpec((tl, d), lambda i: (i, 0))] * 2,
        out_shape=[jax.ShapeDtypeStruct((l, d), BF16)] * 2,
        compiler_params=_params("parallel"),
        name="hyena_filter",
    )(feats, w1p, b1.reshape(1, hid), w2, b2.reshape(1, hid), w3, freq, window)


def _dft_tables(l):
    n = 2 * l
    blk = 64
    assert l % blk == 0

    def trig(step, count):
        k = lax.broadcasted_iota(jnp.int32, (l, count), 0)
        t = lax.broadcasted_iota(jnp.int32, (l, count), 1)
        ang = (((2 * k + 1) * step * t) % (2 * n)).astype(F32) * (math.pi / n)
        return jnp.cos(ang), jnp.sin(ang)

    c1, s1 = trig(blk, l // blk)
    c0, s0 = trig(1, blk)
    cos_kt = (c1[:, :, None] * c0[:, None, :] - s1[:, :, None] * s0[:, None, :]).reshape(l, l).astype(BF16)
    sin_kt = (s1[:, :, None] * c0[:, None, :] + c1[:, :, None] * s0[:, None, :]).reshape(l, l).astype(BF16)
    c1t, s1t, c0t, s0t = c1.T, s1.T, c0.T, s0.T
    cos_tk = (c1t[:, None, :] * c0t[None, :, :] - s1t[:, None, :] * s0t[None, :, :]).reshape(l, l).astype(BF16)
    sin_tk = (s1t[:, None, :] * c0t[None, :, :] + c1t[:, None, :] * s0t[None, :, :]).reshape(l, l).astype(BF16)
    return cos_kt, sin_kt, cos_tk, sin_tk


def _hy_spec_kernel(c_ref, s_ref, fs_ref, fd_ref, bias_ref, gr_ref, gs_ref):
    gr_ref[...] = _dot(c_ref[...], fs_ref[...]) + bias_ref[...]
    gs_ref[...] = _dot(s_ref[...], fd_ref[...])


def hyena_filter_spectrum(cos_kt, sin_kt, f_sum, f_dif, f_bias):
    l, d = f_sum.shape
    tm = min(l, 512)
    tn = min(d, 512)
    return pl.pallas_call(
        _hy_spec_kernel,
        grid=(l // tm, d // tn),
        in_specs=[
            pl.BlockSpec((tm, l), lambda i, j: (i, 0)),
            pl.BlockSpec((tm, l), lambda i, j: (i, 0)),
            pl.BlockSpec((l, tn), lambda i, j: (0, j)),
            pl.BlockSpec((l, tn), lambda i, j: (0, j)),
            pl.BlockSpec((1, tn), lambda i, j: (0, j)),
        ],
        out_specs=[pl.BlockSpec((tm, tn), lambda i, j: (i, j))] * 2,
        out_shape=[jax.ShapeDtypeStruct((l, d), F32)] * 2,
        compiler_params=_params("parallel", "parallel"),
        name="hyena_filter_spectrum",
    )(cos_kt, sin_kt, f_sum, f_dif, f_bias.reshape(1, d))


def _hy_conv_kernel(p0_ref, p1_ref, pv_ref, w0_ref, w1_ref, wv_ref, b0_ref, b1_ref, bv_ref, z_ref, x0_ref):
    l = p0_ref.shape[1]
    t = lax.broadcasted_iota(jnp.int32, (l, 1), 0)

    def conv(p_ref, w_ref, b_ref):
        p = p_ref[0].astype(F32)
        prev = jnp.where(t == 0, 0.0, pltpu.roll(p, 1, axis=0))
        nxt = jnp.where(t == l - 1, 0.0, pltpu.roll(p, l - 1, axis=0))
        return b_ref[...] + prev * w_ref[0:1, :] + p * w_ref[1:2, :] + nxt * w_ref[2:3, :]

    x0_ref[0] = conv(p0_ref, w0_ref, b0_ref).astype(x0_ref.dtype)
    z_ref[0] = (conv(pv_ref, wv_ref, bv_ref) * conv(p1_ref, w1_ref, b1_ref)).astype(z_ref.dtype)


def hyena_short_conv(proj, conv_w, conv_b, b, l, d):
    tc = min(d, 512)
    nc = d // tc
    p_spec = lambda off: pl.BlockSpec((1, l, tc), lambda bi, j: (bi, 0, off * nc + j))
    w_spec = lambda off: pl.BlockSpec((HY_SHORT, tc), lambda bi, j: (0, off * nc + j))
    b_spec = lambda off: pl.BlockSpec((1, tc), lambda bi, j: (0, off * nc + j))
    cb = conv_b.reshape(1, 3 * d)
    return pl.pallas_call(
        _hy_conv_kernel,
        grid=(b, nc),
        in_specs=[p_spec(0), p_spec(1), p_spec(2), w_spec(0), w_spec(1), w_spec(2),
                  b_spec(0), b_spec(1), b_spec(2)],
        out_specs=[pl.BlockSpec((1, l, tc), lambda bi, j: (bi, 0, j))] * 2,
        out_shape=[jax.ShapeDtypeStruct((b, l, d), BF16), jax.ShapeDtypeStruct((b, l, d), BF16)],
        compiler_params=_params("parallel", "parallel"),
        name="hyena_short_conv",
    )(proj, proj, proj, conv_w, conv_w, conv_w, cb, cb, cb)


def _hy_fwd_kernel(c_ref, s_ref, z_ref, gr_ref, gs_ref, yr_ref, ys_ref):
    z = z_ref[0]
    zr = _dot(c_ref[...], z)
    zs = _dot(s_ref[...], z)
    gr = gr_ref[...]
    gs = gs_ref[...]
    yr_ref[0] = (zr * gr - zs * gs).astype(yr_ref.dtype)
    ys_ref[0] = (zr * gs + zs * gr).astype(ys_ref.dtype)


def hyena_forward_dft(cos_kt, sin_kt, z, g_r, g_s):
    b, l, d = z.shape
    tm = min(l, 512)
    tn = min(d, 512)
    return pl.pallas_call(
        _hy_fwd_kernel,
        grid=(l // tm, b, d // tn),
        in_specs=[
            pl.BlockSpec((tm, l), lambda i, bi, j: (i, 0)),
            pl.BlockSpec((tm, l), lambda i, bi, j: (i, 0)),
            pl.BlockSpec((1, l, tn), lambda i, bi, j: (bi, 0, j)),
            pl.BlockSpec((tm, tn), lambda i, bi, j: (i, j)),
            pl.BlockSpec((tm, tn), lambda i, bi, j: (i, j)),
        ],
        out_specs=[pl.BlockSpec((1, tm, tn), lambda i, bi, j: (bi, i, j))] * 2,
        out_shape=[jax.ShapeDtypeStruct((b, l, d), BF16)] * 2,
        compiler_params=_params("parallel", "parallel", "parallel"),
        name="hyena_forward_dft",
    )(cos_kt, sin_kt, z, g_r, g_s)


def _hy_inv_kernel(ct_ref, st_ref, yr_ref, ys_ref, x0_ref, o_ref, *, inv_scale):
    y = _dot(ct_ref[...], yr_ref[0]) + _dot(st_ref[...], ys_ref[0])
    o_ref[0] = (y * inv_scale * x0_ref[0]).astype(o_ref.dtype)


def hyena_inverse_dft(cos_tk, sin_tk, y_r, y_s, x0):
    b, l, d = y_r.shape
    tm = min(l, 512)
    tn = min(d, 512)
    return pl.pallas_call(
        functools.partial(_hy_inv_kernel, inv_scale=1.0 / l),
        grid=(l // tm, b, d // tn),
        in_specs=[
            pl.BlockSpec((tm, l), lambda i, bi, j: (i, 0)),
            pl.BlockSpec((tm, l), lambda i, bi, j: (i, 0)),
            pl.BlockSpec((1, l, tn), lambda i, bi, j: (bi, 0, j)),
            pl.BlockSpec((1, l, tn), lambda i, bi, j: (bi, 0, j)),
            pl.BlockSpec((1, tm, tn), lambda i, bi, j: (bi, i, j)),
        ],
        out_specs=pl.BlockSpec((1, tm, tn), lambda i, bi, j: (bi, i, j)),
        out_shape=jax.ShapeDtypeStruct((b, l, d), BF16),
        compiler_params=_params("parallel", "parallel", "parallel"),
        name="hyena_inverse_dft",
    )(cos_tk, sin_tk, y_r, y_s, x0)


def _tile_rows(grp):
    span = grp.l if grp.per_batch else grp.t
    return next(tm for tm in (1024, 512, 256, 128) if span % tm == 0)


def kernel(x_prompt, x_sample, cache_k, cache_v, state_ret_fwd, state_ret_bwd, state_ml_C_fwd, state_ml_n_fwd, state_ml_m_fwd, state_ml_C_bwd, state_ml_n_bwd, state_ml_m_bwd, c, c_ctx, mod_w, mod_b, norm_mix_pre, norm_mix_post, norm_ffn_pre, norm_ffn_post, mlp_w1, mlp_w2, ret_w_in, ret_decay_fwd, ret_decay_bwd, ret_gn, ret_w_out, att_w_in, att_q_gain, att_k_gain, att_w_out, ml_w_in, ml_gate_b, ml_gn, ml_w_out, hy_w_in, hy_b_in, hy_conv_w, hy_conv_b, hy_f_w1, hy_f_b1, hy_f_w2, hy_f_b2, hy_f_w3, hy_sin_freq, hy_f_bias, hy_w_out):
    bp, lp, d = x_prompt.shape
    bs, ls, _ = x_sample.shape
    depth = mod_w.shape[0]
    n_mixers = 4
    mod_rows = 16
    assert 1 + bs <= mod_rows

    grp_p = Group(bp, lp, 0, False)
    grp_s = Group(bs, ls, 1, True)
    groups = (grp_p, grp_s)

    cond = jnp.concatenate([c_ctx[None, :], c, jnp.zeros((mod_rows - 1 - bs, d), F32)], axis=0)
    mod_all = adaln_all(cond, mod_w, mod_b)

    xs = [x_prompt.reshape(grp_p.t, d), x_sample.reshape(grp_s.t, d)]
    new_k = new_v = new_rf = new_rb = None
    new_ml = None

    for i in range(depth):
        mixer = i % n_mixers
        j = i // n_mixers
        mod3 = mod_all[i].reshape(mod_rows, 1, 6 * d)
        ys = []
        for gi, grp in enumerate(groups):
            x = xs[gi]
            tm = _tile_rows(grp)
            is_prompt = gi == 0
            if mixer == 0:
                w_in = ret_w_in[j].astype(BF16)
                proj = norm_matmul(x, norm_mix_pre[i], mod3, 0, 1, w_in, None, BF16, grp, tm)
                dec = jnp.stack([ret_decay_fwd[j], ret_decay_bwd[j]]).astype(F32)
                s0f = None if is_prompt else state_ret_fwd
                s0b = None if is_prompt else state_ret_bwd
                assert is_prompt or state_ret_fwd.shape[1] == 1
                res = retention_core(proj.reshape(grp.b, grp.l, -1), dec, ret_gn[j], s0f, s0b,
                                     grp.b, grp.l, emit_state=is_prompt)
                if is_prompt:
                    new_rf, new_rb = res[1], res[2]
                a = res[0].reshape(grp.t, -1)
                w_out = ret_w_out[j].astype(BF16)
            elif mixer == 1:
                w_in = att_w_in[j].astype(BF16)
                proj = norm_matmul(x, norm_mix_pre[i], mod3, 0, 1, w_in, None, F32 if is_prompt else BF16, grp, tm)
                proj = proj.reshape(grp.b, grp.l, -1)
                x3 = x.reshape(grp.b, grp.l, d)
                w_out = att_w_out[j].astype(BF16)
                tail = (w_out, mod3, norm_mix_post[i], norm_ffn_pre[i], grp, min(grp.l, 256))
                if is_prompt:
                    res = attention_block(proj, x3, att_q_gain[j], att_k_gain[j], None, None, None, *tail,
                                          emit_kv=True)
                    new_k = res[2].reshape(grp.b, 1, grp.l, ATT_KV, ATT_HD)
                    new_v = res[3].reshape(grp.b, 1, grp.l, ATT_KV, ATT_HD)
                else:
                    assert cache_k.shape[1] == 1
                    ck = cache_k.reshape(grp.b, cache_k.shape[2], ATT_KV * ATT_HD)
                    cv = cache_v.reshape(grp.b, cache_v.shape[2], ATT_KV * ATT_HD)
                    res = attention_block(proj, x3, att_q_gain[j], att_k_gain[j], _rope_tables(grp.l), ck, cv,
                                          *tail, emit_kv=False)
                a = None
                x, h_mlp = res[0].reshape(grp.t, d), res[1].reshape(grp.t, d)
            elif mixer == 2:
                hh = ML_HEADS
                n_main = ml_w_in.shape[2] - 4 * hh
                w_main = ml_w_in[j][:, :n_main].astype(BF16)
                w_gate = jnp.pad(ml_w_in[j][:, n_main:], ((0, 0), (0, LANES - 4 * hh))).astype(BF16)
                b_gate = jnp.pad(ml_gate_b[j], (0, LANES - 4 * hh))
                proj, gates = norm_matmul(x, norm_mix_pre[i], mod3, 0, 1, w_main, None, BF16, grp, tm,
                                          aux=(w_gate, b_gate))
                gates = gates[:, :4 * hh].reshape(grp.b, grp.l, 4, hh)
                n_chunks = grp.l // CHUNK
                gates_t = gates.transpose(0, 3, 2, 1).reshape(grp.b, hh, 4, n_chunks, CHUNK)
                gates_t = jnp.pad(gates_t, ((0, 0), (0, 0), (0, 0), (0, -n_chunks % 16), (0, 0)))
                state = None
                if not is_prompt:
                    assert state_ml_C_fwd.shape[1] == 1
                    state = (state_ml_C_fwd, state_ml_n_fwd, state_ml_m_fwd,
                             state_ml_C_bwd, state_ml_n_bwd, state_ml_m_bwd)
                res = mlstm_core(proj.reshape(grp.b, grp.l, -1), gates_t, ml_gn[j], state,
                                 grp.b, grp.l, emit_state=is_prompt)
                if is_prompt:
                    dqk = res[2].shape[-1]
                    new_ml = (res[1], res[2].reshape(grp.b, 1, hh, dqk), res[3].reshape(grp.b, 1, hh),
                              res[4], res[5].reshape(grp.b, 1, hh, dqk), res[6].reshape(grp.b, 1, hh))
                a = res[0].reshape(grp.t, -1)
                w_out = ml_w_out[j].astype(BF16)
            else:
                w_in = hy_w_in[j].astype(BF16)
                proj = norm_matmul(x, norm_mix_pre[i], mod3, 0, 1, w_in, hy_b_in[j], BF16, grp, tm)
                z, x0 = hyena_short_conv(proj.reshape(grp.b, grp.l, 3 * d), hy_conv_w[j], hy_conv_b[j],
                                         grp.b, grp.l, d)
                f_sum, f_dif = hyena_filters(grp.l, d, hy_f_w1[j], hy_f_b1[j], hy_f_w2[j], hy_f_b2[j],
                                             hy_f_w3[j], hy_sin_freq[j])
                cos_kt, sin_kt, cos_tk, sin_tk = _dft_tables(grp.l)
                g_r, g_s = hyena_filter_spectrum(cos_kt, sin_kt, f_sum, f_dif, hy_f_bias[j])
                y_r, y_s = hyena_forward_dft(cos_kt, sin_kt, z, g_r, g_s)
                a = hyena_inverse_dft(cos_tk, sin_tk, y_r, y_s, x0).reshape(grp.t, d)
                w_out = hy_w_out[j].astype(BF16)
            if a is not None:
                x, h_mlp = matmul_resnorm(a, w_out, x, mod3, norm_mix_post[i], norm_ffn_pre[i], grp, tm)
            x = mlp_block(h_mlp, x, norm_ffn_post[i], mod3,
                          mlp_w1, mlp_w2, i, grp, tm, MLP_TILE_F)
            ys.append(x)
        xs = ys

    y_prompt = xs[0].reshape(bp, lp, d)
    y_sample = xs[1].reshape(bs, ls, d)
    return (y_prompt, y_sample, new_k, new_v, new_rf, new_rb) + tuple(new_ml)
```

```python
import functools
import math

import jax
import jax.numpy as jnp
import numpy as np
from jax import lax
from jax.experimental import pallas as pl
from jax.experimental.pallas import tpu as pltpu

F32 = jnp.float32
BF16 = jnp.bfloat16

EPS = 1e-6
CHUNK = 128
RET_CHUNK = 256
GRID_W = 64
ROPE_THETA = 10000.0

RET_HEADS = 4
ATT_HEADS = 8
ATT_KV = 2
ATT_HD = 128
ML_HEADS = 4

HY_BANDS = 16
HY_SHORT = 3
HY_FAST_DECAY = 0.3
HY_SLOW_DECAY = 1.5
HY_TARGET = 1e-2

VMEM_LIMIT_BYTES = 56 * 1024 * 1024
LANES = 128
PROJ_TILES_N = (3072, 1536)
MLP_TILE_F = 1024


def _params(*sem):
    return pltpu.CompilerParams(dimension_semantics=sem, vmem_limit_bytes=VMEM_LIMIT_BYTES)


def _dot(a, b):
    return jnp.dot(a, b, preferred_element_type=F32)


def _dot_nt(a, b):
    return lax.dot_general(a, b, (((1,), (1,)), ((), ())), preferred_element_type=F32)


def _dot_tn(a, b):
    return lax.dot_general(a, b, (((0,), (0,)), ((), ())), preferred_element_type=F32)


def _rms(x, g):
    return x * lax.rsqrt(jnp.mean(x * x, axis=-1, keepdims=True) + EPS) * g


def _log_sigmoid(x):
    return jnp.minimum(x, 0.0) - jnp.log1p(jnp.exp(-jnp.abs(x)))


def _chunk_offset(ch, c):
    return ch * c if isinstance(ch, int) else pl.multiple_of(ch * c, c)


def _pipelined_scan(n_chunks, weigh_both, advance_both, fin):
    if n_chunks == 1:
        weigh_both(0)
        advance_both(0)
        fin(0)
        return
    assert n_chunks == 2 or n_chunks % 4 == 0

    def advance_pair(i):
        advance_both(i)
        advance_both(i + 1)

    def fin_pair(i):
        chunks = (i, n_chunks - 1 - i, i + 1, n_chunks - 2 - i)
        for ch in (dict.fromkeys(chunks) if isinstance(i, int) else chunks):
            fin(ch)

    def body(p, carry):
        advance_pair(2 * p)
        weigh_both(2 * p + 2)
        weigh_both(2 * p + 3)
        return carry

    def body_fin(p, carry):
        advance_pair(2 * p)
        fin_pair(2 * p)
        weigh_both(2 * p + 2)
        weigh_both(2 * p + 3)
        return carry

    pairs = n_chunks // 2
    weigh_both(0)
    weigh_both(1)
    if pairs > 1:
        lax.fori_loop(0, pairs // 2, body, 0)
        lax.fori_loop(pairs // 2, pairs - 1, body_fin, 0)
    advance_pair(n_chunks - 2)
    fin_pair(n_chunks - 2)


class Group:
    def __init__(self, b, l, row0, per_batch):
        self.b, self.l, self.row0, self.per_batch = b, l, row0, per_batch
        self.t = b * l

    def mod_spec(self, chunk, tm, d):
        row0, per_batch, l = self.row0, self.per_batch, self.l
        if per_batch:
            return pl.BlockSpec((1, 1, d), lambda i, *_: (row0 + (i * tm) // l, 0, chunk))
        return pl.BlockSpec((1, 1, d), lambda i, *_: (row0, 0, chunk))


def _adaln_kernel(c_ref, w_ref, b_ref, o_ref):
    s = jax.nn.silu(c_ref[...])
    o_ref[0] = _dot(s.astype(BF16), w_ref[0].astype(BF16)) + b_ref[0]


def adaln_all(cond, mod_w, mod_b):
    depth, d, n = mod_w.shape
    rows = cond.shape[0]
    tn = 768
    return pl.pallas_call(
        _adaln_kernel,
        grid=(depth, n // tn),
        in_specs=[
            pl.BlockSpec((rows, d), lambda l, j: (0, 0)),
            pl.BlockSpec((1, d, tn), lambda l, j: (l, 0, j)),
            pl.BlockSpec((1, 1, tn), lambda l, j: (l, 0, j)),
        ],
        out_specs=pl.BlockSpec((1, rows, tn), lambda l, j: (l, 0, j)),
        out_shape=jax.ShapeDtypeStruct((depth, rows, n), F32),
        compiler_params=_params("parallel", "parallel"),
        name="adaln",
    )(cond, mod_w, mod_b.reshape(depth, 1, n))


def _norm_mm_kernel(x_ref, g_ref, sh_ref, sc_ref, w_ref, *rest, has_bias, has_aux):
    rest = list(rest)
    b_ref = rest.pop(0) if has_bias else None
    wa_ref, ba_ref = (rest.pop(0), rest.pop(0)) if has_aux else (None, None)
    o_ref = rest.pop(0)
    oa_ref = rest.pop(0) if has_aux else None
    (h_scr,) = rest

    @pl.when(pl.program_id(1) == 0)
    def _():
        y = _rms(x_ref[...], g_ref[...])
        h = (y * (1.0 + sc_ref[0]) + sh_ref[0]).astype(BF16)
        h_scr[...] = h
        if has_aux:
            oa_ref[...] = _dot(h, wa_ref[...]) + ba_ref[...]

    acc = _dot(h_scr[...], w_ref[...])
    if has_bias:
        acc = acc + b_ref[...]
    o_ref[...] = acc.astype(o_ref.dtype)


def norm_matmul(x, gain, mod3, sh_idx, sc_idx, w, bias, out_dtype, grp, tm, aux=None):
    t, d = x.shape
    n = w.shape[1]
    tn = next(c for c in PROJ_TILES_N if n % c == 0)
    in_specs = [
        pl.BlockSpec((tm, d), lambda i, j: (i, 0)),
        pl.BlockSpec((1, d), lambda i, j: (0, 0)),
        grp.mod_spec(sh_idx, tm, d),
        grp.mod_spec(sc_idx, tm, d),
        pl.BlockSpec((d, tn), lambda i, j: (0, j)),
    ]
    args = [x, gain.reshape(1, d), mod3, mod3, w]
    if bias is not None:
        in_specs.append(pl.BlockSpec((1, tn), lambda i, j: (0, j)))
        args.append(bias.reshape(1, n))
    out_specs = [pl.BlockSpec((tm, tn), lambda i, j: (i, j))]
    out_shape = [jax.ShapeDtypeStruct((t, n), out_dtype)]
    if aux is not None:
        w_aux, b_aux = aux
        na = w_aux.shape[1]
        in_specs += [pl.BlockSpec((d, na), lambda i, j: (0, 0)), pl.BlockSpec((1, na), lambda i, j: (0, 0))]
        args += [w_aux, b_aux.reshape(1, na)]
        out_specs.append(pl.BlockSpec((tm, na), lambda i, j: (i, 0)))
        out_shape.append(jax.ShapeDtypeStruct((t, na), F32))
    res = pl.pallas_call(
        functools.partial(_norm_mm_kernel, has_bias=bias is not None, has_aux=aux is not None),
        grid=(t // tm, n // tn),
        in_specs=in_specs,
        out_specs=out_specs,
        out_shape=out_shape,
        scratch_shapes=[pltpu.VMEM((tm, d), BF16)],
        compiler_params=_params("parallel", "arbitrary"),
        name="norm_matmul",
    )(*args)
    return res if aux is not None else res[0]


def _mm_res_kernel(a_ref, w_ref, x_ref, gate_ref, pg_ref, ng_ref, sh_ref, sc_ref, o_ref, h_ref):
    y = _dot(a_ref[...], w_ref[...])
    x1 = x_ref[...] + gate_ref[0] * _rms(y, pg_ref[...])
    o_ref[...] = x1
    h_ref[...] = (_rms(x1, ng_ref[...]) * (1.0 + sc_ref[0]) + sh_ref[0]).astype(h_ref.dtype)


def matmul_resnorm(a, w, x, mod3, post_gain, next_gain, grp, tm):
    t, k = a.shape
    d = w.shape[1]
    row = pl.BlockSpec((tm, d), lambda i: (i, 0))
    vec = pl.BlockSpec((1, d), lambda i: (0, 0))
    return pl.pallas_call(
        _mm_res_kernel,
        grid=(t // tm,),
        in_specs=[
            pl.BlockSpec((tm, k), lambda i: (i, 0)),
            pl.BlockSpec((k, d), lambda i: (0, 0)),
            row,
            grp.mod_spec(2, tm, d),
            vec,
            vec,
            grp.mod_spec(3, tm, d),
            grp.mod_spec(4, tm, d),
        ],
        out_specs=[row, row],
        out_shape=[jax.ShapeDtypeStruct((t, d), F32), jax.ShapeDtypeStruct((t, d), BF16)],
        compiler_params=_params("parallel"),
        name="out_proj",
    )(a, w, x, mod3, post_gain.reshape(1, d), next_gain.reshape(1, d), mod3, mod3)


def _mlp_kernel(h_ref, x_ref, w1_ref, w2_ref, gate_ref, pg_ref, o_ref, acc_scr):
    j = pl.program_id(1)

    @pl.when(j == 0)
    def _():
        acc_scr[...] = jnp.zeros_like(acc_scr)

    u = _dot(h_ref[...], w1_ref[...].astype(BF16))
    u = jnp.square(jnp.maximum(u, 0.0)).astype(BF16)
    acc_scr[...] += _dot(u, w2_ref[...].astype(BF16))

    @pl.when(j == pl.num_programs(1) - 1)
    def _():
        o_ref[...] = x_ref[...] + gate_ref[0] * _rms(acc_scr[...], pg_ref[...])


def mlp_block(h, x, post_gain, mod3, w1, w2, layer, grp, tm, tf):
    t, d = x.shape
    f = w1.shape[2]
    return pl.pallas_call(
        _mlp_kernel,
        grid=(t // tm, f // tf),
        in_specs=[
            pl.BlockSpec((tm, d), lambda i, j: (i, 0)),
            pl.BlockSpec((tm, d), lambda i, j: (i, 0)),
            pl.BlockSpec((None, d, tf), lambda i, j: (layer, 0, j)),
            pl.BlockSpec((None, tf, d), lambda i, j: (layer, j, 0)),
            grp.mod_spec(5, tm, d),
            pl.BlockSpec((1, d), lambda i, j: (0, 0)),
        ],
        out_specs=pl.BlockSpec((tm, d), lambda i, j: (i, 0)),
        out_shape=jax.ShapeDtypeStruct((t, d), F32),
        scratch_shapes=[pltpu.VMEM((tm, d), F32)],
        compiler_params=_params("parallel", "arbitrary"),
        name="mlp",
    )(h, x, w1, w2, mod3, post_gain.reshape(1, d))


def _ret_kernel(dec_ref, q_ref, k_ref, v_ref, g_ref, gn_ref, *rest, n_chunks, has_state, emit_state):
    rest = list(rest)
    s0f_ref = s0b_ref = sf_ref = sb_ref = None
    if has_state:
        s0f_ref, s0b_ref = rest[:2]
        rest = rest[2:]
    o_ref = rest.pop(0)
    if emit_state:
        sf_ref, sb_ref = rest[:2]
        rest = rest[2:]
    st_f, st_b, of_scr, ob_scr, sw_f, sw_b, kw_f, kw_b = rest

    c = RET_CHUNK
    dk = q_ref.shape[-1]
    h = pl.program_id(1)
    lg_f = _log_sigmoid(jnp.full((1, 1), dec_ref[0, h], F32))
    lg_b = _log_sigmoid(jnp.full((1, 1), dec_ref[1, h], F32))
    ri = lax.broadcasted_iota(jnp.int32, (c, c), 0)
    ci = lax.broadcasted_iota(jnp.int32, (c, c), 1)
    rel = (ri - ci).astype(F32)
    intra_f = jnp.where(rel >= 0, jnp.exp(lg_f * jnp.maximum(rel, 0.0)), 0.0)
    intra_b = jnp.where(rel <= 0, jnp.exp(lg_b * jnp.maximum(-rel, 0.0)), 0.0)
    idx = lax.broadcasted_iota(jnp.int32, (c, 1), 0).astype(F32)
    qdec_f = jnp.exp(lg_f * (idx + 1.0))
    kdec_f = jnp.exp(lg_f * (c - 1.0 - idx))
    qdec_b = jnp.exp(lg_b * (c - idx))
    kdec_b = jnp.exp(lg_b * idx)
    cdec_f = jnp.exp(lg_f * c)
    cdec_b = jnp.exp(lg_b * c)
    q_scale = dk ** -0.5

    if has_state:
        st_f[...] = s0f_ref[0, 0, 0]
        st_b[...] = s0b_ref[0, 0, 0]
    else:
        st_f[...] = jnp.zeros_like(st_f)
        st_b[...] = jnp.zeros_like(st_b)

    def weigh(ch, intra, kdec, sw, kw_s):
        rows = pl.ds(_chunk_offset(ch, c), c)
        kc = k_ref[0, rows, :]
        sw[ch] = (_dot_nt(q_ref[0, rows, :], kc) * (intra * q_scale)).astype(BF16)
        kw_s[ch] = (kc.astype(F32) * kdec).T.astype(BF16)

    def advance(ch, st, out_scr, qdec, cdec, sw, kw_s):
        rows = pl.ds(_chunk_offset(ch, c), c)
        qc = q_ref[0, rows, :]
        vc = v_ref[0, rows, :]
        s_old = st[...]
        out_scr[rows, :] = _dot(sw[ch], vc) + _dot(qc, s_old.astype(BF16)) * (qdec * q_scale)
        st[...] = s_old * cdec + _dot(kw_s[ch], vc)

    def fin(ch):
        rows = pl.ds(_chunk_offset(ch, c), c)
        o = _rms(of_scr[rows, :] + ob_scr[rows, :], gn_ref[...])
        o_ref[0, rows, :] = (jax.nn.silu(g_ref[0, rows, :].astype(F32)) * o).astype(o_ref.dtype)

    def weigh_both(i):
        weigh(i, intra_f, kdec_f, sw_f, kw_f)
        weigh(n_chunks - 1 - i, intra_b, kdec_b, sw_b, kw_b)

    def advance_both(i):
        advance(i, st_f, of_scr, qdec_f, cdec_f, sw_f, kw_f)
        advance(n_chunks - 1 - i, st_b, ob_scr, qdec_b, cdec_b, sw_b, kw_b)

    _pipelined_scan(n_chunks, weigh_both, advance_both, fin)

    if emit_state:
        sf_ref[0, 0, 0] = st_f[...]
        sb_ref[0, 0, 0] = st_b[...]


def retention_core(proj, dec, gn, s0f, s0b, b, l, emit_state):
    hh = RET_HEADS
    width = proj.shape[-1]
    dk = width // (6 * hh)
    dv = 2 * dk
    has_state = s0f is not None
    n_chunks = l // RET_CHUNK
    in_specs = [
        pl.BlockSpec(memory_space=pltpu.SMEM),
        pl.BlockSpec((1, l, dk), lambda bi, h: (bi, 0, h)),
        pl.BlockSpec((1, l, dk), lambda bi, h: (bi, 0, hh + h)),
        pl.BlockSpec((1, l, dv), lambda bi, h: (bi, 0, hh + h)),
        pl.BlockSpec((1, l, dv), lambda bi, h: (bi, 0, 2 * hh + h)),
        pl.BlockSpec((1, dv), lambda bi, h: (0, h)),
    ]
    args = [dec, proj, proj, proj, proj, gn.reshape(1, hh * dv)]
    if has_state:
        st_spec = pl.BlockSpec((1, 1, 1, dk, dv), lambda bi, h: (bi, 0, h, 0, 0))
        in_specs += [st_spec, st_spec]
        args += [s0f, s0b]
    out_specs = [pl.BlockSpec((1, l, dv), lambda bi, h: (bi, 0, h))]
    out_shape = [jax.ShapeDtypeStruct((b, l, hh * dv), BF16)]
    if emit_state:
        so_spec = pl.BlockSpec((1, 1, 1, dk, dv), lambda bi, h: (bi, 0, h, 0, 0))
        out_specs += [so_spec, so_spec]
        out_shape += [jax.ShapeDtypeStruct((b, 1, hh, dk, dv), F32)] * 2
    return pl.pallas_call(
        functools.partial(_ret_kernel, n_chunks=n_chunks, has_state=has_state, emit_state=emit_state),
        grid=(b, hh),
        in_specs=in_specs,
        out_specs=out_specs,
        out_shape=out_shape,
        scratch_shapes=[pltpu.VMEM((dk, dv), F32), pltpu.VMEM((dk, dv), F32),
                        pltpu.VMEM((l, dv), F32), pltpu.VMEM((l, dv), F32)]
                       + [pltpu.VMEM((n_chunks, RET_CHUNK, RET_CHUNK), BF16)] * 2
                       + [pltpu.VMEM((n_chunks, dk, RET_CHUNK), BF16)] * 2,
        compiler_params=_params("parallel", "parallel"),
        name="retention",
    )(*args)


def _rope_rot(x, cos_t, sin_t):
    lane = lax.broadcasted_iota(jnp.int32, x.shape, x.ndim - 1)
    nxt = pltpu.roll(x, LANES - 1, axis=x.ndim - 1)
    prv = pltpu.roll(x, 1, axis=x.ndim - 1)
    swapped = jnp.where(jnp.bitwise_and(lane, 1) == 0, nxt, prv)
    return x * cos_t + swapped * sin_t


def _att_kernel(q_ref, k_ref, v_ref, qg_ref, kg_ref, *rest, rope, has_cache, emit_kv, kv_heads, groups):
    rest = list(rest)
    cosq_ref = sinq_ref = cosk_ref = sink_ref = ck_ref = cv_ref = kn_ref = vo_ref = None
    if rope:
        cosq_ref, sinq_ref, cosk_ref, sink_ref = rest[:4]
        rest = rest[4:]
    if has_cache:
        ck_ref, cv_ref = rest[:2]
        rest = rest[2:]
    w_ref, x_ref, gate_ref, pg_ref, ng_ref, sh_ref, sc_ref, x1_ref, h_ref = rest[:9]
    rest = rest[9:]
    if emit_kv:
        kn_ref, vo_ref = rest[:2]
        rest = rest[2:]
    k_scr, v_scr, o_scr = rest
    hd = ATT_HD
    exp2_scale = hd ** -0.5 * math.log2(math.e)

    @pl.when(pl.program_id(1) == 0)
    def _():
        for kh in range(kv_heads):
            cols = slice(kh * hd, (kh + 1) * hd)
            kn = _rms(k_ref[0, :, cols].astype(F32), kg_ref[...])
            if emit_kv:
                kn_ref[0, :, cols] = kn
            if rope:
                kn = _rope_rot(kn, cosk_ref[...], sink_ref[...])
            k_scr[kh] = kn.astype(BF16)
            v_scr[kh, :, :hd] = v_ref[0, :, cols].astype(BF16)
            v_scr[kh, :, hd:] = jnp.ones((v_scr.shape[1], hd), BF16)
        if emit_kv:
            vo_ref[0] = v_ref[0]

    if has_cache:
        ck = [ck_ref[0, :, kh * hd:(kh + 1) * hd].astype(BF16) for kh in range(kv_heads)]
        cv = [jnp.concatenate([cv_ref[0, :, kh * hd:(kh + 1) * hd].astype(BF16),
                               jnp.ones((cv_ref.shape[1], hd), BF16)], axis=1) for kh in range(kv_heads)]

    def scores(head):
        kh = head // groups
        qn = _rms(q_ref[0, :, head * hd:(head + 1) * hd].astype(F32), qg_ref[...])
        if rope:
            qn = _rope_rot(qn, cosq_ref[...], sinq_ref[...])
        qb = qn.astype(BF16)
        return _dot_nt(qb, k_scr[kh]), (_dot_nt(qb, ck[kh]) if has_cache else None)

    heads = kv_heads * groups
    nxt = scores(0)
    for head in range(heads):
        kh = head // groups
        s1, s2 = nxt
        if head + 1 < heads:
            nxt = scores(head + 1)
        m = jnp.max(s1, axis=-1, keepdims=True)
        if has_cache:
            m = jnp.maximum(m, jnp.max(s2, axis=-1, keepdims=True))
        p1 = jnp.exp2((s1 - m) * exp2_scale)
        nd = _dot(p1.astype(BF16), v_scr[kh])
        if has_cache:
            p2 = jnp.exp2((s2 - m) * exp2_scale)
            nd = nd + _dot(p2.astype(BF16), cv[kh])
        o_scr[:, head * hd:(head + 1) * hd] = (nd[:, :hd] * (1.0 / nd[:, hd:])).astype(o_scr.dtype)

    y = _dot(o_scr[...], w_ref[...])
    x1 = x_ref[0] + gate_ref[0] * _rms(y, pg_ref[...])
    x1_ref[0] = x1
    h_ref[0] = (_rms(x1, ng_ref[...]) * (1.0 + sc_ref[0]) + sh_ref[0]).astype(h_ref.dtype)


def attention_block(proj, x, q_gain, k_gain, rope_tabs, cache_k, cache_v, w_out, mod3, post_gain, next_gain,
                    grp, tq, emit_kv):
    hd, kv, heads = ATT_HD, ATT_KV, ATT_HEADS
    b, l, d = x.shape
    rope = rope_tabs is not None
    has_cache = cache_k is not None
    kv_blk = heads // kv
    assert heads % kv == 0
    row0, per_batch = grp.row0, grp.per_batch

    def mod_spec(chunk):
        return pl.BlockSpec((1, 1, d), lambda bi, qi: (row0 + bi if per_batch else row0, 0, chunk))

    vec = lambda n: pl.BlockSpec((1, n), lambda bi, qi: (0, 0))
    in_specs = [
        pl.BlockSpec((1, tq, heads * hd), lambda bi, qi: (bi, qi, 0)),
        pl.BlockSpec((1, l, kv * hd), lambda bi, qi: (bi, 0, kv_blk)),
        pl.BlockSpec((1, l, kv * hd), lambda bi, qi: (bi, 0, kv_blk + 1)),
        vec(hd),
        vec(hd),
    ]
    args = [proj, proj, proj, q_gain.reshape(1, hd), k_gain.reshape(1, hd)]
    if rope:
        cos_t, sin_t = rope_tabs
        in_specs += [
            pl.BlockSpec((tq, hd), lambda bi, qi: (qi, 0)),
            pl.BlockSpec((tq, hd), lambda bi, qi: (qi, 0)),
            pl.BlockSpec((l, hd), lambda bi, qi: (0, 0)),
            pl.BlockSpec((l, hd), lambda bi, qi: (0, 0)),
        ]
        args += [cos_t, sin_t, cos_t, sin_t]
    if has_cache:
        past = cache_k.shape[1]
        c_spec = pl.BlockSpec((1, past, kv * hd), lambda bi, qi: (bi, 0, 0))
        in_specs += [c_spec, c_spec]
        args += [cache_k, cache_v]
    row = pl.BlockSpec((1, tq, d), lambda bi, qi: (bi, qi, 0))
    in_specs += [pl.BlockSpec((heads * hd, d), lambda bi, qi: (0, 0)), row,
                 mod_spec(2), vec(d), vec(d), mod_spec(3), mod_spec(4)]
    args += [w_out, x, mod3, post_gain.reshape(1, d), next_gain.reshape(1, d), mod3, mod3]
    out_specs = [row, row]
    out_shape = [jax.ShapeDtypeStruct((b, l, d), F32), jax.ShapeDtypeStruct((b, l, d), BF16)]
    if emit_kv:
        kv_spec = pl.BlockSpec((1, l, kv * hd), lambda bi, qi: (bi, 0, 0))
        out_specs += [kv_spec, kv_spec]
        out_shape += [jax.ShapeDtypeStruct((b, l, kv * hd), F32)] * 2
    return pl.pallas_call(
        functools.partial(_att_kernel, rope=rope, has_cache=has_cache, emit_kv=emit_kv,
                          kv_heads=kv, groups=heads // kv),
        grid=(b, l // tq),
        in_specs=in_specs,
        out_specs=out_specs,
        out_shape=out_shape,
        scratch_shapes=[pltpu.VMEM((kv, l, hd), BF16), pltpu.VMEM((kv, l, 2 * hd), BF16),
                        pltpu.VMEM((tq, heads * hd), BF16)],
        compiler_params=_params("parallel", "arbitrary"),
        name="attention",
    )(*args)


def _rope_tables(l):
    rows = l // GRID_W
    row = jnp.repeat(jnp.arange(rows, dtype=F32), GRID_W)
    col = jnp.tile(jnp.arange(GRID_W, dtype=F32), rows)
    half = ATT_HD // 2
    inv = ROPE_THETA ** (-jnp.arange(0, half, 2, dtype=F32) / half)
    ang = jnp.concatenate([row[:, None] * inv, col[:, None] * inv], axis=-1)
    cos_t = jnp.repeat(jnp.cos(ang), 2, axis=-1)
    sin_h = jnp.sin(ang)
    sin_t = jnp.stack([-sin_h, sin_h], axis=-1).reshape(l, ATT_HD)
    return cos_t, sin_t


def _ml_kernel(q_ref, k_ref, v_ref, og_ref, gt_ref, gn_ref, *rest, n_chunks, has_state, emit_state):
    rest = list(rest)
    c0f_ref = n0f_ref = m0f_ref = c0b_ref = n0b_ref = m0b_ref = None
    if has_state:
        c0f_ref, n0f_ref, m0f_ref, c0b_ref, n0b_ref, m0b_ref = rest[:6]
        rest = rest[6:]
    o_ref = rest.pop(0)
    outs = None
    if emit_state:
        outs = rest[:6]
        rest = rest[6:]
    cm_f, cm_b, nv_f, nv_b, hf_scr, hb_scr, fl_f, fl_b, br_f, br_b, mo_f, mo_b, mn_f, mn_b = rest[:14]
    sw_f, sw_b, kw_f, kw_b = rest[14:18]
    vt_f, vt_b = rest[18:23], rest[23:28]

    c = CHUNK
    dqk = q_ref.shape[-1]
    k_scale = dqk ** -0.5
    ri = lax.broadcasted_iota(jnp.int32, (c, c), 0)
    ci = lax.broadcasted_iota(jnp.int32, (c, c), 1)
    mask_f = ci <= ri
    mask_b = ci >= ri
    assert c == LANES and dqk == LANES
    dv_tiles = v_ref.shape[-1] // LANES

    def bf3(m):
        m = jnp.where(m, 1.0, 0.0).astype(BF16)
        return jnp.concatenate([m, m, m], axis=1), jnp.concatenate([m, m, m], axis=0)

    mf3, mf3_t = bf3(mask_f)
    mb3, mb3_t = bf3(mask_b)
    eye3, _ = bf3(ci == ri)

    def split3(x):
        hi = x.astype(BF16)
        r1 = x - hi.astype(F32)
        mid = r1.astype(BF16)
        lo = (r1 - mid.astype(F32)).astype(BF16)
        return jnp.concatenate([hi, mid, lo], axis=1)

    def wide(x):
        return jnp.concatenate([x] * dv_tiles, axis=1)

    if has_state:
        cm_f[...] = c0f_ref[0, 0, 0]
        cm_b[...] = c0b_ref[0, 0, 0]
        nv_f[...] = n0f_ref[0, 0]
        nv_b[...] = n0b_ref[0, 0]
        m0_f = jnp.broadcast_to(m0f_ref[0, 0], (1, LANES))
        m0_b = jnp.broadcast_to(m0b_ref[0, 0], (1, LANES))
    else:
        for r in (cm_f, cm_b, nv_f, nv_b):
            r[...] = jnp.zeros_like(r)
        m0_f = m0_b = jnp.zeros((1, LANES), F32)

    def gate_pass(row_i, row_f, m3_t, m0, fl, br, mo, mn, order):
        f_all = _log_sigmoid(gt_ref[0, 0, row_f])
        fl[...] = f_all
        b_all = _dot(split3(f_all), m3_t)
        br[...] = b_all
        b_end = jnp.sum(f_all, axis=-1, keepdims=True)
        w_max = jnp.max(b_end - b_all + gt_ref[0, 0, row_i], axis=-1, keepdims=True)
        m = m0
        for r in order:
            mo[r:r + 1, :] = m
            m = jnp.maximum(b_end[r:r + 1, :] + m, w_max[r:r + 1, :])
            mn[r:r + 1, :] = m
        return m

    m_last_f = gate_pass(0, 1, mb3_t, m0_f, fl_f, br_f, mo_f, mn_f, range(n_chunks))
    m_last_b = gate_pass(2, 3, mf3_t, m0_b, fl_b, br_b, mo_b, mn_b, reversed(range(n_chunks)))

    def weigh(ch, mask, m3, fl, br, mo, mn, row_i, sw, kw_s, wx_s, rs_s, em_s, cd_s, ks_s):
        rows = pl.ds(_chunk_offset(ch, c), c)
        one = pl.ds(ch, 1)
        qc = q_ref[0, rows, :]
        kc = k_ref[0, rows, :]
        i_row = gt_ref[0, 0, row_i, one, :]
        f_row = fl[one, :]
        b_row = br[one, :]
        m_old = mo[one, :]
        m_new = mn[one, :]
        b_q = _dot_nt(m3, jnp.broadcast_to(split3(f_row), (LANES, 3 * c)))
        i_q = _dot_nt(eye3, jnp.broadcast_to(split3(i_row), (LANES, 3 * c)))
        dlog = jnp.where(mask, b_q - b_row + i_row, -jnp.inf)
        inter = b_q + m_old
        m_q = jnp.maximum(inter, jnp.max(dlog, axis=-1, keepdims=True))
        s = _dot_nt(qc, kc) * (jnp.exp(dlog - m_q) * k_scale)
        sw[ch] = s.astype(BF16)
        rs_s[rows, :] = jnp.broadcast_to(jnp.sum(s, axis=-1, keepdims=True), (c, LANES))
        wx_s[rows, :] = jnp.exp(inter - m_q)
        em_s[rows, :] = jnp.exp(-m_q)
        b_end = jnp.sum(f_row, axis=-1, keepdims=True)
        wlog = b_end - b_q + i_q
        cd_s[one, :] = jnp.exp(b_end + m_old - m_new)
        kw = kc.astype(F32) * (jnp.exp(wlog - m_new) * k_scale)
        kw_s[ch] = kw.T.astype(BF16)
        ks_s[one, :] = jnp.sum(kw, axis=0, keepdims=True)

    def advance(ch, cm, nv, out_scr, sw, kw_s, wx_s, rs_s, em_s, cd_s, ks_s):
        rows = pl.ds(_chunk_offset(ch, c), c)
        one = pl.ds(ch, 1)
        qc = q_ref[0, rows, :]
        vc = v_ref[0, rows, :]
        c_old = cm[...]
        n_old = nv[...]
        w_x = wx_s[rows, :]
        num = _dot(sw[ch], vc) + _dot(qc, c_old.astype(BF16)) * wide(w_x)
        qn = _dot_nt(qc, jnp.broadcast_to(n_old, (LANES, dqk)).astype(BF16))
        den = jnp.maximum(jnp.abs(rs_s[rows, :] + qn * w_x), em_s[rows, :])
        out_scr[rows, :] = num * wide(1.0 / den)
        carry_dec = cd_s[one, :]
        cm[...] = c_old * wide(carry_dec) + _dot(kw_s[ch], vc)
        nv[...] = n_old * carry_dec + ks_s[one, :]

    def fin(ch):
        rows = pl.ds(_chunk_offset(ch, c), c)
        hn = _rms(hf_scr[rows, :] + hb_scr[rows, :], gn_ref[...])
        o_ref[0, rows, :] = (jax.nn.sigmoid(og_ref[0, rows, :].astype(F32)) * hn).astype(o_ref.dtype)

    tmp_f = (sw_f, kw_f) + tuple(vt_f)
    tmp_b = (sw_b, kw_b) + tuple(vt_b)

    def weigh_both(i):
        weigh(i, mask_f, mf3, fl_f, br_f, mo_f, mn_f, 0, *tmp_f)
        weigh(n_chunks - 1 - i, mask_b, mb3, fl_b, br_b, mo_b, mn_b, 2, *tmp_b)

    def advance_both(i):
        advance(i, cm_f, nv_f, hf_scr, *tmp_f)
        advance(n_chunks - 1 - i, cm_b, nv_b, hb_scr, *tmp_b)

    _pipelined_scan(n_chunks, weigh_both, advance_both, fin)

    if emit_state:
        for dst, src in zip(outs[0::3], (cm_f, cm_b)):
            dst[0, 0, 0] = src[...]
        for dst, src in zip(outs[1::3], (nv_f, nv_b)):
            dst[0, 0] = src[...]
        for dst, m_last in zip(outs[2::3], (m_last_f, m_last_b)):
            dst[0, 0] = m_last[:, 0:1]


def mlstm_core(proj, gates_t, gn, state, b, l, emit_state):
    hh = ML_HEADS
    width = proj.shape[-1]
    dqk = width // (6 * hh)
    dv = 2 * dqk
    has_state = state is not None
    n_chunks = l // CHUNK
    chunk_rows = gates_t.shape[3]
    in_specs = [
        pl.BlockSpec((1, l, dqk), lambda bi, h: (bi, 0, h)),
        pl.BlockSpec((1, l, dqk), lambda bi, h: (bi, 0, hh + h)),
        pl.BlockSpec((1, l, dv), lambda bi, h: (bi, 0, hh + h)),
        pl.BlockSpec((1, l, dv), lambda bi, h: (bi, 0, 2 * hh + h)),
        pl.BlockSpec((1, 1, 4, chunk_rows, CHUNK), lambda bi, h: (bi, h, 0, 0, 0)),
        pl.BlockSpec((1, dv), lambda bi, h: (0, h)),
    ]
    args = [proj, proj, proj, proj, gates_t, gn.reshape(1, hh * dv)]
    c_spec_in = pl.BlockSpec((1, 1, 1, dqk, dv), lambda bi, h: (bi, 0, h, 0, 0))
    n_spec = pl.BlockSpec((1, 1, 1, dqk), lambda bi, h: (bi, h, 0, 0))
    m_spec = pl.BlockSpec((1, 1, 1, 1), lambda bi, h: (bi, h, 0, 0))
    if has_state:
        cf, nf, mf, cb, nb, mb = state
        in_specs += [c_spec_in, n_spec, m_spec, c_spec_in, n_spec, m_spec]
        args += [cf, nf.reshape(b, hh, 1, dqk), mf.reshape(b, hh, 1, 1),
                 cb, nb.reshape(b, hh, 1, dqk), mb.reshape(b, hh, 1, 1)]
    out_specs = [pl.BlockSpec((1, l, dv), lambda bi, h: (bi, 0, h))]
    out_shape = [jax.ShapeDtypeStruct((b, l, hh * dv), BF16)]
    if emit_state:
        out_specs += [c_spec_in, n_spec, m_spec] * 2
        out_shape += [jax.ShapeDtypeStruct((b, 1, hh, dqk, dv), F32),
                      jax.ShapeDtypeStruct((b, hh, 1, dqk), F32),
                      jax.ShapeDtypeStruct((b, hh, 1, 1), F32)] * 2
    return pl.pallas_call(
        functools.partial(_ml_kernel, n_chunks=n_chunks, has_state=has_state, emit_state=emit_state),
        grid=(b, hh),
        in_specs=in_specs,
        out_specs=out_specs,
        out_shape=out_shape,
        scratch_shapes=[pltpu.VMEM((dqk, dv), F32), pltpu.VMEM((dqk, dv), F32),
                        pltpu.VMEM((1, dqk), F32), pltpu.VMEM((1, dqk), F32),
                        pltpu.VMEM((l, dv), F32), pltpu.VMEM((l, dv), F32)]
                       + [pltpu.VMEM((chunk_rows, CHUNK), F32)] * 4 + [pltpu.VMEM((chunk_rows, LANES), F32)] * 4
                       + [pltpu.VMEM((n_chunks, CHUNK, CHUNK), BF16)] * 2 + [pltpu.VMEM((n_chunks, dqk, CHUNK), BF16)] * 2
                       + [pltpu.VMEM((l, LANES), F32)] * 3 + [pltpu.VMEM((chunk_rows, LANES), F32)] * 2
                       + [pltpu.VMEM((l, LANES), F32)] * 3 + [pltpu.VMEM((chunk_rows, LANES), F32)] * 2,
        compiler_params=_params("parallel", "parallel"),
        name="mlstm",
    )(*args)


def _hy_filter_kernel(feat_ref, w1_ref, b1_ref, w2_ref, b2_ref, w3_ref, fr_ref, win_ref, sum_ref, dif_ref):
    d = win_ref.shape[-1]
    z = jnp.sin(fr_ref[0:1, :] * (_dot(feat_ref[...].astype(BF16), w1_ref[...].astype(BF16)) + b1_ref[...]))
    z = jnp.sin(fr_ref[1:2, :] * (_dot(z.astype(BF16), w2_ref[...].astype(BF16)) + b2_ref[...]))
    filt = _dot(z.astype(BF16), w3_ref[...].astype(BF16))
    win = win_ref[...]
    ff = filt[:, :d] * win
    fb = filt[:, d:] * win
    sum_ref[...] = (ff + fb).astype(sum_ref.dtype)
    dif_ref[...] = (ff - fb).astype(dif_ref.dtype)


def hyena_filters(l, d, w1, b1, w2, b2, w3, freq):
    t = jnp.linspace(0.0, 1.0, l, dtype=F32)[:, None]
    pos = jnp.arange(l, dtype=F32)[:, None]
    bands = jnp.linspace(1e-4, HY_BANDS - 1, HY_BANDS, dtype=F32)
    ang = 2.0 * math.pi * pos * bands / l
    feats = jnp.concatenate([t, jnp.cos(ang), -jnp.sin(ang)], axis=-1)
    emb = feats.shape[1]
    feats = jnp.pad(feats, ((0, 0), (0, LANES - emb)))
    w1p = jnp.pad(w1, ((0, LANES - emb), (0, 0)))
    deltas = jnp.abs(jnp.linspace(math.log(HY_TARGET) / HY_SLOW_DECAY,
                                  math.log(HY_TARGET) / HY_FAST_DECAY, d, dtype=F32))
    window = jnp.exp(-t * deltas)
    hid = w2.shape[0]
    tl = min(l, 512)
    full = lambda shp: pl.BlockSpec(shp, lambda i: (0,) * len(shp))
    return pl.pallas_call(
        _hy_filter_kernel,
        grid=(l // tl,),
        in_specs=[
            pl.BlockSpec((tl, LANES), lambda i: (i, 0)),
            full((LANES, hid)), full((1, hid)), full((hid, hid)), full((1, hid)), full((hid, 2 * d)),
            full((2, hid)),
            pl.BlockSpec((tl, d), lambda i: (i, 0)),
        ],
        out_specs=[pl.BlockSpec((tl, d), lambda i: (i, 0))] * 2,
        out_shape=[jax.ShapeDtypeStruct((l, d), BF16)] * 2,
        compiler_params=_params("parallel"),
        name="hyena_filter",
    )(feats, w1p, b1.reshape(1, hid), w2, b2.reshape(1, hid), w3, freq, window)


def _dft_tables(l):
    n = 2 * l
    blk = 64
    assert l % blk == 0

    def trig(step, count):
        k = lax.broadcasted_iota(jnp.int32, (l, count), 0)
        t = lax.broadcasted_iota(jnp.int32, (l, count), 1)
        ang = (((2 * k + 1) * step * t) % (2 * n)).astype(F32) * (math.pi / n)
        return jnp.cos(ang), jnp.sin(ang)

    c1, s1 = trig(blk, l // blk)
    c0, s0 = trig(1, blk)
    cos_kt = (c1[:, :, None] * c0[:, None, :] - s1[:, :, None] * s0[:, None, :]).reshape(l, l).astype(BF16)
    sin_kt = (s1[:, :, None] * c0[:, None, :] + c1[:, :, None] * s0[:, None, :]).reshape(l, l).astype(BF16)
    c1t, s1t, c0t, s0t = c1.T, s1.T, c0.T, s0.T
    cos_tk = (c1t[:, None, :] * c0t[None, :, :] - s1t[:, None, :] * s0t[None, :, :]).reshape(l, l).astype(BF16)
    sin_tk = (s1t[:, None, :] * c0t[None, :, :] + c1t[:, None, :] * s0t[None, :, :]).reshape(l, l).astype(BF16)
    return cos_kt, sin_kt, cos_tk, sin_tk


def _hy_spec_kernel(c_ref, s_ref, fs_ref, fd_ref, bias_ref, gr_ref, gs_ref):
    gr_ref[...] = _dot(c_ref[...], fs_ref[...]) + bias_ref[...]
    gs_ref[...] = _dot(s_ref[...], fd_ref[...])


def hyena_filter_spectrum(cos_kt, sin_kt, f_sum, f_dif, f_bias):
    l, d = f_sum.shape
    tm = min(l, 512)
    tn = min(d, 512)
    return pl.pallas_call(
        _hy_spec_kernel,
        grid=(l // tm, d // tn),
        in_specs=[
            pl.BlockSpec((tm, l), lambda i, j: (i, 0)),
            pl.BlockSpec((tm, l), lambda i, j: (i, 0)),
            pl.BlockSpec((l, tn), lambda i, j: (0, j)),
            pl.BlockSpec((l, tn), lambda i, j: (0, j)),
            pl.BlockSpec((1, tn), lambda i, j: (0, j)),
        ],
        out_specs=[pl.BlockSpec((tm, tn), lambda i, j: (i, j))] * 2,
        out_shape=[jax.ShapeDtypeStruct((l, d), F32)] * 2,
        compiler_params=_params("parallel", "parallel"),
        name="hyena_filter_spectrum",
    )(cos_kt, sin_kt, f_sum, f_dif, f_bias.reshape(1, d))


def _hy_conv_kernel(p0_ref, p1_ref, pv_ref, w0_ref, w1_ref, wv_ref, b0_ref, b1_ref, bv_ref, z_ref, x0_ref):
    l = p0_ref.shape[1]
    t = lax.broadcasted_iota(jnp.int32, (l, 1), 0)

    def conv(p_ref, w_ref, b_ref):
        p = p_ref[0].astype(F32)
        prev = jnp.where(t == 0, 0.0, pltpu.roll(p, 1, axis=0))
        nxt = jnp.where(t == l - 1, 0.0, pltpu.roll(p, l - 1, axis=0))
        return b_ref[...] + prev * w_ref[0:1, :] + p * w_ref[1:2, :] + nxt * w_ref[2:3, :]

    x0_ref[0] = conv(p0_ref, w0_ref, b0_ref).astype(x0_ref.dtype)
    z_ref[0] = (conv(pv_ref, wv_ref, bv_ref) * conv(p1_ref, w1_ref, b1_ref)).astype(z_ref.dtype)


def hyena_short_conv(proj, conv_w, conv_b, b, l, d):
    tc = min(d, 512)
    nc = d // tc
    p_spec = lambda off: pl.BlockSpec((1, l, tc), lambda bi, j: (bi, 0, off * nc + j))
    w_spec = lambda off: pl.BlockSpec((HY_SHORT, tc), lambda bi, j: (0, off * nc + j))
    b_spec = lambda off: pl.BlockSpec((1, tc), lambda bi, j: (0, off * nc + j))
    cb = conv_b.reshape(1, 3 * d)
    return pl.pallas_call(
        _hy_conv_kernel,
        grid=(b, nc),
        in_specs=[p_spec(0), p_spec(1), p_spec(2), w_spec(0), w_spec(1), w_spec(2),
                  b_spec(0), b_spec(1), b_spec(2)],
        out_specs=[pl.BlockSpec((1, l, tc), lambda bi, j: (bi, 0, j))] * 2,
        out_shape=[jax.ShapeDtypeStruct((b, l, d), BF16), jax.ShapeDtypeStruct((b, l, d), BF16)],
        compiler_params=_params("parallel", "parallel"),
        name="hyena_short_conv",
    )(proj, proj, proj, conv_w, conv_w, conv_w, cb, cb, cb)


def _hy_fwd_kernel(c_ref, s_ref, z_ref, gr_ref, gs_ref, yr_ref, ys_ref):
    z = z_ref[0]
    zr = _dot(c_ref[...], z)
    zs = _dot(s_ref[...], z)
    gr = gr_ref[...]
    gs = gs_ref[...]
    yr_ref[0] = (zr * gr - zs * gs).astype(yr_ref.dtype)
    ys_ref[0] = (zr * gs + zs * gr).astype(ys_ref.dtype)


def hyena_forward_dft(cos_kt, sin_kt, z, g_r, g_s):
    b, l, d = z.shape
    tm = min(l, 512)
    tn = min(d, 1024)
    return pl.pallas_call(
        _hy_fwd_kernel,
        grid=(l // tm, b, d // tn),
        in_specs=[
            pl.BlockSpec((tm, l), lambda i, bi, j: (i, 0)),
            pl.BlockSpec((tm, l), lambda i, bi, j: (i, 0)),
            pl.BlockSpec((1, l, tn), lambda i, bi, j: (bi, 0, j)),
            pl.BlockSpec((tm, tn), lambda i, bi, j: (i, j)),
            pl.BlockSpec((tm, tn), lambda i, bi, j: (i, j)),
        ],
        out_specs=[pl.BlockSpec((1, tm, tn), lambda i, bi, j: (bi, i, j))] * 2,
        out_shape=[jax.ShapeDtypeStruct((b, l, d), BF16)] * 2,
        compiler_params=_params("parallel", "parallel", "parallel"),
        name="hyena_forward_dft",
    )(cos_kt, sin_kt, z, g_r, g_s)


def _hy_inv_kernel(ct_ref, st_ref, yr_ref, ys_ref, x0_ref, w_ref, x_ref, gate_ref, pg_ref, ng_ref, sh_ref, sc_ref,
                   x1_ref, h_ref, *, inv_scale):
    z = _dot(ct_ref[...], yr_ref[0]) + _dot(st_ref[...], ys_ref[0])
    a = (z * inv_scale * x0_ref[0]).astype(BF16)
    y = _dot(a, w_ref[...])
    x1 = x_ref[0] + gate_ref[0] * _rms(y, pg_ref[...])
    x1_ref[0] = x1
    h_ref[0] = (_rms(x1, ng_ref[...]) * (1.0 + sc_ref[0]) + sh_ref[0]).astype(h_ref.dtype)


def hyena_inverse_block(cos_tk, sin_tk, y_r, y_s, x0, x, w_out, mod3, post_gain, next_gain, grp):
    b, l, d = y_r.shape
    tm = min(l, 512)
    row0, per_batch = grp.row0, grp.per_batch

    def mod_spec(chunk):
        return pl.BlockSpec((1, 1, d), lambda i, bi: (row0 + bi if per_batch else row0, 0, chunk))

    tab = pl.BlockSpec((tm, l), lambda i, bi: (i, 0))
    spec = pl.BlockSpec((1, l, d), lambda i, bi: (bi, 0, 0))
    row = pl.BlockSpec((1, tm, d), lambda i, bi: (bi, i, 0))
    vec = pl.BlockSpec((1, d), lambda i, bi: (0, 0))
    return pl.pallas_call(
        functools.partial(_hy_inv_kernel, inv_scale=1.0 / l),
        grid=(l // tm, b),
        in_specs=[tab, tab, spec, spec, row, pl.BlockSpec((d, d), lambda i, bi: (0, 0)), row,
                  mod_spec(2), vec, vec, mod_spec(3), mod_spec(4)],
        out_specs=[row, row],
        out_shape=[jax.ShapeDtypeStruct((b, l, d), F32), jax.ShapeDtypeStruct((b, l, d), BF16)],
        compiler_params=_params("parallel", "parallel"),
        name="hyena_inverse_dft",
    )(cos_tk, sin_tk, y_r, y_s, x0, w_out, x, mod3, post_gain.reshape(1, d), next_gain.reshape(1, d), mod3, mod3)


def _tile_rows(grp):
    span = grp.l if grp.per_batch else grp.t
    return next(tm for tm in (1024, 512, 256, 128) if span % tm == 0)


def kernel(x_prompt, x_sample, cache_k, cache_v, state_ret_fwd, state_ret_bwd, state_ml_C_fwd, state_ml_n_fwd, state_ml_m_fwd, state_ml_C_bwd, state_ml_n_bwd, state_ml_m_bwd, c, c_ctx, mod_w, mod_b, norm_mix_pre, norm_mix_post, norm_ffn_pre, norm_ffn_post, mlp_w1, mlp_w2, ret_w_in, ret_decay_fwd, ret_decay_bwd, ret_gn, ret_w_out, att_w_in, att_q_gain, att_k_gain, att_w_out, ml_w_in, ml_gate_b, ml_gn, ml_w_out, hy_w_in, hy_b_in, hy_conv_w, hy_conv_b, hy_f_w1, hy_f_b1, hy_f_w2, hy_f_b2, hy_f_w3, hy_sin_freq, hy_f_bias, hy_w_out):
    bp, lp, d = x_prompt.shape
    bs, ls, _ = x_sample.shape
    depth = mod_w.shape[0]
    n_mixers = 4
    mod_rows = 16
    assert 1 + bs <= mod_rows

    grp_p = Group(bp, lp, 0, False)
    grp_s = Group(bs, ls, 1, True)
    groups = (grp_p, grp_s)

    cond = jnp.concatenate([c_ctx[None, :], c, jnp.zeros((mod_rows - 1 - bs, d), F32)], axis=0)
    mod_all = adaln_all(cond, mod_w, mod_b)

    xs = [x_prompt.reshape(grp_p.t, d), x_sample.reshape(grp_s.t, d)]
    new_k = new_v = new_rf = new_rb = None
    new_ml = None

    for i in range(depth):
        mixer = i % n_mixers
        j = i // n_mixers
        mod3 = mod_all[i].reshape(mod_rows, 1, 6 * d)
        ys = []
        for gi, grp in enumerate(groups):
            x = xs[gi]
            tm = _tile_rows(grp)
            is_prompt = gi == 0
            if mixer == 0:
                w_in = ret_w_in[j].astype(BF16)
                proj = norm_matmul(x, norm_mix_pre[i], mod3, 0, 1, w_in, None, BF16, grp, tm)
                dec = jnp.stack([ret_decay_fwd[j], ret_decay_bwd[j]]).astype(F32)
                s0f = None if is_prompt else state_ret_fwd
                s0b = None if is_prompt else state_ret_bwd
                assert is_prompt or state_ret_fwd.shape[1] == 1
                res = retention_core(proj.reshape(grp.b, grp.l, -1), dec, ret_gn[j], s0f, s0b,
                                     grp.b, grp.l, emit_state=is_prompt)
                if is_prompt:
                    new_rf, new_rb = res[1], res[2]
                a = res[0].reshape(grp.t, -1)
                w_out = ret_w_out[j].astype(BF16)
            elif mixer == 1:
                w_in = att_w_in[j].astype(BF16)
                proj = norm_matmul(x, norm_mix_pre[i], mod3, 0, 1, w_in, None, F32 if is_prompt else BF16, grp, tm)
                proj = proj.reshape(grp.b, grp.l, -1)
                x3 = x.reshape(grp.b, grp.l, d)
                w_out = att_w_out[j].astype(BF16)
                tail = (w_out, mod3, norm_mix_post[i], norm_ffn_pre[i], grp, min(grp.l, 256))
                if is_prompt:
                    res = attention_block(proj, x3, att_q_gain[j], att_k_gain[j], None, None, None, *tail,
                                          emit_kv=True)
                    new_k = res[2].reshape(grp.b, 1, grp.l, ATT_KV, ATT_HD)
                    new_v = res[3].reshape(grp.b, 1, grp.l, ATT_KV, ATT_HD)
                else:
                    assert cache_k.shape[1] == 1
                    ck = cache_k.reshape(grp.b, cache_k.shape[2], ATT_KV * ATT_HD)
                    cv = cache_v.reshape(grp.b, cache_v.shape[2], ATT_KV * ATT_HD)
                    res = attention_block(proj, x3, att_q_gain[j], att_k_gain[j], _rope_tables(grp.l), ck, cv,
                                          *tail, emit_kv=False)
                a = None
                x, h_mlp = res[0].reshape(grp.t, d), res[1].reshape(grp.t, d)
            elif mixer == 2:
                hh = ML_HEADS
                n_main = ml_w_in.shape[2] - 4 * hh
                w_main = ml_w_in[j][:, :n_main].astype(BF16)
                w_gate = jnp.pad(ml_w_in[j][:, n_main:], ((0, 0), (0, LANES - 4 * hh))).astype(BF16)
                b_gate = jnp.pad(ml_gate_b[j], (0, LANES - 4 * hh))
                proj, gates = norm_matmul(x, norm_mix_pre[i], mod3, 0, 1, w_main, None, BF16, grp, tm,
                                          aux=(w_gate, b_gate))
                gates = gates[:, :4 * hh].reshape(grp.b, grp.l, 4, hh)
                n_chunks = grp.l // CHUNK
                gates_t = gates.transpose(0, 3, 2, 1).reshape(grp.b, hh, 4, n_chunks, CHUNK)
                gates_t = jnp.pad(gates_t, ((0, 0), (0, 0), (0, 0), (0, -n_chunks % 16), (0, 0)))
                state = None
                if not is_prompt:
                    assert state_ml_C_fwd.shape[1] == 1
                    state = (state_ml_C_fwd, state_ml_n_fwd, state_ml_m_fwd,
                             state_ml_C_bwd, state_ml_n_bwd, state_ml_m_bwd)
                res = mlstm_core(proj.reshape(grp.b, grp.l, -1), gates_t, ml_gn[j], state,
                                 grp.b, grp.l, emit_state=is_prompt)
                if is_prompt:
                    dqk = res[2].shape[-1]
                    new_ml = (res[1], res[2].reshape(grp.b, 1, hh, dqk), res[3].reshape(grp.b, 1, hh),
                              res[4], res[5].reshape(grp.b, 1, hh, dqk), res[6].reshape(grp.b, 1, hh))
                a = res[0].reshape(grp.t, -1)
                w_out = ml_w_out[j].astype(BF16)
            else:
                w_in = hy_w_in[j].astype(BF16)
                proj = norm_matmul(x, norm_mix_pre[i], mod3, 0, 1, w_in, hy_b_in[j], BF16, grp, tm)
                z, x0 = hyena_short_conv(proj.reshape(grp.b, grp.l, 3 * d), hy_conv_w[j], hy_conv_b[j],
                                         grp.b, grp.l, d)
                f_sum, f_dif = hyena_filters(grp.l, d, hy_f_w1[j], hy_f_b1[j], hy_f_w2[j], hy_f_b2[j],
                                             hy_f_w3[j], hy_sin_freq[j])
                cos_kt, sin_kt, cos_tk, sin_tk = _dft_tables(grp.l)
                g_r, g_s = hyena_filter_spectrum(cos_kt, sin_kt, f_sum, f_dif, hy_f_bias[j])
                y_r, y_s = hyena_forward_dft(cos_kt, sin_kt, z, g_r, g_s)
                res = hyena_inverse_block(cos_tk, sin_tk, y_r, y_s, x0, x.reshape(grp.b, grp.l, d),
                                          hy_w_out[j].astype(BF16), mod3, norm_mix_post[i], norm_ffn_pre[i], grp)
                a = None
                x, h_mlp = res[0].reshape(grp.t, d), res[1].reshape(grp.t, d)
            if a is not None:
                x, h_mlp = matmul_resnorm(a, w_out, x, mod3, norm_mix_post[i], norm_ffn_pre[i], grp, tm)
            x = mlp_block(h_mlp, x, norm_ffn_post[i], mod3,
                          mlp_w1, mlp_w2, i, grp, tm, MLP_TILE_F)
            ys.append(x)
        xs = ys

    y_prompt = xs[0].reshape(bp, lp, d)
    y_sample = xs[1].reshape(bs, ls, d)
    return (y_prompt, y_sample, new_k, new_v, new_rf, new_rb) + tuple(new_ml)
```

```python
import functools
import math

import jax
import jax.numpy as jnp
import numpy as np
from jax import lax
from jax.experimental import pallas as pl
from jax.experimental.pallas import tpu as pltpu

F32 = jnp.float32
BF16 = jnp.bfloat16

EPS = 1e-6
CHUNK = 128
RET_CHUNK = 256
GRID_W = 64
ROPE_THETA = 10000.0

RET_HEADS = 4
ATT_HEADS = 8
ATT_KV = 2
ATT_HD = 128
ML_HEADS = 4

HY_BANDS = 16
HY_SHORT = 3
HY_FAST_DECAY = 0.3
HY_SLOW_DECAY = 1.5
HY_TARGET = 1e-2

VMEM_LIMIT_BYTES = 56 * 1024 * 1024
LANES = 128
PROJ_TILES_N = (3072, 1536)
MLP_TILE_F = 1024


def _params(*sem):
    return pltpu.CompilerParams(dimension_semantics=sem, vmem_limit_bytes=VMEM_LIMIT_BYTES)


def _dot(a, b):
    return jnp.dot(a, b, preferred_element_type=F32)


def _dot_nt(a, b):
    return lax.dot_general(a, b, (((1,), (1,)), ((), ())), preferred_element_type=F32)


def _dot_tn(a, b):
    return lax.dot_general(a, b, (((0,), (0,)), ((), ())), preferred_element_type=F32)


def _rms(x, g):
    return x * lax.rsqrt(jnp.mean(x * x, axis=-1, keepdims=True) + EPS) * g


def _log_sigmoid(x):
    return jnp.minimum(x, 0.0) - jnp.log1p(jnp.exp(-jnp.abs(x)))


def _chunk_offset(ch, c):
    return ch * c if isinstance(ch, int) else pl.multiple_of(ch * c, c)


def _pipelined_scan(n_chunks, weigh_both, advance_both, fin):
    if n_chunks == 1:
        weigh_both(0)
        advance_both(0)
        fin(0)
        return
    assert n_chunks == 2 or n_chunks % 4 == 0

    def advance_pair(i):
        advance_both(i)
        advance_both(i + 1)

    def fin_pair(i):
        chunks = (i, n_chunks - 1 - i, i + 1, n_chunks - 2 - i)
        for ch in (dict.fromkeys(chunks) if isinstance(i, int) else chunks):
            fin(ch)

    def body(p, carry):
        advance_pair(2 * p)
        weigh_both(2 * p + 2)
        weigh_both(2 * p + 3)
        return carry

    def body_fin(p, carry):
        advance_pair(2 * p)
        fin_pair(2 * p)
        weigh_both(2 * p + 2)
        weigh_both(2 * p + 3)
        return carry

    pairs = n_chunks // 2
    weigh_both(0)
    weigh_both(1)
    if pairs > 1:
        lax.fori_loop(0, pairs // 2, body, 0, unroll=2 if (pairs // 2) % 2 == 0 else 1)
        lax.fori_loop(pairs // 2, pairs - 1, body_fin, 0)
    advance_pair(n_chunks - 2)
    fin_pair(n_chunks - 2)


class Group:
    def __init__(self, b, l, row0, per_batch):
        self.b, self.l, self.row0, self.per_batch = b, l, row0, per_batch
        self.t = b * l

    def mod_spec(self, chunk, tm, d):
        row0, per_batch, l = self.row0, self.per_batch, self.l
        if per_batch:
            return pl.BlockSpec((1, 1, d), lambda i, *_: (row0 + (i * tm) // l, 0, chunk))
        return pl.BlockSpec((1, 1, d), lambda i, *_: (row0, 0, chunk))


def _adaln_kernel(c_ref, w_ref, b_ref, o_ref):
    s = jax.nn.silu(c_ref[...])
    o_ref[0] = _dot(s.astype(BF16), w_ref[0].astype(BF16)) + b_ref[0]


def adaln_all(cond, mod_w, mod_b):
    depth, d, n = mod_w.shape
    rows = cond.shape[0]
    tn = 768
    return pl.pallas_call(
        _adaln_kernel,
        grid=(depth, n // tn),
        in_specs=[
            pl.BlockSpec((rows, d), lambda l, j: (0, 0)),
            pl.BlockSpec((1, d, tn), lambda l, j: (l, 0, j)),
            pl.BlockSpec((1, 1, tn), lambda l, j: (l, 0, j)),
        ],
        out_specs=pl.BlockSpec((1, rows, tn), lambda l, j: (l, 0, j)),
        out_shape=jax.ShapeDtypeStruct((depth, rows, n), F32),
        compiler_params=_params("parallel", "parallel"),
        name="adaln",
    )(cond, mod_w, mod_b.reshape(depth, 1, n))


def _norm_mm_kernel(x_ref, g_ref, sh_ref, sc_ref, w_ref, *rest, has_bias, has_aux):
    rest = list(rest)
    b_ref = rest.pop(0) if has_bias else None
    wa_ref, ba_ref = (rest.pop(0), rest.pop(0)) if has_aux else (None, None)
    o_ref = rest.pop(0)
    oa_ref = rest.pop(0) if has_aux else None
    (h_scr,) = rest

    @pl.when(pl.program_id(1) == 0)
    def _():
        y = _rms(x_ref[...], g_ref[...])
        h = (y * (1.0 + sc_ref[0]) + sh_ref[0]).astype(BF16)
        h_scr[...] = h
        if has_aux:
            oa_ref[...] = _dot(h, wa_ref[...]) + ba_ref[...]

    acc = _dot(h_scr[...], w_ref[...])
    if has_bias:
        acc = acc + b_ref[...]
    o_ref[...] = acc.astype(o_ref.dtype)


def norm_matmul(x, gain, mod3, sh_idx, sc_idx, w, bias, out_dtype, grp, tm, aux=None):
    t, d = x.shape
    n = w.shape[1]
    tn = next(c for c in PROJ_TILES_N if n % c == 0)
    in_specs = [
        pl.BlockSpec((tm, d), lambda i, j: (i, 0)),
        pl.BlockSpec((1, d), lambda i, j: (0, 0)),
        grp.mod_spec(sh_idx, tm, d),
        grp.mod_spec(sc_idx, tm, d),
        pl.BlockSpec((d, tn), lambda i, j: (0, j)),
    ]
    args = [x, gain.reshape(1, d), mod3, mod3, w]
    if bias is not None:
        in_specs.append(pl.BlockSpec((1, tn), lambda i, j: (0, j)))
        args.append(bias.reshape(1, n))
    out_specs = [pl.BlockSpec((tm, tn), lambda i, j: (i, j))]
    out_shape = [jax.ShapeDtypeStruct((t, n), out_dtype)]
    if aux is not None:
        w_aux, b_aux = aux
        na = w_aux.shape[1]
        in_specs += [pl.BlockSpec((d, na), lambda i, j: (0, 0)), pl.BlockSpec((1, na), lambda i, j: (0, 0))]
        args += [w_aux, b_aux.reshape(1, na)]
        out_specs.append(pl.BlockSpec((tm, na), lambda i, j: (i, 0)))
        out_shape.append(jax.ShapeDtypeStruct((t, na), F32))
    res = pl.pallas_call(
        functools.partial(_norm_mm_kernel, has_bias=bias is not None, has_aux=aux is not None),
        grid=(t // tm, n // tn),
        in_specs=in_specs,
        out_specs=out_specs,
        out_shape=out_shape,
        scratch_shapes=[pltpu.VMEM((tm, d), BF16)],
        compiler_params=_params("parallel", "arbitrary"),
        name="norm_matmul",
    )(*args)
    return res if aux is not None else res[0]


RING_SLOTS = 3


def _mm_res_kernel(a_hbm, w_ref, x_hbm, gate_ref, pg_ref, ng_ref, sh_ref, sc_ref, o_ref, h_ref,
                   a_buf, x_buf, sems):
    tm = a_buf.shape[1]
    step = pl.program_id(0)
    n_steps = pl.num_programs(0)

    def copies(s):
        slot = s % RING_SLOTS
        rows = pl.ds(pl.multiple_of(s * tm, tm), tm)
        return (pltpu.make_async_copy(a_hbm.at[rows, :], a_buf.at[slot], sems.at[0, slot]),
                pltpu.make_async_copy(x_hbm.at[rows, :], x_buf.at[slot], sems.at[1, slot]))

    def start(s):
        for cp in copies(s):
            cp.start()

    @pl.when(step == 0)
    def _():
        for s in range(RING_SLOTS - 1):
            @pl.when(s < n_steps)
            def _():
                start(s)

    @pl.when(step + RING_SLOTS - 1 < n_steps)
    def _():
        start(step + RING_SLOTS - 1)

    for cp in copies(step):
        cp.wait()
    slot = step % RING_SLOTS
    sub = min(tm, 256)
    for r in range(tm // sub):
        rows = slice(r * sub, (r + 1) * sub)
        y = _dot(a_buf[slot, rows, :], w_ref[...])
        x1 = x_buf[slot, rows, :] + gate_ref[0] * _rms(y, pg_ref[...])
        o_ref[rows, :] = x1
        h_ref[rows, :] = (_rms(x1, ng_ref[...]) * (1.0 + sc_ref[0]) + sh_ref[0]).astype(h_ref.dtype)


def matmul_resnorm(a, w, x, mod3, post_gain, next_gain, grp, tm):
    t, k = a.shape
    d = w.shape[1]
    row = pl.BlockSpec((tm, d), lambda i: (i, 0))
    vec = pl.BlockSpec((1, d), lambda i: (0, 0))
    return pl.pallas_call(
        _mm_res_kernel,
        grid=(t // tm,),
        in_specs=[
            pl.BlockSpec(memory_space=pl.ANY),
            pl.BlockSpec((k, d), lambda i: (0, 0)),
            pl.BlockSpec(memory_space=pl.ANY),
            grp.mod_spec(2, tm, d),
            vec,
            vec,
            grp.mod_spec(3, tm, d),
            grp.mod_spec(4, tm, d),
        ],
        out_specs=[row, row],
        out_shape=[jax.ShapeDtypeStruct((t, d), F32), jax.ShapeDtypeStruct((t, d), BF16)],
        scratch_shapes=[pltpu.VMEM((RING_SLOTS, tm, k), BF16), pltpu.VMEM((RING_SLOTS, tm, d), F32),
                        pltpu.SemaphoreType.DMA((2, RING_SLOTS))],
        compiler_params=_params("arbitrary"),
        name="out_proj",
    )(a, w, x, mod3, post_gain.reshape(1, d), next_gain.reshape(1, d), mod3, mod3)


def _mlp_kernel(h_ref, x_ref, w1_ref, w2_ref, gate_ref, pg_ref, o_ref, acc_scr):
    j = pl.program_id(1)

    @pl.when(j == 0)
    def _():
        acc_scr[...] = jnp.zeros_like(acc_scr)

    u = _dot(h_ref[...], w1_ref[...].astype(BF16))
    u = jnp.square(jnp.maximum(u, 0.0)).astype(BF16)
    acc_scr[...] += _dot(u, w2_ref[...].astype(BF16))

    @pl.when(j == pl.num_programs(1) - 1)
    def _():
        o_ref[...] = x_ref[...] + gate_ref[0] * _rms(acc_scr[...], pg_ref[...])


def mlp_block(h, x, post_gain, mod3, w1, w2, layer, grp, tm, tf):
    t, d = x.shape
    f = w1.shape[2]
    return pl.pallas_call(
        _mlp_kernel,
        grid=(t // tm, f // tf),
        in_specs=[
            pl.BlockSpec((tm, d), lambda i, j: (i, 0)),
            pl.BlockSpec((tm, d), lambda i, j: (i, 0)),
            pl.BlockSpec((None, d, tf), lambda i, j: (layer, 0, j)),
            pl.BlockSpec((None, tf, d), lambda i, j: (layer, j, 0)),
            grp.mod_spec(5, tm, d),
            pl.BlockSpec((1, d), lambda i, j: (0, 0)),
        ],
        out_specs=pl.BlockSpec((tm, d), lambda i, j: (i, 0)),
        out_shape=jax.ShapeDtypeStruct((t, d), F32),
        scratch_shapes=[pltpu.VMEM((tm, d), F32)],
        compiler_params=_params("parallel", "arbitrary"),
        name="mlp",
    )(h, x, w1, w2, mod3, post_gain.reshape(1, d))


def _ret_kernel(dec_ref, q_ref, k_ref, v_ref, g_ref, gn_ref, *rest, n_chunks, has_state, emit_state):
    rest = list(rest)
    s0f_ref = s0b_ref = sf_ref = sb_ref = None
    if has_state:
        s0f_ref, s0b_ref = rest[:2]
        rest = rest[2:]
    o_ref = rest.pop(0)
    if emit_state:
        sf_ref, sb_ref = rest[:2]
        rest = rest[2:]
    st_f, st_b, of_scr, ob_scr, sw_f, sw_b, kw_f, kw_b = rest

    c = RET_CHUNK
    dk = q_ref.shape[-1]
    h = pl.program_id(1)
    lg_f = _log_sigmoid(jnp.full((1, 1), dec_ref[0, h], F32))
    lg_b = _log_sigmoid(jnp.full((1, 1), dec_ref[1, h], F32))
    ri = lax.broadcasted_iota(jnp.int32, (c, c), 0)
    ci = lax.broadcasted_iota(jnp.int32, (c, c), 1)
    rel = (ri - ci).astype(F32)
    intra_f = jnp.where(rel >= 0, jnp.exp(lg_f * jnp.maximum(rel, 0.0)), 0.0)
    intra_b = jnp.where(rel <= 0, jnp.exp(lg_b * jnp.maximum(-rel, 0.0)), 0.0)
    idx = lax.broadcasted_iota(jnp.int32, (c, 1), 0).astype(F32)
    qdec_f = jnp.exp(lg_f * (idx + 1.0))
    kdec_f = jnp.exp(lg_f * (c - 1.0 - idx))
    qdec_b = jnp.exp(lg_b * (c - idx))
    kdec_b = jnp.exp(lg_b * idx)
    cdec_f = jnp.exp(lg_f * c)
    cdec_b = jnp.exp(lg_b * c)
    q_scale = dk ** -0.5

    if has_state:
        st_f[...] = s0f_ref[0, 0, 0]
        st_b[...] = s0b_ref[0, 0, 0]
    else:
        st_f[...] = jnp.zeros_like(st_f)
        st_b[...] = jnp.zeros_like(st_b)

    def weigh(ch, intra, kdec, sw, kw_s):
        rows = pl.ds(_chunk_offset(ch, c), c)
        kc = k_ref[0, rows, :]
        sw[ch] = (_dot_nt(q_ref[0, rows, :], kc) * (intra * q_scale)).astype(BF16)
        kw_s[ch] = (kc.astype(F32) * kdec).T.astype(BF16)

    def advance(ch, st, out_scr, qdec, cdec, sw, kw_s):
        rows = pl.ds(_chunk_offset(ch, c), c)
        qc = q_ref[0, rows, :]
        vc = v_ref[0, rows, :]
        s_old = st[...]
        out_scr[rows, :] = _dot(sw[ch], vc) + _dot(qc, s_old.astype(BF16)) * (qdec * q_scale)
        st[...] = s_old * cdec + _dot(kw_s[ch], vc)

    def fin(ch):
        rows = pl.ds(_chunk_offset(ch, c), c)
        o = _rms(of_scr[rows, :] + ob_scr[rows, :], gn_ref[...])
        o_ref[0, rows, :] = (jax.nn.silu(g_ref[0, rows, :].astype(F32)) * o).astype(o_ref.dtype)

    def weigh_both(i):
        weigh(i, intra_f, kdec_f, sw_f, kw_f)
        weigh(n_chunks - 1 - i, intra_b, kdec_b, sw_b, kw_b)

    def advance_both(i):
        advance(i, st_f, of_scr, qdec_f, cdec_f, sw_f, kw_f)
        advance(n_chunks - 1 - i, st_b, ob_scr, qdec_b, cdec_b, sw_b, kw_b)

    _pipelined_scan(n_chunks, weigh_both, advance_both, fin)

    if emit_state:
        sf_ref[0, 0, 0] = st_f[...]
        sb_ref[0, 0, 0] = st_b[...]


def retention_core(proj, dec, gn, s0f, s0b, b, l, emit_state):
    hh = RET_HEADS
    width = proj.shape[-1]
    dk = width // (6 * hh)
    dv = 2 * dk
    has_state = s0f is not None
    n_chunks = l // RET_CHUNK
    in_specs = [
        pl.BlockSpec(memory_space=pltpu.SMEM),
        pl.BlockSpec((1, l, dk), lambda bi, h: (bi, 0, h)),
        pl.BlockSpec((1, l, dk), lambda bi, h: (bi, 0, hh + h)),
        pl.BlockSpec((1, l, dv), lambda bi, h: (bi, 0, hh + h)),
        pl.BlockSpec((1, l, dv), lambda bi, h: (bi, 0, 2 * hh + h)),
        pl.BlockSpec((1, dv), lambda bi, h: (0, h)),
    ]
    args = [dec, proj, proj, proj, proj, gn.reshape(1, hh * dv)]
    if has_state:
        st_spec = pl.BlockSpec((1, 1, 1, dk, dv), lambda bi, h: (bi, 0, h, 0, 0))
        in_specs += [st_spec, st_spec]
        args += [s0f, s0b]
    out_specs = [pl.BlockSpec((1, l, dv), lambda bi, h: (bi, 0, h))]
    out_shape = [jax.ShapeDtypeStruct((b, l, hh * dv), BF16)]
    if emit_state:
        so_spec = pl.BlockSpec((1, 1, 1, dk, dv), lambda bi, h: (bi, 0, h, 0, 0))
        out_specs += [so_spec, so_spec]
        out_shape += [jax.ShapeDtypeStruct((b, 1, hh, dk, dv), F32)] * 2
    return pl.pallas_call(
        functools.partial(_ret_kernel, n_chunks=n_chunks, has_state=has_state, emit_state=emit_state),
        grid=(b, hh),
        in_specs=in_specs,
        out_specs=out_specs,
        out_shape=out_shape,
        scratch_shapes=[pltpu.VMEM((dk, dv), F32), pltpu.VMEM((dk, dv), F32),
                        pltpu.VMEM((l, dv), F32), pltpu.VMEM((l, dv), F32)]
                       + [pltpu.VMEM((n_chunks, RET_CHUNK, RET_CHUNK), BF16)] * 2
                       + [pltpu.VMEM((n_chunks, dk, RET_CHUNK), BF16)] * 2,
        compiler_params=_params("parallel", "parallel"),
        name="retention",
    )(*args)


def _rope_rot(x, cos_t, sin_t):
    lane = lax.broadcasted_iota(jnp.int32, x.shape, x.ndim - 1)
    nxt = pltpu.roll(x, LANES - 1, axis=x.ndim - 1)
    prv = pltpu.roll(x, 1, axis=x.ndim - 1)
    swapped = jnp.where(jnp.bitwise_and(lane, 1) == 0, nxt, prv)
    return x * cos_t + swapped * sin_t


def _att_kernel(q_ref, k_ref, v_ref, qg_ref, kg_ref, *rest, rope, has_cache, emit_kv, kv_heads, groups):
    rest = list(rest)
    cosq_ref = sinq_ref = cosk_ref = sink_ref = ck_ref = cv_ref = kn_ref = vo_ref = None
    if rope:
        cosq_ref, sinq_ref, cosk_ref, sink_ref = rest[:4]
        rest = rest[4:]
    if has_cache:
        ck_ref, cv_ref = rest[:2]
        rest = rest[2:]
    w_ref, x_ref, gate_ref, pg_ref, ng_ref, sh_ref, sc_ref, x1_ref, h_ref = rest[:9]
    rest = rest[9:]
    if emit_kv:
        kn_ref, vo_ref = rest[:2]
        rest = rest[2:]
    k_scr, v_scr, o_scr = rest
    hd = ATT_HD
    exp2_scale = hd ** -0.5 * math.log2(math.e)

    @pl.when(pl.program_id(1) == 0)
    def _():
        for kh in range(kv_heads):
            cols = slice(kh * hd, (kh + 1) * hd)
            kn = _rms(k_ref[0, :, cols].astype(F32), kg_ref[...])
            if emit_kv:
                kn_ref[0, :, cols] = kn
            if rope:
                kn = _rope_rot(kn, cosk_ref[...], sink_ref[...])
            k_scr[kh] = kn.astype(BF16)
            v_scr[kh, :, :hd] = v_ref[0, :, cols].astype(BF16)
            v_scr[kh, :, hd:] = jnp.ones((v_scr.shape[1], hd), BF16)
        if emit_kv:
            vo_ref[0] = v_ref[0]

    if has_cache:
        ck = [ck_ref[0, :, kh * hd:(kh + 1) * hd].astype(BF16) for kh in range(kv_heads)]
        cv = [jnp.concatenate([cv_ref[0, :, kh * hd:(kh + 1) * hd].astype(BF16),
                               jnp.ones((cv_ref.shape[1], hd), BF16)], axis=1) for kh in range(kv_heads)]

    def scores(head):
        kh = head // groups
        qn = _rms(q_ref[0, :, head * hd:(head + 1) * hd].astype(F32), qg_ref[...])
        if rope:
            qn = _rope_rot(qn, cosq_ref[...], sinq_ref[...])
        qb = qn.astype(BF16)
        return _dot_nt(qb, k_scr[kh]), (_dot_nt(qb, ck[kh]) if has_cache else None)

    heads = kv_heads * groups
    nxt = scores(0)
    for head in range(heads):
        kh = head // groups
        s1, s2 = nxt
        if head + 1 < heads:
            nxt = scores(head + 1)
        m = jnp.max(s1, axis=-1, keepdims=True)
        if has_cache:
            m = jnp.maximum(m, jnp.max(s2, axis=-1, keepdims=True))
        p1 = jnp.exp2((s1 - m) * exp2_scale)
        nd = _dot(p1.astype(BF16), v_scr[kh])
        if has_cache:
            p2 = jnp.exp2((s2 - m) * exp2_scale)
            nd = nd + _dot(p2.astype(BF16), cv[kh])
        o_scr[:, head * hd:(head + 1) * hd] = (nd[:, :hd] * (1.0 / nd[:, hd:])).astype(o_scr.dtype)

    y = _dot(o_scr[...], w_ref[...])
    x1 = x_ref[0] + gate_ref[0] * _rms(y, pg_ref[...])
    x1_ref[0] = x1
    h_ref[0] = (_rms(x1, ng_ref[...]) * (1.0 + sc_ref[0]) + sh_ref[0]).astype(h_ref.dtype)


def attention_block(proj, x, q_gain, k_gain, rope_tabs, cache_k, cache_v, w_out, mod3, post_gain, next_gain,
                    grp, tq, emit_kv):
    hd, kv, heads = ATT_HD, ATT_KV, ATT_HEADS
    b, l, d = x.shape
    rope = rope_tabs is not None
    has_cache = cache_k is not None
    kv_blk = heads // kv
    assert heads % kv == 0
    row0, per_batch = grp.row0, grp.per_batch

    def mod_spec(chunk):
        return pl.BlockSpec((1, 1, d), lambda bi, qi: (row0 + bi if per_batch else row0, 0, chunk))

    vec = lambda n: pl.BlockSpec((1, n), lambda bi, qi: (0, 0))
    in_specs = [
        pl.BlockSpec((1, tq, heads * hd), lambda bi, qi: (bi, qi, 0)),
        pl.BlockSpec((1, l, kv * hd), lambda bi, qi: (bi, 0, kv_blk)),
        pl.BlockSpec((1, l, kv * hd), lambda bi, qi: (bi, 0, kv_blk + 1)),
        vec(hd),
        vec(hd),
    ]
    args = [proj, proj, proj, q_gain.reshape(1, hd), k_gain.reshape(1, hd)]
    if rope:
        cos_t, sin_t = rope_tabs
        in_specs += [
            pl.BlockSpec((tq, hd), lambda bi, qi: (qi, 0)),
            pl.BlockSpec((tq, hd), lambda bi, qi: (qi, 0)),
            pl.BlockSpec((l, hd), lambda bi, qi: (0, 0)),
            pl.BlockSpec((l, hd), lambda bi, qi: (0, 0)),
        ]
        args += [cos_t, sin_t, cos_t, sin_t]
    if has_cache:
        past = cache_k.shape[1]
        c_spec = pl.BlockSpec((1, past, kv * hd), lambda bi, qi: (bi, 0, 0))
        in_specs += [c_spec, c_spec]
        args += [cache_k, cache_v]
    row = pl.BlockSpec((1, tq, d), lambda bi, qi: (bi, qi, 0))
    in_specs += [pl.BlockSpec((heads * hd, d), lambda bi, qi: (0, 0)), row,
                 mod_spec(2), vec(d), vec(d), mod_spec(3), mod_spec(4)]
    args += [w_out, x, mod3, post_gain.reshape(1, d), next_gain.reshape(1, d), mod3, mod3]
    out_specs = [row, row]
    out_shape = [jax.ShapeDtypeStruct((b, l, d), F32), jax.ShapeDtypeStruct((b, l, d), BF16)]
    if emit_kv:
        kv_spec = pl.BlockSpec((1, l, kv * hd), lambda bi, qi: (bi, 0, 0))
        out_specs += [kv_spec, kv_spec]
        out_shape += [jax.ShapeDtypeStruct((b, l, kv * hd), F32)] * 2
    return pl.pallas_call(
        functools.partial(_att_kernel, rope=rope, has_cache=has_cache, emit_kv=emit_kv,
                          kv_heads=kv, groups=heads // kv),
        grid=(b, l // tq),
        in_specs=in_specs,
        out_specs=out_specs,
        out_shape=out_shape,
        scratch_shapes=[pltpu.VMEM((kv, l, hd), BF16), pltpu.VMEM((kv, l, 2 * hd), BF16),
                        pltpu.VMEM((tq, heads * hd), BF16)],
        compiler_params=_params("parallel", "arbitrary"),
        name="attention",
    )(*args)


def _rope_tables(l):
    rows = l // GRID_W
    row = jnp.repeat(jnp.arange(rows, dtype=F32), GRID_W)
    col = jnp.tile(jnp.arange(GRID_W, dtype=F32), rows)
    half = ATT_HD // 2
    inv = ROPE_THETA ** (-jnp.arange(0, half, 2, dtype=F32) / half)
    ang = jnp.concatenate([row[:, None] * inv, col[:, None] * inv], axis=-1)
    cos_t = jnp.repeat(jnp.cos(ang), 2, axis=-1)
    sin_h = jnp.sin(ang)
    sin_t = jnp.stack([-sin_h, sin_h], axis=-1).reshape(l, ATT_HD)
    return cos_t, sin_t


def _ml_kernel(q_ref, k_ref, v_ref, og_ref, gt_ref, gn_ref, *rest, n_chunks, has_state, emit_state):
    rest = list(rest)
    c0f_ref = n0f_ref = m0f_ref = c0b_ref = n0b_ref = m0b_ref = None
    if has_state:
        c0f_ref, n0f_ref, m0f_ref, c0b_ref, n0b_ref, m0b_ref = rest[:6]
        rest = rest[6:]
    o_ref = rest.pop(0)
    outs = None
    if emit_state:
        outs = rest[:6]
        rest = rest[6:]
    cm_f, cm_b, nv_f, nv_b, hf_scr, hb_scr, fl_f, fl_b, br_f, br_b, mo_f, mo_b, mn_f, mn_b = rest[:14]
    sw_f, sw_b, kw_f, kw_b = rest[14:18]
    vt_f, vt_b = rest[18:23], rest[23:28]

    c = CHUNK
    dqk = q_ref.shape[-1]
    k_scale = dqk ** -0.5
    ri = lax.broadcasted_iota(jnp.int32, (c, c), 0)
    ci = lax.broadcasted_iota(jnp.int32, (c, c), 1)
    mask_f = ci <= ri
    mask_b = ci >= ri
    assert c == LANES and dqk == LANES
    dv_tiles = v_ref.shape[-1] // LANES

    def bf3(m):
        m = jnp.where(m, 1.0, 0.0).astype(BF16)
        return jnp.concatenate([m, m, m], axis=1), jnp.concatenate([m, m, m], axis=0)

    mf3, mf3_t = bf3(mask_f)
    mb3, mb3_t = bf3(mask_b)
    eye3, _ = bf3(ci == ri)

    def split3(x):
        hi = x.astype(BF16)
        r1 = x - hi.astype(F32)
        mid = r1.astype(BF16)
        lo = (r1 - mid.astype(F32)).astype(BF16)
        return jnp.concatenate([hi, mid, lo], axis=1)

    def wide(x):
        return jnp.concatenate([x] * dv_tiles, axis=1)

    if has_state:
        cm_f[...] = c0f_ref[0, 0, 0]
        cm_b[...] = c0b_ref[0, 0, 0]
        nv_f[...] = n0f_ref[0, 0]
        nv_b[...] = n0b_ref[0, 0]
        m0_f = jnp.broadcast_to(m0f_ref[0, 0], (1, LANES))
        m0_b = jnp.broadcast_to(m0b_ref[0, 0], (1, LANES))
    else:
        for r in (cm_f, cm_b, nv_f, nv_b):
            r[...] = jnp.zeros_like(r)
        m0_f = m0_b = jnp.zeros((1, LANES), F32)

    def gate_pass(row_i, row_f, m3_t, m0, fl, br, mo, mn, order):
        f_all = _log_sigmoid(gt_ref[0, 0, row_f])
        fl[...] = f_all
        b_all = _dot(split3(f_all), m3_t)
        br[...] = b_all
        b_end = jnp.sum(f_all, axis=-1, keepdims=True)
        w_max = jnp.max(b_end - b_all + gt_ref[0, 0, row_i], axis=-1, keepdims=True)
        m = m0
        for r in order:
            mo[r:r + 1, :] = m
            m = jnp.maximum(b_end[r:r + 1, :] + m, w_max[r:r + 1, :])
            mn[r:r + 1, :] = m
        return m

    m_last_f = gate_pass(0, 1, mb3_t, m0_f, fl_f, br_f, mo_f, mn_f, range(n_chunks))
    m_last_b = gate_pass(2, 3, mf3_t, m0_b, fl_b, br_b, mo_b, mn_b, reversed(range(n_chunks)))

    def weigh(ch, mask, m3, fl, br, mo, mn, row_i, sw, kw_s, wx_s, rs_s, em_s, cd_s, ks_s):
        rows = pl.ds(_chunk_offset(ch, c), c)
        one = pl.ds(ch, 1)
        qc = q_ref[0, rows, :]
        kc = k_ref[0, rows, :]
        i_row = gt_ref[0, 0, row_i, one, :]
        f_row = fl[one, :]
        b_row = br[one, :]
        m_old = mo[one, :]
        m_new = mn[one, :]
        b_q = _dot_nt(m3, jnp.broadcast_to(split3(f_row), (LANES, 3 * c)))
        i_q = _dot_nt(eye3, jnp.broadcast_to(split3(i_row), (LANES, 3 * c)))
        dlog = jnp.where(mask, b_q - b_row + i_row, -jnp.inf)
        inter = b_q + m_old
        m_q = jnp.maximum(inter, jnp.max(dlog, axis=-1, keepdims=True))
        s = _dot_nt(qc, kc) * (jnp.exp(dlog - m_q) * k_scale)
        sw[ch] = s.astype(BF16)
        rs_s[rows, :] = jnp.broadcast_to(jnp.sum(s, axis=-1, keepdims=True), (c, LANES))
        wx_s[rows, :] = jnp.exp(inter - m_q)
        em_s[rows, :] = jnp.exp(-m_q)
        b_end = jnp.sum(f_row, axis=-1, keepdims=True)
        wlog = b_end - b_q + i_q
        cd_s[one, :] = jnp.exp(b_end + m_old - m_new)
        kw = kc.astype(F32) * (jnp.exp(wlog - m_new) * k_scale)
        kw_s[ch] = kw.T.astype(BF16)
        ks_s[one, :] = jnp.sum(kw, axis=0, keepdims=True)

    def advance(ch, cm, nv, out_scr, sw, kw_s, wx_s, rs_s, em_s, cd_s, ks_s):
        rows = pl.ds(_chunk_offset(ch, c), c)
        one = pl.ds(ch, 1)
        qc = q_ref[0, rows, :]
        vc = v_ref[0, rows, :]
        c_old = cm[...]
        n_old = nv[...]
        w_x = wx_s[rows, :]
        num = _dot(sw[ch], vc) + _dot(qc, c_old.astype(BF16)) * wide(w_x)
        qn = _dot_nt(qc, jnp.broadcast_to(n_old, (LANES, dqk)).astype(BF16))
        den = jnp.maximum(jnp.abs(rs_s[rows, :] + qn * w_x), em_s[rows, :])
        out_scr[rows, :] = num * wide(1.0 / den)
        carry_dec = cd_s[one, :]
        cm[...] = c_old * wide(carry_dec) + _dot(kw_s[ch], vc)
        nv[...] = n_old * carry_dec + ks_s[one, :]

    def fin(ch):
        rows = pl.ds(_chunk_offset(ch, c), c)
        hn = _rms(hf_scr[rows, :] + hb_scr[rows, :], gn_ref[...])
        o_ref[0, rows, :] = (jax.nn.sigmoid(og_ref[0, rows, :].astype(F32)) * hn).astype(o_ref.dtype)

    tmp_f = (sw_f, kw_f) + tuple(vt_f)
    tmp_b = (sw_b, kw_b) + tuple(vt_b)

    def weigh_both(i):
        weigh(i, mask_f, mf3, fl_f, br_f, mo_f, mn_f, 0, *tmp_f)
        weigh(n_chunks - 1 - i, mask_b, mb3, fl_b, br_b, mo_b, mn_b, 2, *tmp_b)

    def advance_both(i):
        advance(i, cm_f, nv_f, hf_scr, *tmp_f)
        advance(n_chunks - 1 - i, cm_b, nv_b, hb_scr, *tmp_b)

    _pipelined_scan(n_chunks, weigh_both, advance_both, fin)

    if emit_state:
        for dst, src in zip(outs[0::3], (cm_f, cm_b)):
            dst[0, 0, 0] = src[...]
        for dst, src in zip(outs[1::3], (nv_f, nv_b)):
            dst[0, 0] = src[...]
        for dst, m_last in zip(outs[2::3], (m_last_f, m_last_b)):
            dst[0, 0] = m_last[:, 0:1]


def mlstm_core(proj, gates_t, gn, state, b, l, emit_state):
    hh = ML_HEADS
    width = proj.shape[-1]
    dqk = width // (6 * hh)
    dv = 2 * dqk
    has_state = state is not None
    n_chunks = l // CHUNK
    chunk_rows = gates_t.shape[3]
    in_specs = [
        pl.BlockSpec((1, l, dqk), lambda bi, h: (bi, 0, h)),
        pl.BlockSpec((1, l, dqk), lambda bi, h: (bi, 0, hh + h)),
        pl.BlockSpec((1, l, dv), lambda bi, h: (bi, 0, hh + h)),
        pl.BlockSpec((1, l, dv), lambda bi, h: (bi, 0, 2 * hh + h)),
        pl.BlockSpec((1, 1, 4, chunk_rows, CHUNK), lambda bi, h: (bi, h, 0, 0, 0)),
        pl.BlockSpec((1, dv), lambda bi, h: (0, h)),
    ]
    args = [proj, proj, proj, proj, gates_t, gn.reshape(1, hh * dv)]
    c_spec_in = pl.BlockSpec((1, 1, 1, dqk, dv), lambda bi, h: (bi, 0, h, 0, 0))
    n_spec = pl.BlockSpec((1, 1, 1, dqk), lambda bi, h: (bi, h, 0, 0))
    m_spec = pl.BlockSpec((1, 1, 1, 1), lambda bi, h: (bi, h, 0, 0))
    if has_state:
        cf, nf, mf, cb, nb, mb = state
        in_specs += [c_spec_in, n_spec, m_spec, c_spec_in, n_spec, m_spec]
        args += [cf, nf.reshape(b, hh, 1, dqk), mf.reshape(b, hh, 1, 1),
                 cb, nb.reshape(b, hh, 1, dqk), mb.reshape(b, hh, 1, 1)]
    out_specs = [pl.BlockSpec((1, l, dv), lambda bi, h: (bi, 0, h))]
    out_shape = [jax.ShapeDtypeStruct((b, l, hh * dv), BF16)]
    if emit_state:
        out_specs += [c_spec_in, n_spec, m_spec] * 2
        out_shape += [jax.ShapeDtypeStruct((b, 1, hh, dqk, dv), F32),
                      jax.ShapeDtypeStruct((b, hh, 1, dqk), F32),
                      jax.ShapeDtypeStruct((b, hh, 1, 1), F32)] * 2
    return pl.pallas_call(
        functools.partial(_ml_kernel, n_chunks=n_chunks, has_state=has_state, emit_state=emit_state),
        grid=(b, hh),
        in_specs=in_specs,
        out_specs=out_specs,
        out_shape=out_shape,
        scratch_shapes=[pltpu.VMEM((dqk, dv), F32), pltpu.VMEM((dqk, dv), F32),
                        pltpu.VMEM((1, dqk), F32), pltpu.VMEM((1, dqk), F32),
                        pltpu.VMEM((l, dv), F32), pltpu.VMEM((l, dv), F32)]
                       + [pltpu.VMEM((chunk_rows, CHUNK), F32)] * 4 + [pltpu.VMEM((chunk_rows, LANES), F32)] * 4
                       + [pltpu.VMEM((n_chunks, CHUNK, CHUNK), BF16)] * 2 + [pltpu.VMEM((n_chunks, dqk, CHUNK), BF16)] * 2
                       + [pltpu.VMEM((l, LANES), F32)] * 3 + [pltpu.VMEM((chunk_rows, LANES), F32)] * 2
                       + [pltpu.VMEM((l, LANES), F32)] * 3 + [pltpu.VMEM((chunk_rows, LANES), F32)] * 2,
        compiler_params=_params("parallel", "parallel"),
        name="mlstm",
    )(*args)


def _hy_filter_kernel(feat_ref, w1_ref, b1_ref, w2_ref, b2_ref, w3_ref, fr_ref, win_ref, sum_ref, dif_ref):
    d = win_ref.shape[-1]
    z = jnp.sin(fr_ref[0:1, :] * (_dot(feat_ref[...].astype(BF16), w1_ref[...].astype(BF16)) + b1_ref[...]))
    z = jnp.sin(fr_ref[1:2, :] * (_dot(z.astype(BF16), w2_ref[...].astype(BF16)) + b2_ref[...]))
    filt = _dot(z.astype(BF16), w3_ref[...].astype(BF16))
    win = win_ref[...]
    ff = filt[:, :d] * win
    fb = filt[:, d:] * win
    sum_ref[...] = (ff + fb).astype(sum_ref.dtype)
    dif_ref[...] = (ff - fb).astype(dif_ref.dtype)


def hyena_filters(l, d, w1, b1, w2, b2, w3, freq):
    t = jnp.linspace(0.0, 1.0, l, dtype=F32)[:, None]
    pos = jnp.arange(l, dtype=F32)[:, None]
    bands = jnp.linspace(1e-4, HY_BANDS - 1, HY_BANDS, dtype=F32)
    ang = 2.0 * math.pi * pos * bands / l
    feats = jnp.concatenate([t, jnp.cos(ang), -jnp.sin(ang)], axis=-1)
    emb = feats.shape[1]
    feats = jnp.pad(feats, ((0, 0), (0, LANES - emb)))
    w1p = jnp.pad(w1, ((0, LANES - emb), (0, 0)))
    deltas = jnp.abs(jnp.linspace(math.log(HY_TARGET) / HY_SLOW_DECAY,
                                  math.log(HY_TARGET) / HY_FAST_DECAY, d, dtype=F32))
    window = jnp.exp(-t * deltas)
    hid = w2.shape[0]
    tl = min(l, 512)
    full = lambda shp: pl.BlockSpec(shp, lambda i: (0,) * len(shp))
    return pl.pallas_call(
        _hy_filter_kernel,
        grid=(l // tl,),
        in_specs=[
            pl.BlockSpec((tl, LANES), lambda i: (i, 0)),
            full((LANES, hid)), full((1, hid)), full((hid, hid)), full((1, hid)), full((hid, 2 * d)),
            full((2, hid)),
            pl.BlockSpec((tl, d), lambda i: (i, 0)),
        ],
        out_specs=[pl.BlockSpec((tl, d), lambda i: (i, 0))] * 2,
        out_shape=[jax.ShapeDtypeStruct((l, d), BF16)] * 2,
        compiler_params=_params("parallel"),
        name="hyena_filter",
    )(feats, w1p, b1.reshape(1, hid), w2, b2.reshape(1, hid), w3, freq, window)


def _dft_tables(l):
    n = 2 * l
    blk = 64
    assert l % blk == 0

    def trig(step, count):
        k = lax.broadcasted_iota(jnp.int32, (l, count), 0)
        t = lax.broadcasted_iota(jnp.int32, (l, count), 1)
        ang = (((2 * k + 1) * step * t) % (2 * n)).astype(F32) * (math.pi / n)
        return jnp.cos(ang), jnp.sin(ang)

    c1, s1 = trig(blk, l // blk)
    c0, s0 = trig(1, blk)
    cos_kt = (c1[:, :, None] * c0[:, None, :] - s1[:, :, None] * s0[:, None, :]).reshape(l, l).astype(BF16)
    sin_kt = (s1[:, :, None] * c0[:, None, :] + c1[:, :, None] * s0[:, None, :]).reshape(l, l).astype(BF16)
    c1t, s1t, c0t, s0t = c1.T, s1.T, c0.T, s0.T
    cos_tk = (c1t[:, None, :] * c0t[None, :, :] - s1t[:, None, :] * s0t[None, :, :]).reshape(l, l).astype(BF16)
    sin_tk = (s1t[:, None, :] * c0t[None, :, :] + c1t[:, None, :] * s0t[None, :, :]).reshape(l, l).astype(BF16)
    return cos_kt, sin_kt, cos_tk, sin_tk


def _hy_spec_kernel(c_ref, s_ref, fs_ref, fd_ref, bias_ref, gr_ref, gs_ref):
    gr_ref[...] = _dot(c_ref[...], fs_ref[...]) + bias_ref[...]
    gs_ref[...] = _dot(s_ref[...], fd_ref[...])


def hyena_filter_spectrum(cos_kt, sin_kt, f_sum, f_dif, f_bias):
    l, d = f_sum.shape
    tm = min(l, 512)
    tn = min(d, 512)
    return pl.pallas_call(
        _hy_spec_kernel,
        grid=(l // tm, d // tn),
        in_specs=[
            pl.BlockSpec((tm, l), lambda i, j: (i, 0)),
            pl.BlockSpec((tm, l), lambda i, j: (i, 0)),
            pl.BlockSpec((l, tn), lambda i, j: (0, j)),
            pl.BlockSpec((l, tn), lambda i, j: (0, j)),
            pl.BlockSpec((1, tn), lambda i, j: (0, j)),
        ],
        out_specs=[pl.BlockSpec((tm, tn), lambda i, j: (i, j))] * 2,
        out_shape=[jax.ShapeDtypeStruct((l, d), F32)] * 2,
        compiler_params=_params("parallel", "parallel"),
        name="hyena_filter_spectrum",
    )(cos_kt, sin_kt, f_sum, f_dif, f_bias.reshape(1, d))


def _hy_conv_kernel(p0_ref, p1_ref, pv_ref, w0_ref, w1_ref, wv_ref, b0_ref, b1_ref, bv_ref, z_ref, x0_ref):
    l = p0_ref.shape[1]
    t = lax.broadcasted_iota(jnp.int32, (l, 1), 0)

    def conv(p_ref, w_ref, b_ref):
        p = p_ref[0].astype(F32)
        prev = jnp.where(t == 0, 0.0, pltpu.roll(p, 1, axis=0))
        nxt = jnp.where(t == l - 1, 0.0, pltpu.roll(p, l - 1, axis=0))
        return b_ref[...] + prev * w_ref[0:1, :] + p * w_ref[1:2, :] + nxt * w_ref[2:3, :]

    x0_ref[0] = conv(p0_ref, w0_ref, b0_ref).astype(x0_ref.dtype)
    z_ref[0] = (conv(pv_ref, wv_ref, bv_ref) * conv(p1_ref, w1_ref, b1_ref)).astype(z_ref.dtype)


def hyena_short_conv(proj, conv_w, conv_b, b, l, d):
    tc = min(d, 512)
    nc = d // tc
    p_spec = lambda off: pl.BlockSpec((1, l, tc), lambda bi, j: (bi, 0, off * nc + j))
    w_spec = lambda off: pl.BlockSpec((HY_SHORT, tc), lambda bi, j: (0, off * nc + j))
    b_spec = lambda off: pl.BlockSpec((1, tc), lambda bi, j: (0, off * nc + j))
    cb = conv_b.reshape(1, 3 * d)
    return pl.pallas_call(
        _hy_conv_kernel,
        grid=(b, nc),
        in_specs=[p_spec(0), p_spec(1), p_spec(2), w_spec(0), w_spec(1), w_spec(2),
                  b_spec(0), b_spec(1), b_spec(2)],
        out_specs=[pl.BlockSpec((1, l, tc), lambda bi, j: (bi, 0, j))] * 2,
        out_shape=[jax.ShapeDtypeStruct((b, l, d), BF16), jax.ShapeDtypeStruct((b, l, d), BF16)],
        compiler_params=_params("parallel", "parallel"),
        name="hyena_short_conv",
    )(proj, proj, proj, conv_w, conv_w, conv_w, cb, cb, cb)


def _hy_fwd_kernel(c_ref, s_ref, z_ref, gr_ref, gs_ref, yr_ref, ys_ref):
    z = z_ref[0]
    zr = _dot(c_ref[...], z)
    zs = _dot(s_ref[...], z)
    gr = gr_ref[...]
    gs = gs_ref[...]
    yr_ref[0] = (zr * gr - zs * gs).astype(yr_ref.dtype)
    ys_ref[0] = (zr * gs + zs * gr).astype(ys_ref.dtype)


def hyena_forward_dft(cos_kt, sin_kt, z, g_r, g_s):
    b, l, d = z.shape
    tm = min(l, 512)
    tn = min(d, 1024)
    return pl.pallas_call(
        _hy_fwd_kernel,
        grid=(l // tm, b, d // tn),
        in_specs=[
            pl.BlockSpec((tm, l), lambda i, bi, j: (i, 0)),
            pl.BlockSpec((tm, l), lambda i, bi, j: (i, 0)),
            pl.BlockSpec((1, l, tn), lambda i, bi, j: (bi, 0, j)),
            pl.BlockSpec((tm, tn), lambda i, bi, j: (i, j)),
            pl.BlockSpec((tm, tn), lambda i, bi, j: (i, j)),
        ],
        out_specs=[pl.BlockSpec((1, tm, tn), lambda i, bi, j: (bi, i, j))] * 2,
        out_shape=[jax.ShapeDtypeStruct((b, l, d), BF16)] * 2,
        compiler_params=_params("parallel", "parallel", "parallel"),
        name="hyena_forward_dft",
    )(cos_kt, sin_kt, z, g_r, g_s)


def _hy_inv_kernel(ct_ref, st_ref, yr_ref, ys_ref, x0_ref, w_ref, x_ref, gate_ref, pg_ref, ng_ref, sh_ref, sc_ref,
                   x1_ref, h_ref, *, inv_scale):
    tm = ct_ref.shape[0]
    sub = min(tm, 256)
    for r in range(tm // sub):
        rows = slice(r * sub, (r + 1) * sub)
        z = _dot(ct_ref[rows, :], yr_ref[0]) + _dot(st_ref[rows, :], ys_ref[0])
        a = (z * inv_scale * x0_ref[0, rows, :]).astype(BF16)
        y = _dot(a, w_ref[...])
        x1 = x_ref[0, rows, :] + gate_ref[0] * _rms(y, pg_ref[...])
        x1_ref[0, rows, :] = x1
        h_ref[0, rows, :] = (_rms(x1, ng_ref[...]) * (1.0 + sc_ref[0]) + sh_ref[0]).astype(h_ref.dtype)


def hyena_inverse_block(cos_tk, sin_tk, y_r, y_s, x0, x, w_out, mod3, post_gain, next_gain, grp):
    b, l, d = y_r.shape
    tm = min(l, 512)
    row0, per_batch = grp.row0, grp.per_batch

    def mod_spec(chunk):
        return pl.BlockSpec((1, 1, d), lambda i, bi: (row0 + bi if per_batch else row0, 0, chunk))

    tab = pl.BlockSpec((tm, l), lambda i, bi: (i, 0))
    spec = pl.BlockSpec((1, l, d), lambda i, bi: (bi, 0, 0))
    row = pl.BlockSpec((1, tm, d), lambda i, bi: (bi, i, 0))
    vec = pl.BlockSpec((1, d), lambda i, bi: (0, 0))
    return pl.pallas_call(
        functools.partial(_hy_inv_kernel, inv_scale=1.0 / l),
        grid=(l // tm, b),
        in_specs=[tab, tab, spec, spec, row, pl.BlockSpec((d, d), lambda i, bi: (0, 0)), row,
                  mod_spec(2), vec, vec, mod_spec(3), mod_spec(4)],
        out_specs=[row, row],
        out_shape=[jax.ShapeDtypeStruct((b, l, d), F32), jax.ShapeDtypeStruct((b, l, d), BF16)],
        compiler_params=_params("parallel", "parallel"),
        name="hyena_inverse_dft",
    )(cos_tk, sin_tk, y_r, y_s, x0, w_out, x, mod3, post_gain.reshape(1, d), next_gain.reshape(1, d), mod3, mod3)


def _tile_rows(grp):
    span = grp.l if grp.per_batch else grp.t
    return next(tm for tm in (1024, 512, 256, 128) if span % tm == 0)


def kernel(x_prompt, x_sample, cache_k, cache_v, state_ret_fwd, state_ret_bwd, state_ml_C_fwd, state_ml_n_fwd, state_ml_m_fwd, state_ml_C_bwd, state_ml_n_bwd, state_ml_m_bwd, c, c_ctx, mod_w, mod_b, norm_mix_pre, norm_mix_post, norm_ffn_pre, norm_ffn_post, mlp_w1, mlp_w2, ret_w_in, ret_decay_fwd, ret_decay_bwd, ret_gn, ret_w_out, att_w_in, att_q_gain, att_k_gain, att_w_out, ml_w_in, ml_gate_b, ml_gn, ml_w_out, hy_w_in, hy_b_in, hy_conv_w, hy_conv_b, hy_f_w1, hy_f_b1, hy_f_w2, hy_f_b2, hy_f_w3, hy_sin_freq, hy_f_bias, hy_w_out):
    bp, lp, d = x_prompt.shape
    bs, ls, _ = x_sample.shape
    depth = mod_w.shape[0]
    n_mixers = 4
    mod_rows = 16
    assert 1 + bs <= mod_rows

    grp_p = Group(bp, lp, 0, False)
    grp_s = Group(bs, ls, 1, True)
    groups = (grp_p, grp_s)

    cond = jnp.concatenate([c_ctx[None, :], c, jnp.zeros((mod_rows - 1 - bs, d), F32)], axis=0)
    mod_all = adaln_all(cond, mod_w, mod_b)

    xs = [x_prompt.reshape(grp_p.t, d), x_sample.reshape(grp_s.t, d)]
    new_k = new_v = new_rf = new_rb = None
    new_ml = None

    for i in range(depth):
        mixer = i % n_mixers
        j = i // n_mixers
        mod3 = mod_all[i].reshape(mod_rows, 1, 6 * d)
        ys = []
        for gi, grp in enumerate(groups):
            x = xs[gi]
            tm = _tile_rows(grp)
            is_prompt = gi == 0
            if mixer == 0:
                w_in = ret_w_in[j].astype(BF16)
                proj = norm_matmul(x, norm_mix_pre[i], mod3, 0, 1, w_in, None, BF16, grp, tm)
                dec = jnp.stack([ret_decay_fwd[j], ret_decay_bwd[j]]).astype(F32)
                s0f = None if is_prompt else state_ret_fwd
                s0b = None if is_prompt else state_ret_bwd
                assert is_prompt or state_ret_fwd.shape[1] == 1
                res = retention_core(proj.reshape(grp.b, grp.l, -1), dec, ret_gn[j], s0f, s0b,
                                     grp.b, grp.l, emit_state=is_prompt)
                if is_prompt:
                    new_rf, new_rb = res[1], res[2]
                a = res[0].reshape(grp.t, -1)
                w_out = ret_w_out[j].astype(BF16)
            elif mixer == 1:
                w_in = att_w_in[j].astype(BF16)
                proj = norm_matmul(x, norm_mix_pre[i], mod3, 0, 1, w_in, None, F32 if is_prompt else BF16, grp, tm)
                proj = proj.reshape(grp.b, grp.l, -1)
                x3 = x.reshape(grp.b, grp.l, d)
                w_out = att_w_out[j].astype(BF16)
                tail = (w_out, mod3, norm_mix_post[i], norm_ffn_pre[i], grp, min(grp.l, 256))
                if is_prompt:
                    res = attention_block(proj, x3, att_q_gain[j], att_k_gain[j], None, None, None, *tail,
                                          emit_kv=True)
                    new_k = res[2].reshape(grp.b, 1, grp.l, ATT_KV, ATT_HD)
                    new_v = res[3].reshape(grp.b, 1, grp.l, ATT_KV, ATT_HD)
                else:
                    assert cache_k.shape[1] == 1
                    ck = cache_k.reshape(grp.b, cache_k.shape[2], ATT_KV * ATT_HD)
                    cv = cache_v.reshape(grp.b, cache_v.shape[2], ATT_KV * ATT_HD)
                    res = attention_block(proj, x3, att_q_gain[j], att_k_gain[j], _rope_tables(grp.l), ck, cv,
                                          *tail, emit_kv=False)
                a = None
                x, h_mlp = res[0].reshape(grp.t, d), res[1].reshape(grp.t, d)
            elif mixer == 2:
                hh = ML_HEADS
                n_main = ml_w_in.shape[2] - 4 * hh
                w_main = ml_w_in[j][:, :n_main].astype(BF16)
                w_gate = jnp.pad(ml_w_in[j][:, n_main:], ((0, 0), (0, LANES - 4 * hh))).astype(BF16)
                b_gate = jnp.pad(ml_gate_b[j], (0, LANES - 4 * hh))
                proj, gates = norm_matmul(x, norm_mix_pre[i], mod3, 0, 1, w_main, None, BF16, grp, tm,
                                          aux=(w_gate, b_gate))
                gates = gates[:, :4 * hh].reshape(grp.b, grp.l, 4, hh)
                n_chunks = grp.l // CHUNK
                gates_t = gates.transpose(0, 3, 2, 1).reshape(grp.b, hh, 4, n_chunks, CHUNK)
                gates_t = jnp.pad(gates_t, ((0, 0), (0, 0), (0, 0), (0, -n_chunks % 16), (0, 0)))
                state = None
                if not is_prompt:
                    assert state_ml_C_fwd.shape[1] == 1
                    state = (state_ml_C_fwd, state_ml_n_fwd, state_ml_m_fwd,
                             state_ml_C_bwd, state_ml_n_bwd, state_ml_m_bwd)
                res = mlstm_core(proj.reshape(grp.b, grp.l, -1), gates_t, ml_gn[j], state,
                                 grp.b, grp.l, emit_state=is_prompt)
                if is_prompt:
                    dqk = res[2].shape[-1]
                    new_ml = (res[1], res[2].reshape(grp.b, 1, hh, dqk), res[3].reshape(grp.b, 1, hh),
                              res[4], res[5].reshape(grp.b, 1, hh, dqk), res[6].reshape(grp.b, 1, hh))
                a = res[0].reshape(grp.t, -1)
                w_out = ml_w_out[j].astype(BF16)
            else:
                w_in = hy_w_in[j].astype(BF16)
                proj = norm_matmul(x, norm_mix_pre[i], mod3, 0, 1, w_in, hy_b_in[j], BF16, grp, tm)
                z, x0 = hyena_short_conv(proj.reshape(grp.b, grp.l, 3 * d), hy_conv_w[j], hy_conv_b[j],
                                         grp.b, grp.l, d)
                f_sum, f_dif = hyena_filters(grp.l, d, hy_f_w1[j], hy_f_b1[j], hy_f_w2[j], hy_f_b2[j],
                                             hy_f_w3[j], hy_sin_freq[j])
                cos_kt, sin_kt, cos_tk, sin_tk = _dft_tables(grp.l)
                g_r, g_s = hyena_filter_spectrum(cos_kt, sin_kt, f_sum, f_dif, hy_f_bias[j])
                y_r, y_s = hyena_forward_dft(cos_kt, sin_kt, z, g_r, g_s)
                res = hyena_inverse_block(cos_tk, sin_tk, y_r, y_s, x0, x.reshape(grp.b, grp.l, d),
                                          hy_w_out[j].astype(BF16), mod3, norm_mix_post[i], norm_ffn_pre[i], grp)
                a = None
                x, h_mlp = res[0].reshape(grp.t, d), res[1].reshape(grp.t, d)
            if a is not None:
                x, h_mlp = matmul_resnorm(a, w_out, x, mod3, norm_mix_post[i], norm_ffn_pre[i], grp, tm)
            x = mlp_block(h_mlp, x, norm_ffn_post[i], mod3,
                          mlp_w1, mlp_w2, i, grp, tm, MLP_TILE_F)
            ys.append(x)
        xs = ys

    y_prompt = xs[0].reshape(bp, lp, d)
    y_sample = xs[1].reshape(bs, ls, d)
    return (y_prompt, y_sample, new_k, new_v, new_rf, new_rb) + tuple(new_ml)
```
